```python
import jax, jax.numpy as jnp
from jax import lax
import numpy as np

D_MODEL = 2048
BATCH = 2
SEQ = 4096
DEPTH = 4
DEC_BATCH = 32
DEC_SEQ = 32
PAST_LEN = 1024

CHUNK = 64
N_A = DEPTH // 2
N_B = DEPTH - N_A
MIX_W = D_MODEL
N_MEM = 256
MEM_HEADS = 4
MEM_HD = 128
MEM_W = MEM_HEADS * MEM_HD
TOK_W = MIX_W - MEM_W
RWKV_HD = 64
RWKV_H = TOK_W // RWKV_HD
W_LORA = 64
A_LORA = 64
V_LORA = 64
G_LORA = 224
A_SPLITS = (TOK_W, 2 * TOK_W, 3 * TOK_W, 3 * TOK_W + W_LORA, 3 * TOK_W + W_LORA + A_LORA)
A_SHIFT = 3 * TOK_W + W_LORA + A_LORA + G_LORA
A_IN = A_SHIFT + MEM_W
SWA_HD = 64
SWA_H = TOK_W // SWA_HD
SWA_KV = 4
SWA_G = SWA_H // SWA_KV
WINDOW = 128
WIN_CHUNKS = WINDOW // CHUNK
D_FF = 4 * D_MODEL
RMS_EPS = 1e-6
GN_EPS = 64e-5
NEG = -1e30

kernel_name = 'yoco_rwkv7_swa_sink_stream_step'


def rmsnorm(x, g):
    xf = x.astype(jnp.float32)
    y = xf * lax.rsqrt(jnp.mean(xf * xf, axis=-1, keepdims=True) + RMS_EPS)
    return (y * g.astype(jnp.float32)).astype(x.dtype)


def sqrelu_mlp(h, w_up, w_down):
    return jnp.square(jax.nn.relu(h @ w_up)) @ w_down


def memory_kv(mem, g, w_kv):
    hm = rmsnorm(mem[None], g[:, None, None, :])
    kv = jnp.einsum('lbmd,lde->lbme', hm, w_kv)
    L, B, M, _ = kv.shape
    k = kv[..., :MEM_W].reshape(L, B, M, MEM_HEADS, MEM_HD)
    v = kv[..., MEM_W:].reshape(L, B, M, MEM_HEADS, MEM_HD)
    return k, v


def mem_attend(q, k, v):
    B, T, _ = q.shape
    qh = q.reshape(B, T, MEM_HEADS, MEM_HD)
    s = jnp.einsum('bthd,bmhd->bhtm', qh, k).astype(jnp.float32) * (MEM_HD ** -0.5)
    p = jax.nn.softmax(s, axis=-1).astype(v.dtype)
    return jnp.einsum('bhtm,bmhd->bthd', p, v).reshape(B, T, MEM_W)


def wkv_scan(r, w, k, v, a, b, s0):
    def step(S, inp):
        r_t, w_t, k_t, v_t, a_t, b_t = inp
        sa = jnp.einsum('bhvk,bhk->bhv', S, a_t)
        S = S * w_t[:, :, None, :] + sa[..., None] * b_t[:, :, None, :] + v_t[..., None] * k_t[:, :, None, :]
        return S, jnp.einsum('bhvk,bhk->bhv', S, r_t)
    xs = tuple(jnp.swapaxes(t, 0, 1) for t in (r, w, k, v, a, b))
    S, ys = lax.scan(step, s0, xs)
    return jnp.swapaxes(ys, 0, 1), S


def rwkv_mix(p, prev, s0, v_first, mu, w0, w2, a0, a2, g2, k_k, k_a, r_k, lnx_g, lnx_b, vres):
    B, T, _ = p.shape
    f32 = jnp.float32
    pf = p.astype(f32)
    p_prev = jnp.concatenate([prev.astype(f32), pf[:, :-1]], axis=1)
    m = pf + (p_prev - pf) * mu
    r, k, v, xw, xa, xg = jnp.split(m, A_SPLITS, axis=-1)
    w_log = -jax.nn.softplus(-(w0 + jnp.tanh(xw) @ w2)) - 0.5
    a = jax.nn.sigmoid(a0 + xa @ a2)
    g = jax.nn.sigmoid(xg) @ g2
    if vres is not None:
        v0, v1, v2 = vres
        v = v + (v_first - v) * jax.nn.sigmoid(v0 + (v @ v1) @ v2)
    heads = lambda t: t.reshape(B, T, RWKV_H, RWKV_HD)
    kk = heads(k * k_k)
    kk = kk / jnp.maximum(jnp.sqrt(jnp.sum(kk * kk, axis=-1, keepdims=True)), 1e-12)
    k = k * (1.0 + (a - 1.0) * k_a)
    rh, kh, vh, ah = heads(r), heads(k), heads(v), heads(a)
    decay = jnp.exp(-jnp.exp(heads(w_log)))
    y, S = wkv_scan(rh, decay, kh, vh, -kk, kk * ah, s0.astype(f32))
    mean = jnp.mean(y, axis=-1, keepdims=True)
    var = jnp.var(y, axis=-1, keepdims=True)
    y = ((y - mean) * lax.rsqrt(var + GN_EPS)).reshape(B, T, TOK_W) * lnx_g + lnx_b
    bonus = jnp.sum(rh * kh * r_k, axis=-1, keepdims=True) * vh
    out = (y + bonus.reshape(B, T, TOK_W)) * g
    return out.astype(p.dtype), S.astype(s0.dtype), p[:, -1:], v


def sink_softmax(s, sink):
    m = jnp.maximum(jnp.max(s, axis=-1, keepdims=True), sink)
    e = jnp.exp(s - m)
    return e / (jnp.sum(e, axis=-1, keepdims=True) + jnp.exp(sink - m))


def swa_prompt(q, k, v, sinks):
    B, S, _ = q.shape
    nC = S // CHUNK
    KB = (WIN_CHUNKS + 1) * CHUNK
    qb = q.reshape(B, nC, CHUNK, SWA_KV, SWA_G, SWA_HD)
    pad = ((0, 0), (WIN_CHUNKS * CHUNK, 0), (0, 0), (0, 0))
    def band(t):
        tc = jnp.pad(t, pad).reshape(B, nC + WIN_CHUNKS, CHUNK, SWA_KV, SWA_HD)
        return jnp.concatenate([tc[:, j:j + nC] for j in range(WIN_CHUNKS + 1)], axis=2)
    kb, vb = band(k), band(v)
    s = jnp.einsum('bcqkgd,bcskd->bckgqs', qb, kb).astype(jnp.float32) * (SWA_HD ** -0.5)
    key_chunk = jnp.arange(nC)[:, None] - WIN_CHUNKS + jnp.arange(KB)[None, :] // CHUNK
    valid = key_chunk >= 0
    s = jnp.where(valid[None, :, None, None, None, :], s, NEG)
    sink = sinks.astype(jnp.float32).reshape(SWA_KV, SWA_G)[None, None, :, :, None, None]
    p = sink_softmax(s, sink).astype(vb.dtype)
    o = jnp.einsum('bckgqs,bcskd->bcqkgd', p, vb)
    return o.reshape(B, S, TOK_W)


def swa_sample(q, k_all, v_all, sinks):
    B, T, _ = q.shape
    qh = q.reshape(B, T, SWA_KV, SWA_G, SWA_HD)
    s = jnp.einsum('btkgd,bskd->bkgts', qh, k_all).astype(jnp.float32) * (SWA_HD ** -0.5)
    sink = sinks.astype(jnp.float32).reshape(SWA_KV, SWA_G)[None, :, :, None, None]
    p = sink_softmax(s, sink).astype(v_all.dtype)
    o = jnp.einsum('bkgts,bskd->btkgd', p, v_all)
    return o.reshape(B, T, TOK_W)


def _trunk(x, mem_k, mem_v, wkv0, shift0, past_k, past_v, P):
    B, T, _ = x.shape
    new_wkv, new_shift = [], []
    v_first = None
    k_new = v_new = k_all = v_all = None
    for l in range(DEPTH):
        h = rmsnorm(x, P['ln_mix_pre'][l])
        if l < N_A:
            i = l
            proj = h @ P['a_w_in'][i]
            vres = None if i == 0 else (P['a_v0'][i - 1], P['a_v1'][i - 1], P['a_v2'][i - 1])
            tok, s_new, last_row, v = rwkv_mix(
                proj[..., :A_SHIFT], shift0[i], wkv0[i], v_first, P['a_mu'][i], P['a_w0'][i], P['a_w2'][i],
                P['a_a0'][i], P['a_a2'][i], P['a_g2'][i], P['a_k_k'][i], P['a_k_a'][i], P['a_r_k'][i],
                P['a_lnx_g'][i], P['a_lnx_b'][i], vres)
            if i == 0:
                v_first = v
            new_wkv.append(s_new)
            new_shift.append(last_row)
            q_mem = proj[..., A_SHIFT:]
            w_out = P['a_w_out'][i]
        else:
            i = l - N_A
            if k_new is None:
                kv = rmsnorm(x, P['kv_norm_g']) @ P['w_kv']
                k_new = kv[..., :SWA_KV * SWA_HD].reshape(B, T, SWA_KV, SWA_HD)
                v_new = kv[..., SWA_KV * SWA_HD:].reshape(B, T, SWA_KV, SWA_HD)
                if past_k is not None:
                    k_all = jnp.concatenate([past_k.astype(k_new.dtype), k_new], axis=1)
                    v_all = jnp.concatenate([past_v.astype(v_new.dtype), v_new], axis=1)
            proj = h @ P['b_w_in'][i]
            q_tok, q_mem = proj[..., :TOK_W], proj[..., TOK_W:]
            if past_k is None:
                tok = swa_prompt(q_tok, k_new, v_new, P['b_sinks'][i])
            else:
                tok = swa_sample(q_tok, k_all, v_all, P['b_sinks'][i])
            w_out = P['b_w_out'][i]
        mem_o = mem_attend(q_mem, mem_k[l], mem_v[l])
        mix = jnp.concatenate([tok, mem_o.astype(tok.dtype)], axis=-1) @ w_out
        x = x + rmsnorm(mix, P['ln_mix_post'][l])
        h = rmsnorm(x, P['ln_mlp_pre'][l])
        x = x + rmsnorm(sqrelu_mlp(h, P['w_up'][l], P['w_down'][l]), P['ln_mlp_post'][l])
    if past_k is None:
        win_k, win_v = k_new[:, -WINDOW:], v_new[:, -WINDOW:]
    else:
        L = past_k.shape[1]
        win_k, win_v = k_all[:, -L:], v_all[:, -L:]
    return x, jnp.stack(new_wkv), jnp.stack(new_shift), win_k, win_v


def setup_inputs(seed: int = 0) -> dict:
    key = jax.random.key(seed)
    ks = iter(jax.random.split(key, 48))
    def nrm(shape, scale=1.0):
        return jax.random.normal(next(ks), shape, jnp.float32) * scale
    def gain(shape):
        return 1.0 + nrm(shape, 0.05)
    win_cache = min(WINDOW, PAST_LEN)
    nv = max(N_A - 1, 0)
    return {
        'x_prompt': nrm((BATCH, SEQ, D_MODEL)),
        'x_sample': nrm((DEC_BATCH, DEC_SEQ, D_MODEL)),
        'mem_prompt': nrm((BATCH, N_MEM, D_MODEL)),
        'state_wkv': nrm((N_A, DEC_BATCH, RWKV_H, RWKV_HD, RWKV_HD), 0.3),
        'state_shift': nrm((N_A, DEC_BATCH, 1, A_SHIFT)),
        'cache_win_k': nrm((DEC_BATCH, win_cache, SWA_KV, SWA_HD)),
        'cache_win_v': nrm((DEC_BATCH, win_cache, SWA_KV, SWA_HD)),
        'cache_mem_k': nrm((DEPTH, DEC_BATCH, N_MEM, MEM_HEADS, MEM_HD)),
        'cache_mem_v': nrm((DEPTH, DEC_BATCH, N_MEM, MEM_HEADS, MEM_HD)),
        'ln_mix_pre': gain((DEPTH, D_MODEL)),
        'ln_mix_post': gain((DEPTH, D_MODEL)),
        'ln_mlp_pre': gain((DEPTH, D_MODEL)),
        'ln_mlp_post': gain((DEPTH, D_MODEL)),
        'mem_norm_g': gain((DEPTH, D_MODEL)),
        'w_mem_kv': nrm((DEPTH, D_MODEL, 2 * MEM_W), D_MODEL ** -0.5),
        'w_up': nrm((DEPTH, D_MODEL, D_FF), D_MODEL ** -0.5),
        'w_down': nrm((DEPTH, D_FF, D_MODEL), D_FF ** -0.5),
        'a_w_in': nrm((N_A, D_MODEL, A_IN), D_MODEL ** -0.5),
        'a_mu': jax.random.uniform(next(ks), (N_A, A_SHIFT), jnp.float32),
        'a_w0': -1.0 + nrm((N_A, TOK_W), 0.5),
        'a_w2': nrm((N_A, W_LORA, TOK_W), 0.1 * W_LORA ** -0.5),
        'a_a0': nrm((N_A, TOK_W), 0.1),
        'a_a2': nrm((N_A, A_LORA, TOK_W), 0.1 * A_LORA ** -0.5),
        'a_g2': nrm((N_A, G_LORA, TOK_W), G_LORA ** -0.5),
        'a_k_k': 0.85 + nrm((N_A, TOK_W), 0.05),
        'a_k_a': gain((N_A, TOK_W)),
        'a_r_k': nrm((N_A, RWKV_H, RWKV_HD), 0.1),
        'a_lnx_g': gain((N_A, TOK_W)),
        'a_lnx_b': nrm((N_A, TOK_W), 0.02),
        'a_v0': nrm((nv, TOK_W), 0.1),
        'a_v1': nrm((nv, TOK_W, V_LORA), TOK_W ** -0.5),
        'a_v2': nrm((nv, V_LORA, TOK_W), 0.1 * V_LORA ** -0.5),
        'a_w_out': nrm((N_A, MIX_W, D_MODEL), MIX_W ** -0.5),
        'kv_norm_g': gain((D_MODEL,)),
        'w_kv': nrm((D_MODEL, 2 * SWA_KV * SWA_HD), D_MODEL ** -0.5),
        'b_w_in': nrm((N_B, D_MODEL, TOK_W + MEM_W), D_MODEL ** -0.5),
        'b_sinks': nrm((N_B, SWA_H), 0.5),
        'b_w_out': nrm((N_B, MIX_W, D_MODEL), MIX_W ** -0.5),
    }


def reference(x_prompt, x_sample, mem_prompt, state_wkv, state_shift, cache_win_k, cache_win_v,
              cache_mem_k, cache_mem_v, ln_mix_pre, ln_mix_post, ln_mlp_pre, ln_mlp_post, mem_norm_g,
              w_mem_kv, w_up, w_down, a_w_in, a_mu, a_w0, a_w2, a_a0, a_a2, a_g2, a_k_k, a_k_a, a_r_k,
              a_lnx_g, a_lnx_b, a_v0, a_v1, a_v2, a_w_out, kv_norm_g, w_kv, b_w_in, b_sinks, b_w_out):
    P = {
        'ln_mix_pre': ln_mix_pre, 'ln_mix_post': ln_mix_post, 'ln_mlp_pre': ln_mlp_pre,
        'ln_mlp_post': ln_mlp_post, 'w_up': w_up, 'w_down': w_down,
        'a_w_in': a_w_in, 'a_mu': a_mu, 'a_w0': a_w0, 'a_w2': a_w2, 'a_a0': a_a0, 'a_a2': a_a2,
        'a_g2': a_g2, 'a_k_k': a_k_k, 'a_k_a': a_k_a, 'a_r_k': a_r_k, 'a_lnx_g': a_lnx_g,
        'a_lnx_b': a_lnx_b, 'a_v0': a_v0, 'a_v1': a_v1, 'a_v2': a_v2, 'a_w_out': a_w_out,
        'kv_norm_g': kv_norm_g, 'w_kv': w_kv, 'b_w_in': b_w_in, 'b_sinks': b_sinks, 'b_w_out': b_w_out,
    }
    B = x_prompt.shape[0]
    mem_k_prompt, mem_v_prompt = memory_kv(mem_prompt, mem_norm_g, w_mem_kv)
    wkv0 = jnp.zeros((N_A, B, RWKV_H, RWKV_HD, RWKV_HD), x_prompt.dtype)
    shift0 = jnp.zeros((N_A, B, 1, A_SHIFT), x_prompt.dtype)
    y_prompt, wkv_prompt, shift_prompt, win_k_prompt, win_v_prompt = _trunk(
        x_prompt, mem_k_prompt, mem_v_prompt, wkv0, shift0, None, None, P)
    y_sample, wkv_sample, shift_sample, win_k_sample, win_v_sample = _trunk(
        x_sample, cache_mem_k, cache_mem_v, state_wkv, state_shift, cache_win_k, cache_win_v, P)
    return (y_prompt, y_sample, wkv_prompt, shift_prompt, win_k_prompt, win_v_prompt, mem_k_prompt, mem_v_prompt,
            wkv_sample, shift_sample, win_k_sample, win_v_sample)
```

```python
import functools

import jax
import jax.numpy as jnp
from jax import lax
from jax.experimental import pallas as pl
from jax.experimental.pallas import tpu as pltpu

F32 = jnp.float32
BF16 = jnp.bfloat16

D_MODEL = 2048
DEPTH = 4
N_A = 2
CHUNK = 64
N_MEM = 256
MEM_HEADS = 4
MEM_HD = 128
MEM_W = MEM_HEADS * MEM_HD
TOK_W = D_MODEL - MEM_W
HEAD_D = 64
N_HEADS = TOK_W // HEAD_D
N_PAIRS = N_HEADS // 2
W_LORA = 64
A_LORA = 64
G_LORA = 224
A_SHIFT = 3 * TOK_W + W_LORA + A_LORA + G_LORA
LORA_OFF = 3 * TOK_W
GATE_OFF = LORA_OFF + 128
SHIFT_PAD = 4992
A_QMEM_OFF = 5120
A_IN_PAD = A_QMEM_OFF + MEM_W
SWA_KV = 4
SWA_G = N_HEADS // SWA_KV
WINDOW = 128
D_FF = 4 * D_MODEL
RMS_EPS = 1e-6
GN_EPS = 64e-5
NEG = -1e30
LANES = 128
VMEM_LIMIT = 56 * 1024 * 1024

_NT = (((1,), (1,)), ((), ()))
_TN = (((0,), (0,)), ((), ()))


def _cparams(*sem):
    return pltpu.CompilerParams(dimension_semantics=sem, vmem_limit_bytes=VMEM_LIMIT)


def _bdot(a, b):
    return jnp.dot(a.astype(BF16), b.astype(BF16), preferred_element_type=F32)


def _bdot_nt(a, b):
    return lax.dot_general(a.astype(BF16), b.astype(BF16), _NT, preferred_element_type=F32)


def _bdot_tn(a, b):
    return lax.dot_general(a.astype(BF16), b.astype(BF16), _TN, preferred_element_type=F32)


def _rms(x, g):
    return x * lax.rsqrt(jnp.mean(x * x, axis=-1, keepdims=True) + RMS_EPS) * g


def _norm_matmul_body(x_ref, g_ref, w_ref, o_ref, xn_ref):
    @pl.when(pl.program_id(1) == 0)
    def _():
        xn_ref[...] = _rms(x_ref[...], g_ref[...]).astype(BF16)

    o_ref[...] = jnp.dot(xn_ref[...], w_ref[...], preferred_element_type=F32).astype(o_ref.dtype)


def norm_matmul(x, g, w, tm, tn, out_dtype=F32):
    m, d = x.shape
    n = w.shape[1]
    return pl.pallas_call(
        _norm_matmul_body,
        grid=(m // tm, n // tn),
        in_specs=[pl.BlockSpec((tm, d), lambda i, j: (i, 0)),
                  pl.BlockSpec((1, d), lambda i, j: (0, 0)),
                  pl.BlockSpec((d, tn), lambda i, j: (0, j))],
        out_specs=pl.BlockSpec((tm, tn), lambda i, j: (i, j)),
        out_shape=jax.ShapeDtypeStruct((m, n), out_dtype),
        scratch_shapes=[pltpu.VMEM((tm, d), BF16)],
        compiler_params=_cparams("parallel", "arbitrary"),
        name="norm_matmul",
    )(x, g.reshape(1, d), w)


def _mix_out_body(tok_ref, mem_ref, wt_ref, wm_ref, g_ref, x_ref, o_ref):
    y = jnp.dot(tok_ref[...], wt_ref[...], preferred_element_type=F32)
    y = y + jnp.dot(mem_ref[...], wm_ref[...], preferred_element_type=F32)
    o_ref[...] = x_ref[...] + _rms(y, g_ref[...])


def mix_out(tok, mem, w_out, g, x, tm):
    m, d = x.shape
    return pl.pallas_call(
        _mix_out_body,
        grid=(m // tm,),
        in_specs=[pl.BlockSpec((tm, TOK_W), lambda i: (i, 0)),
                  pl.BlockSpec((tm, MEM_W), lambda i: (i, 0)),
                  pl.BlockSpec((TOK_W, d), lambda i: (0, 0)),
                  pl.BlockSpec((MEM_W, d), lambda i: (0, 0)),
                  pl.BlockSpec((1, d), lambda i: (0, 0)),
                  pl.BlockSpec((tm, d), lambda i: (i, 0))],
        out_specs=pl.BlockSpec((tm, d), lambda i: (i, 0)),
        out_shape=jax.ShapeDtypeStruct((m, d), F32),
        compiler_params=_cparams("parallel"),
        name="mix_out",
    )(tok, mem, w_out[:TOK_W], w_out[TOK_W:], g.reshape(1, d), x)


def _mlp_body(x_ref, g1_ref, wu_ref, wd_ref, g2_ref, o_ref, xn_ref, acc_ref):
    f = pl.program_id(1)

    @pl.when(f == 0)
    def _():
        xn_ref[...] = _rms(x_ref[...], g1_ref[...]).astype(BF16)
        acc_ref[...] = jnp.zeros_like(acc_ref)

    h = jnp.maximum(jnp.dot(xn_ref[...], wu_ref[...], preferred_element_type=F32), 0.0)
    acc_ref[...] += jnp.dot((h * h).astype(BF16), wd_ref[...], preferred_element_type=F32)

    @pl.when(f == pl.num_programs(1) - 1)
    def _():
        o_ref[...] = x_ref[...] + _rms(acc_ref[...], g2_ref[...])


def mlp(x, g1, w_up, w_down, g2, tm, tf):
    m, d = x.shape
    ff = w_up.shape[1]
    return pl.pallas_call(
        _mlp_body,
        grid=(m // tm, ff // tf),
        in_specs=[pl.BlockSpec((tm, d), lambda i, f: (i, 0)),
                  pl.BlockSpec((1, d), lambda i, f: (0, 0)),
                  pl.BlockSpec((d, tf), lambda i, f: (0, f)),
                  pl.BlockSpec((tf, d), lambda i, f: (f, 0)),
                  pl.BlockSpec((1, d), lambda i, f: (0, 0))],
        out_specs=pl.BlockSpec((tm, d), lambda i, f: (i, 0)),
        out_shape=jax.ShapeDtypeStruct((m, d), F32),
        scratch_shapes=[pltpu.VMEM((tm, d), BF16), pltpu.VMEM((tm, d), F32)],
        compiler_params=_cparams("parallel", "arbitrary"),
        name="mlp",
    )(x, g1.reshape(1, d), w_up, w_down, g2.reshape(1, d))


def _mem_attn_body(q_ref, k_ref, v_ref, o_ref):
    scale = MEM_HD ** -0.5
    for h in range(MEM_HEADS):
        sl = slice(h * MEM_HD, (h + 1) * MEM_HD)
        s = _bdot_nt(q_ref[:, sl], k_ref[:, sl]) * scale
        e = jnp.exp(s - jnp.max(s, axis=-1, keepdims=True))
        p = e / jnp.sum(e, axis=-1, keepdims=True)
        o_ref[:, sl] = _bdot(p, v_ref[:, sl]).astype(o_ref.dtype)


def mem_attn(proj, q_col_block, mem_k, mem_v, rows_per_seq, tr):
    m = proj.shape[0]
    tiles_per_seq = rows_per_seq // tr
    return pl.pallas_call(
        _mem_attn_body,
        grid=(m // tr,),
        in_specs=[pl.BlockSpec((tr, MEM_W), lambda i: (i, q_col_block)),
                  pl.BlockSpec((None, N_MEM, MEM_W), lambda i: (i // tiles_per_seq, 0, 0)),
                  pl.BlockSpec((None, N_MEM, MEM_W), lambda i: (i // tiles_per_seq, 0, 0))],
        out_specs=pl.BlockSpec((tr, MEM_W), lambda i: (i, 0)),
        out_shape=jax.ShapeDtypeStruct((m, MEM_W), BF16),
        compiler_params=_cparams("parallel"),
        name="mem_attn",
    )(proj, mem_k, mem_v)


def _swa_core(q_ref, k_tiles, v_tiles, valid, sink_ref, o_ref, lq):
    lane = lax.broadcasted_iota(jnp.int32, (lq, LANES), 1)
    low = lane < HEAD_D
    for j in range(SWA_KV):
        xs = []
        for i in range(SWA_G // 2):
            c0 = (j * SWA_G + 2 * i) * HEAD_D
            pair = q_ref[:, c0:c0 + LANES] * (HEAD_D ** -0.5)
            xs.append(jnp.where(low, pair, 0.0).astype(BF16))
            xs.append(jnp.where(low, 0.0, pair).astype(BF16))
        x = jnp.concatenate(xs, axis=0)
        kj = jnp.concatenate([t.astype(BF16) for t in k_tiles(j)], axis=0)
        vj = jnp.concatenate([t.astype(BF16) for t in v_tiles(j)], axis=0)
        s = lax.dot_general(x, kj, _NT, preferred_element_type=F32)
        if valid is not None:
            s = jnp.where(valid, s, NEG)
        sink = sink_ref[j]
        mx = jnp.maximum(jnp.max(s, axis=-1, keepdims=True), sink)
        e = jnp.exp(s - mx)
        p = e / (jnp.sum(e, axis=-1, keepdims=True) + jnp.exp(sink - mx))
        o = jnp.dot(p.astype(BF16), vj, preferred_element_type=F32)
        for i in range(SWA_G // 2):
            c0 = (j * SWA_G + 2 * i) * HEAD_D
            top = o[(2 * i) * lq:(2 * i + 1) * lq]
            bot = o[(2 * i + 1) * lq:(2 * i + 2) * lq]
            o_ref[:, c0:c0 + LANES] = jnp.where(low, top, bot).astype(o_ref.dtype)


def _swa_prompt_body(q_ref, k0, k1, k2, v0, v1, v2, sink_ref, o_ref):
    c = pl.program_id(1)
    col = lax.broadcasted_iota(jnp.int32, (SWA_G * CHUNK, 3 * CHUNK), 1)
    valid = col >= jnp.maximum(2 - c, 0) * CHUNK
    _swa_core(q_ref,
              lambda j: [r[:, j * LANES:(j + 1) * LANES] for r in (k0, k1, k2)],
              lambda j: [r[:, j * LANES:(j + 1) * LANES] for r in (v0, v1, v2)],
              valid, sink_ref, o_ref, CHUNK)


def swa_prompt(proj, kvd, sink_col, batch, seq):
    nc = seq // CHUNK
    kd_w = SWA_KV * LANES

    def kv_spec(back, col):
        return pl.BlockSpec((CHUNK, kd_w), lambda b, c: (b * nc + jnp.maximum(c - back, 0), col))

    return pl.pallas_call(
        _swa_prompt_body,
        grid=(batch, nc),
        in_specs=[pl.BlockSpec((CHUNK, TOK_W), lambda b, c: (b * nc + c, 0)),
                  kv_spec(2, 0), kv_spec(1, 0), kv_spec(0, 0),
                  kv_spec(2, 1), kv_spec(1, 1), kv_spec(0, 1),
                  pl.BlockSpec((SWA_KV, SWA_G * CHUNK, 1), lambda b, c: (0, 0, 0))],
        out_specs=pl.BlockSpec((CHUNK, TOK_W), lambda b, c: (b * nc + c, 0)),
        out_shape=jax.ShapeDtypeStruct((batch * seq, TOK_W), BF16),
        compiler_params=_cparams("parallel", "arbitrary"),
        name="swa_prompt",
    )(proj, kvd, kvd, kvd, kvd, kvd, kvd, sink_col)


def _swa_sample_body(q_ref, k_ref, v_ref, sink_ref, o_ref, *, lq):
    _swa_core(q_ref,
              lambda j: [k_ref[:, j * LANES:(j + 1) * LANES]],
              lambda j: [v_ref[:, j * LANES:(j + 1) * LANES]],
              None, sink_ref, o_ref, lq)


def swa_sample(proj, k_all, v_all, sink_col, batch, lq):
    keys = k_all.shape[1]
    kd_w = SWA_KV * LANES
    return pl.pallas_call(
        functools.partial(_swa_sample_body, lq=lq),
        grid=(batch,),
        in_specs=[pl.BlockSpec((lq, TOK_W), lambda b: (b, 0)),
                  pl.BlockSpec((None, keys, kd_w), lambda b: (b, 0, 0)),
                  pl.BlockSpec((None, keys, kd_w), lambda b: (b, 0, 0)),
                  pl.BlockSpec((SWA_KV, SWA_G * lq, 1), lambda b: (0, 0, 0))],
        out_specs=pl.BlockSpec((lq, TOK_W), lambda b: (b, 0)),
        out_shape=jax.ShapeDtypeStruct((batch * lq, TOK_W), BF16),
        compiler_params=_cparams("parallel"),
        name="swa_sample",
    )(proj, k_all, v_all, sink_col)


def _pair_ones():
    r = lax.broadcasted_iota(jnp.int32, (LANES, LANES), 0)
    c = lax.broadcasted_iota(jnp.int32, (LANES, LANES), 1)
    return jnp.where((r < HEAD_D) == (c < HEAD_D), 1.0, 0.0).astype(BF16)


def _head_sum(x, ones_bd):
    hi = x.astype(BF16)
    lo = (x - hi.astype(F32)).astype(BF16)
    return (jnp.dot(hi, ones_bd, preferred_element_type=F32)
            + jnp.dot(lo, ones_bd, preferred_element_type=F32))


def _rwkv_pre_body(p_ref, prev_ref, mu_ref, w0_ref, w2_ref, a0_ref, a2_ref, g2_ref, kk_ref, ka_ref,
                   rk_ref, *rest, has_vres):
    if has_vres:
        vf_ref, v0_ref, v1_ref, v2_ref = rest[:4]
        outs = rest[4:]
    else:
        outs = rest
    r_o, lw_o, k_o, v_o, kn_o, b_o, g_o, bonus_o = outs
    tr = p_ref.shape[0]
    row0 = lax.broadcasted_iota(jnp.int32, (tr, LANES), 0) == 0
    ones_bd = _pair_ones()

    def mixed(c0, width):
        outs_ = []
        for t in range(width // LANES):
            sl = slice(c0 + t * LANES, c0 + (t + 1) * LANES)
            p = p_ref[:, sl]
            pp = jnp.where(row0, prev_ref[:, sl], pltpu.roll(p, 1, 0))
            outs_.append(p + (pp - p) * mu_ref[:, sl])
        return outs_

    lora = mixed(LORA_OFF, LANES)[0]
    gate_in = jnp.concatenate(mixed(GATE_OFF, SHIFT_PAD - GATE_OFF), axis=1)
    w_pre = w0_ref[...] + _bdot(jnp.tanh(lora), w2_ref[...])
    lw_o[...] = -jnp.exp(-jax.nn.softplus(-w_pre) - 0.5)
    a = jax.nn.sigmoid(a0_ref[...] + _bdot(lora, a2_ref[...]))
    g_o[...] = _bdot(jax.nn.sigmoid(gate_in), g2_ref[...])

    r_t = mixed(0, TOK_W)
    k_t = mixed(TOK_W, TOK_W)
    v_t = mixed(2 * TOK_W, TOK_W)
    if has_vres:
        v_all = jnp.concatenate(v_t, axis=1)
        gate = jax.nn.sigmoid(v0_ref[...] + _bdot(_bdot(v_all, v1_ref[...]), v2_ref[...]))
        v_all = v_all + (vf_ref[...] - v_all) * gate
        v_t = [v_all[:, t * LANES:(t + 1) * LANES] for t in range(N_PAIRS)]
    for t in range(N_PAIRS):
        sl = slice(t * LANES, (t + 1) * LANES)
        kraw = k_t[t]
        kk = kraw * kk_ref[:, sl]
        kk = kk / jnp.maximum(jnp.sqrt(_head_sum(kk * kk, ones_bd)), 1e-12)
        a_t = a[:, sl]
        k = kraw * (1.0 + (a_t - 1.0) * ka_ref[:, sl])
        v = v_t[t]
        r_o[:, sl] = r_t[t]
        k_o[:, sl] = k
        v_o[:, sl] = v
        kn_o[:, sl] = kk
        b_o[:, sl] = kk * a_t
        bonus_o[:, sl] = _head_sum(r_t[t] * k * rk_ref[:, sl], ones_bd) * v


def rwkv_pre(proj, prev, pw, v_first, tr):
    m = proj.shape[0]
    has_vres = v_first is not None
    row = lambda w: pl.BlockSpec((1, w), lambda i: (0, 0))
    full = lambda a: pl.BlockSpec(a.shape, lambda i: (0,) * a.ndim)
    tile = pl.BlockSpec((tr, TOK_W), lambda i: (i, 0))
    in_specs = [pl.BlockSpec((tr, A_IN_PAD), lambda i: (i, 0)),
                pl.BlockSpec((None, 1, A_IN_PAD), lambda i: (i, 0, 0)),
                row(SHIFT_PAD), row(TOK_W), full(pw["w2"]), row(TOK_W), full(pw["a2"]), full(pw["g2"]),
                row(TOK_W), row(TOK_W), row(TOK_W)]
    args = [proj, prev, pw["mu"], pw["w0"], pw["w2"], pw["a0"], pw["a2"], pw["g2"], pw["k_k"],
            pw["k_a"], pw["r_k"]]
    if has_vres:
        in_specs += [tile, row(TOK_W), full(pw["v1"]), full(pw["v2"])]
        args += [v_first, pw["v0"], pw["v1"], pw["v2"]]
    return pl.pallas_call(
        functools.partial(_rwkv_pre_body, has_vres=has_vres),
        grid=(m // tr,),
        in_specs=in_specs,
        out_specs=[tile] * 8,
        out_shape=[jax.ShapeDtypeStruct((m, TOK_W), F32)] * 8,
        compiler_params=_cparams("parallel"),
        name="rwkv_pre",
    )(*args)


def _wkv_body(r_ref, lw_ref, k_ref, v_ref, kn_ref, b_ref, g_ref, bonus_ref, s0_ref, lg_ref, lb_ref,
              o_ref, s_out_ref, s_ref, *, chunk):
    c = pl.program_id(1)
    L = chunk
    W2 = 2 * L

    @pl.when(c == 0)
    def _():
        s_ref[...] = s0_ref[...]

    t_i = lax.broadcasted_iota(jnp.int32, (L, W2), 0)
    l_i = lax.broadcasted_iota(jnp.int32, (L, W2), 1)
    j_i = jnp.where(l_i >= L, l_i - L, l_i)
    strict = t_i > j_i
    incl = t_i >= j_i
    eye2 = jnp.where(t_i == j_i, 1.0, 0.0)
    levels = []
    s = 1
    while s < L:
        sh = s.bit_length() - 1
        levels.append(((t_i >> (sh + 1)) == (j_i >> (sh + 1)))
                      & (((t_i >> sh) & 1) == 1) & (((j_i >> sh) & 1) == 0))
        s *= 2
    first_t = l_i < L
    lane = lax.broadcasted_iota(jnp.int32, (L, LANES), 1)
    low = lane < HEAD_D
    rr = lax.broadcasted_iota(jnp.int32, (LANES, LANES), 0)
    cc = lax.broadcasted_iota(jnp.int32, (LANES, LANES), 1)
    bd = (rr < HEAD_D) == (cc < HEAD_D)
    ones_bd = jnp.where(bd, 1.0, 0.0).astype(BF16)
    tri = jnp.where(lax.broadcasted_iota(jnp.int32, (L, L), 0)
                    >= lax.broadcasted_iota(jnp.int32, (L, L), 1), 1.0, 0.0)

    def stack_t(x):
        return jnp.concatenate([jnp.where(first_t, x, 0.0), jnp.where(first_t, 0.0, x)],
                               axis=0).astype(BF16)

    def stack_d(x):
        return jnp.concatenate([jnp.where(low, x, 0.0), jnp.where(low, 0.0, x)], axis=0).astype(BF16)

    def tdot(m, stacked):
        return jnp.dot(m.astype(BF16), stacked, preferred_element_type=F32)

    cum_all = jnp.dot(tri, lw_ref[...], precision=lax.Precision.HIGHEST, preferred_element_type=F32)

    for p in range(N_PAIRS):
        sl = slice(p * LANES, (p + 1) * LANES)
        lw = lw_ref[:, sl]
        cum = cum_all[:, sl]
        cum_l = cum[L - 1:L, :]
        p_in = jnp.exp(cum)
        p_inv = jnp.exp(-cum)
        p_end = jnp.exp(cum_l - cum)
        kn = kn_ref[:, sl]
        b = b_ref[:, sl]
        k = k_ref[:, sl]
        v = v_ref[:, sl]
        a_t = -kn * jnp.exp(cum - lw)
        r_t = r_ref[:, sl] * p_in
        x = jnp.concatenate([a_t, r_t], axis=0).astype(BF16)
        cb = lax.dot_general(x, stack_d(b * p_inv), _NT, preferred_element_type=F32)
        ck = lax.dot_general(x, stack_d(k * p_inv), _NT, preferred_element_type=F32)
        a_ab = jnp.where(strict, cb[:L], 0.0)
        a_ak = jnp.where(strict, ck[:L], 0.0)
        a_rb = jnp.where(incl, cb[L:], 0.0)
        a_rk = jnp.where(incl, ck[L:], 0.0)
        t_m = eye2
        for msk in levels:
            ta = tdot(t_m, stack_t(jnp.where(msk, a_ab, 0.0)))
            t_m = t_m + tdot(ta, stack_t(t_m))
        sv = stack_d(v)
        t_b = t_m.astype(BF16)
        w1 = jnp.dot(t_b, stack_d(a_t), preferred_element_type=F32)
        w2 = jnp.dot(t_b, stack_d(tdot(a_ak, sv)), preferred_element_type=F32)
        q1 = r_t + tdot(a_rb, stack_d(w1))
        q2 = tdot(a_rb, stack_d(w2)) + tdot(a_rk, sv)
        bh = b * p_end
        m1 = jnp.where(bd, _bdot_tn(w1, bh), 0.0)
        m2 = jnp.where(bd, _bdot_tn(w2, bh) + _bdot_tn(v, k * p_end), 0.0)
        s_p = s_ref[p]
        y = _bdot_nt(q1, s_p) + q2
        s_ref[p] = s_p * jnp.exp(cum_l) + _bdot(s_p, m1) + m2
        mean = _head_sum(y, ones_bd) * (1.0 / HEAD_D)
        d = y - mean
        var = _head_sum(d * d, ones_bd) * (1.0 / HEAD_D)
        yn = d * lax.rsqrt(var + GN_EPS) * lg_ref[:, sl] + lb_ref[:, sl]
        o_ref[:, sl] = ((yn + bonus_ref[:, sl]) * g_ref[:, sl]).astype(o_ref.dtype)

    @pl.when(c == pl.num_programs(1) - 1)
    def _():
        s_out_ref[...] = s_ref[...]


def wkv(r, lw, k, v, kn, b, g, bonus, s0, lnx_g, lnx_b, batch, seq, chunk):
    nc = seq // chunk
    tile = pl.BlockSpec((chunk, TOK_W), lambda bi, c: (bi * nc + c, 0))
    state = pl.BlockSpec((None, N_PAIRS, LANES, LANES), lambda bi, c: (bi, 0, 0, 0))
    row = pl.BlockSpec((1, TOK_W), lambda bi, c: (0, 0))
    return pl.pallas_call(
        functools.partial(_wkv_body, chunk=chunk),
        grid=(batch, nc),
        in_specs=[tile] * 8 + [state, row, row],
        out_specs=[tile, state],
        out_shape=[jax.ShapeDtypeStruct((batch * seq, TOK_W), BF16),
                   jax.ShapeDtypeStruct((batch, N_PAIRS, LANES, LANES), F32)],
        scratch_shapes=[pltpu.VMEM((N_PAIRS, LANES, LANES), F32)],
        compiler_params=_cparams("parallel", "arbitrary"),
        name="wkv",
    )(r, lw, k, v, kn, b, g, bonus, s0, lnx_g.reshape(1, TOK_W), lnx_b.reshape(1, TOK_W))


def _state_to_pairs(s):
    b = s.shape[0]
    s = s.reshape(b, N_PAIRS, 2, HEAD_D, HEAD_D)
    z = jnp.zeros_like(s[:, :, 0])
    top = jnp.concatenate([s[:, :, 0], z], axis=-1)
    bot = jnp.concatenate([z, s[:, :, 1]], axis=-1)
    return jnp.concatenate([top, bot], axis=-2)


def _pairs_to_state(sp):
    b = sp.shape[0]
    h0 = sp[:, :, :HEAD_D, :HEAD_D]
    h1 = sp[:, :, HEAD_D:, HEAD_D:]
    return jnp.stack([h0, h1], axis=2).reshape(b, N_HEADS, HEAD_D, HEAD_D)


def _dup_heads(w):
    lead = w.shape[:-1]
    w = w.reshape(lead + (SWA_KV, 1, HEAD_D))
    return jnp.broadcast_to(w, lead + (SWA_KV, 2, HEAD_D)).reshape(lead + (SWA_KV * LANES,))


def _undup_heads(x):
    lead = x.shape[:-1]
    return x.reshape(lead + (SWA_KV, 2, HEAD_D))[..., 0, :]


def _pad_cols(a, width):
    return jnp.pad(a, [(0, 0)] * (a.ndim - 1) + [(0, width - a.shape[-1])])


def _sink_col(sinks, lq):
    return jnp.repeat(sinks.astype(F32).reshape(SWA_KV, SWA_G), lq, axis=1)[..., None]


def _trunk(x, mem_k, mem_v, wkv0, shift0, past_k, past_v, P, batch, seq, tm, tr_pre, tr_mem):
    m = batch * seq
    chunk = min(CHUNK, seq)
    new_wkv, new_shift = [], []
    v_first = None
    kvd = None
    for l in range(DEPTH):
        if l < N_A:
            pw = P["a"][l]
            proj = norm_matmul(x, P["ln_mix_pre"][l], pw["w_in"], tm, 512)
            p3 = proj.reshape(batch, seq // tr_pre, tr_pre, A_IN_PAD)
            first = _pad_cols(shift0[l].astype(F32), A_IN_PAD).reshape(batch, 1, A_IN_PAD)
            prev = jnp.concatenate([first, p3[:, :-1, -1, :]], axis=1).reshape(m // tr_pre, 1, A_IN_PAD)
            new_shift.append(p3[:, -1, -1, :A_SHIFT].reshape(batch, 1, A_SHIFT))
            r, lw, k, v, kn, b, g, bonus = rwkv_pre(proj, prev, pw, v_first, tr_pre)
            if l == 0:
                v_first = v
            tok, s_new = wkv(r, lw, k, v, kn, b, g, bonus, _state_to_pairs(wkv0[l].astype(F32)),
                             pw["lnx_g"], pw["lnx_b"], batch, seq, chunk)
            new_wkv.append(_pairs_to_state(s_new))
            q_col = A_QMEM_OFF // MEM_W
            w_out = pw["w_out"]
        else:
            i = l - N_A
            if kvd is None:
                kvd = norm_matmul(x, P["kv_norm_g"], P["w_kvd"], tm, 512)
            proj = norm_matmul(x, P["ln_mix_pre"][l], P["b_w_in"][i], tm, 512)
            if past_k is None:
                tok = swa_prompt(proj, kvd, _sink_col(P["b_sinks"][i], CHUNK), batch, seq)
            else:
                kd_w = SWA_KV * LANES
                k_all = jnp.concatenate([_dup_heads(past_k.reshape(batch, -1, SWA_KV * HEAD_D)),
                                         kvd[:, :kd_w].reshape(batch, seq, kd_w)], axis=1)
                v_all = jnp.concatenate([_dup_heads(past_v.reshape(batch, -1, SWA_KV * HEAD_D)),
                                         kvd[:, kd_w:].reshape(batch, seq, kd_w)], axis=1)
                tok = swa_sample(proj, k_all, v_all, _sink_col(P["b_sinks"][i], seq), batch, seq)
            q_col = TOK_W // MEM_W
            w_out = P["b_w_out"][i]
        mem_o = mem_attn(proj, q_col, mem_k[l], mem_v[l], seq, tr_mem)
        x = mix_out(tok, mem_o, w_out, P["ln_mix_post"][l], x, tm)
        x = mlp(x, P["ln_mlp_pre"][l], P["w_up"][l], P["w_down"][l], P["ln_mlp_post"][l], tm, 512)
    kd_w = SWA_KV * LANES
    k_new = _undup_heads(kvd[:, :kd_w]).reshape(batch, seq, SWA_KV, HEAD_D)
    v_new = _undup_heads(kvd[:, kd_w:]).reshape(batch, seq, SWA_KV, HEAD_D)
    return x, jnp.stack(new_wkv), jnp.stack(new_shift), k_new, v_new


def _prep_params(ln_mix_pre, ln_mix_post, ln_mlp_pre, ln_mlp_post, w_up, w_down, a_w_in, a_mu, a_w0, a_w2,
                 a_a0, a_a2, a_g2, a_k_k, a_k_a, a_r_k, a_lnx_g, a_lnx_b, a_v0, a_v1, a_v2, a_w_out,
                 kv_norm_g, w_kv, b_w_in, b_sinks, b_w_out):
    d = D_MODEL
    zrow = lambda n: jnp.zeros((n, TOK_W), F32)
    a_params = []
    for i in range(N_A):
        w_in = a_w_in[i]
        w_in = jnp.concatenate([w_in[:, :A_SHIFT], jnp.zeros((d, A_QMEM_OFF - A_SHIFT), F32),
                                w_in[:, A_SHIFT:]], axis=1).astype(BF16)
        pw = {
            "w_in": w_in,
            "mu": _pad_cols(a_mu[i].reshape(1, A_SHIFT), SHIFT_PAD),
            "w0": a_w0[i].reshape(1, TOK_W), "a0": a_a0[i].reshape(1, TOK_W),
            "w2": jnp.concatenate([a_w2[i], zrow(A_LORA)], axis=0).astype(BF16),
            "a2": jnp.concatenate([zrow(W_LORA), a_a2[i]], axis=0).astype(BF16),
            "g2": jnp.concatenate([a_g2[i], zrow(SHIFT_PAD - A_SHIFT)], axis=0).astype(BF16),
            "k_k": a_k_k[i].reshape(1, TOK_W), "k_a": a_k_a[i].reshape(1, TOK_W),
            "r_k": a_r_k[i].reshape(1, TOK_W),
            "lnx_g": a_lnx_g[i], "lnx_b": a_lnx_b[i],
            "w_out": a_w_out[i].astype(BF16),
        }
        if i > 0:
            pw["v0"] = a_v0[i - 1].reshape(1, TOK_W)
            pw["v1"] = a_v1[i - 1].astype(BF16)
            pw["v2"] = a_v2[i - 1].astype(BF16)
        a_params.append(pw)
    kw = SWA_KV * HEAD_D
    P = {
        "a": a_params,
        "ln_mix_pre": ln_mix_pre, "ln_mix_post": ln_mix_post, "ln_mlp_pre": ln_mlp_pre,
        "ln_mlp_post": ln_mlp_post,
        "w_up": w_up.astype(BF16), "w_down": w_down.astype(BF16),
        "kv_norm_g": kv_norm_g,
        "w_kvd": jnp.concatenate([_dup_heads(w_kv[:, :kw]), _dup_heads(w_kv[:, kw:])], axis=1).astype(BF16),
        "b_w_in": b_w_in.astype(BF16), "b_sinks": b_sinks, "b_w_out": b_w_out.astype(BF16),
    }
    return P


def kernel(x_prompt, x_sample, mem_prompt, state_wkv, state_shift, cache_win_k, cache_win_v, cache_mem_k, cache_mem_v, ln_mix_pre, ln_mix_post, ln_mlp_pre, ln_mlp_post, mem_norm_g, w_mem_kv, w_up, w_down, a_w_in, a_mu, a_w0, a_w2, a_a0, a_a2, a_g2, a_k_k, a_k_a, a_r_k, a_lnx_g, a_lnx_b, a_v0, a_v1, a_v2, a_w_out, kv_norm_g, w_kv, b_w_in, b_sinks, b_w_out):
    bp, sp, d = x_prompt.shape
    bs, ss, _ = x_sample.shape
    P = _prep_params(ln_mix_pre, ln_mix_post, ln_mlp_pre, ln_mlp_post, w_up, w_down, a_w_in, a_mu, a_w0,
                     a_w2, a_a0, a_a2, a_g2, a_k_k, a_k_a, a_r_k, a_lnx_g, a_lnx_b, a_v0, a_v1, a_v2,
                     a_w_out, kv_norm_g, w_kv, b_w_in, b_sinks, b_w_out)

    mem2d = mem_prompt.reshape(bp * N_MEM, d)
    mkv = [norm_matmul(mem2d, mem_norm_g[l], w_mem_kv[l].astype(BF16), bp * N_MEM, 512) for l in range(DEPTH)]
    mem_k_p = jnp.stack([t[:, :MEM_W] for t in mkv]).reshape(DEPTH, bp, N_MEM, MEM_W)
    mem_v_p = jnp.stack([t[:, MEM_W:] for t in mkv]).reshape(DEPTH, bp, N_MEM, MEM_W)
    wkv0 = jnp.zeros((N_A, bp, N_HEADS, HEAD_D, HEAD_D), F32)
    shift0 = jnp.zeros((N_A, bp, 1, A_SHIFT), F32)
    y_p, wkv_p, shift_p, k_p, v_p = _trunk(
        x_prompt.reshape(bp * sp, d), mem_k_p, mem_v_p, wkv0, shift0, None, None, P, bp, sp,
        tm=512, tr_pre=256, tr_mem=512)

    y_s, wkv_s, shift_s, k_s, v_s = _trunk(
        x_sample.reshape(bs * ss, d), cache_mem_k.reshape(DEPTH, bs, N_MEM, MEM_W),
        cache_mem_v.reshape(DEPTH, bs, N_MEM, MEM_W), state_wkv, state_shift, cache_win_k, cache_win_v,
        P, bs, ss, tm=512, tr_pre=ss, tr_mem=ss)
    past = cache_win_k.shape[1]
    win_k_s = jnp.concatenate([cache_win_k, k_s], axis=1)[:, -past:]
    win_v_s = jnp.concatenate([cache_win_v, v_s], axis=1)[:, -past:]

    return (y_p.reshape(bp, sp, d), y_s.reshape(bs, ss, d), wkv_p, shift_p,
            k_p[:, -WINDOW:], v_p[:, -WINDOW:],
            mem_k_p.reshape(DEPTH, bp, N_MEM, MEM_HEADS, MEM_HD),
            mem_v_p.reshape(DEPTH, bp, N_MEM, MEM_HEADS, MEM_HD),
            wkv_s, shift_s, win_k_s, win_v_s)
```

```python
import functools

import jax
import jax.numpy as jnp
from jax import lax
from jax.experimental import pallas as pl
from jax.experimental.pallas import tpu as pltpu

F32 = jnp.float32
BF16 = jnp.bfloat16

D_MODEL = 2048
DEPTH = 4
N_A = 2
CHUNK = 64
N_MEM = 256
MEM_HEADS = 4
MEM_HD = 128
MEM_W = MEM_HEADS * MEM_HD
TOK_W = D_MODEL - MEM_W
HEAD_D = 64
N_HEADS = TOK_W // HEAD_D
N_PAIRS = N_HEADS // 2
W_LORA = 64
A_LORA = 64
G_LORA = 224
A_SHIFT = 3 * TOK_W + W_LORA + A_LORA + G_LORA
LORA_OFF = 3 * TOK_W
GATE_OFF = LORA_OFF + 128
SHIFT_PAD = 4992
A_QMEM_OFF = 5120
A_IN_PAD = A_QMEM_OFF + MEM_W
SWA_KV = 4
SWA_G = N_HEADS // SWA_KV
WINDOW = 128
D_FF = 4 * D_MODEL
RMS_EPS = 1e-6
GN_EPS = 64e-5
NEG = -1e30
LANES = 128
VMEM_LIMIT = 56 * 1024 * 1024

_NT = (((1,), (1,)), ((), ()))
_TN = (((0,), (0,)), ((), ()))


def _cparams(*sem):
    return pltpu.CompilerParams(dimension_semantics=sem, vmem_limit_bytes=VMEM_LIMIT)


def _bdot(a, b):
    return jnp.dot(a.astype(BF16), b.astype(BF16), preferred_element_type=F32)


def _bdot_nt(a, b):
    return lax.dot_general(a.astype(BF16), b.astype(BF16), _NT, preferred_element_type=F32)


def _bdot_tn(a, b):
    return lax.dot_general(a.astype(BF16), b.astype(BF16), _TN, preferred_element_type=F32)


def _rms(x, g):
    return x * lax.rsqrt(jnp.mean(x * x, axis=-1, keepdims=True) + RMS_EPS) * g


def _norm_matmul_body(x_ref, g_ref, w_ref, o_ref, xn_ref):
    @pl.when(pl.program_id(1) == 0)
    def _():
        xn_ref[...] = _rms(x_ref[...], g_ref[...]).astype(BF16)

    o_ref[...] = jnp.dot(xn_ref[...], w_ref[...], preferred_element_type=F32).astype(o_ref.dtype)


def norm_matmul(x, g, w, tm, tn, out_dtype=F32):
    m, d = x.shape
    n = w.shape[1]
    return pl.pallas_call(
        _norm_matmul_body,
        grid=(m // tm, n // tn),
        in_specs=[pl.BlockSpec((tm, d), lambda i, j: (i, 0)),
                  pl.BlockSpec((1, d), lambda i, j: (0, 0)),
                  pl.BlockSpec((d, tn), lambda i, j: (0, j))],
        out_specs=pl.BlockSpec((tm, tn), lambda i, j: (i, j)),
        out_shape=jax.ShapeDtypeStruct((m, n), out_dtype),
        scratch_shapes=[pltpu.VMEM((tm, d), BF16)],
        compiler_params=_cparams("parallel", "arbitrary"),
        name="norm_matmul",
    )(x, g.reshape(1, d), w)


def _mix_out_body(tok_ref, mem_ref, wt_ref, wm_ref, g_ref, x_ref, o_ref):
    y = jnp.dot(tok_ref[...], wt_ref[...], preferred_element_type=F32)
    y = y + jnp.dot(mem_ref[...], wm_ref[...], preferred_element_type=F32)
    o_ref[...] = x_ref[...] + _rms(y, g_ref[...])


def mix_out(tok, mem, w_out, g, x, tm):
    m, d = x.shape
    return pl.pallas_call(
        _mix_out_body,
        grid=(m // tm,),
        in_specs=[pl.BlockSpec((tm, TOK_W), lambda i: (i, 0)),
                  pl.BlockSpec((tm, MEM_W), lambda i: (i, 0)),
                  pl.BlockSpec((TOK_W, d), lambda i: (0, 0)),
                  pl.BlockSpec((MEM_W, d), lambda i: (0, 0)),
                  pl.BlockSpec((1, d), lambda i: (0, 0)),
                  pl.BlockSpec((tm, d), lambda i: (i, 0))],
        out_specs=pl.BlockSpec((tm, d), lambda i: (i, 0)),
        out_shape=jax.ShapeDtypeStruct((m, d), F32),
        compiler_params=_cparams("parallel"),
        name="mix_out",
    )(tok, mem, w_out[:TOK_W], w_out[TOK_W:], g.reshape(1, d), x)


def _mlp_body(x_ref, g1_ref, wu_ref, wd_ref, g2_ref, o_ref, xn_ref, acc_ref):
    f = pl.program_id(1)

    @pl.when(f == 0)
    def _():
        xn_ref[...] = _rms(x_ref[...], g1_ref[...]).astype(BF16)
        acc_ref[...] = jnp.zeros_like(acc_ref)

    h = jnp.maximum(jnp.dot(xn_ref[...], wu_ref[...], preferred_element_type=F32), 0.0)
    acc_ref[...] += jnp.dot((h * h).astype(BF16), wd_ref[...], preferred_element_type=F32)

    @pl.when(f == pl.num_programs(1) - 1)
    def _():
        o_ref[...] = x_ref[...] + _rms(acc_ref[...], g2_ref[...])


def mlp(x, g1, w_up, w_down, g2, tm, tf):
    m, d = x.shape
    ff = w_up.shape[1]
    return pl.pallas_call(
        _mlp_body,
        grid=(m // tm, ff // tf),
        in_specs=[pl.BlockSpec((tm, d), lambda i, f: (i, 0)),
                  pl.BlockSpec((1, d), lambda i, f: (0, 0)),
                  pl.BlockSpec((d, tf), lambda i, f: (0, f)),
                  pl.BlockSpec((tf, d), lambda i, f: (f, 0)),
                  pl.BlockSpec((1, d), lambda i, f: (0, 0))],
        out_specs=pl.BlockSpec((tm, d), lambda i, f: (i, 0)),
        out_shape=jax.ShapeDtypeStruct((m, d), F32),
        scratch_shapes=[pltpu.VMEM((tm, d), BF16), pltpu.VMEM((tm, d), F32)],
        compiler_params=_cparams("parallel", "arbitrary"),
        name="mlp",
    )(x, g1.reshape(1, d), w_up, w_down, g2.reshape(1, d))


def _mem_attn_body(q_ref, k_ref, v_ref, o_ref):
    scale = MEM_HD ** -0.5
    heads = range(MEM_HEADS)
    sls = [slice(h * MEM_HD, (h + 1) * MEM_HD) for h in heads]
    s = [_bdot_nt(q_ref[:, sl], k_ref[:, sl]) * scale for sl in sls]
    e = [jnp.exp(s[h] - jnp.max(s[h], axis=-1, keepdims=True)) for h in heads]
    p = [e[h] / jnp.sum(e[h], axis=-1, keepdims=True) for h in heads]
    o = [_bdot(p[h], v_ref[:, sls[h]]) for h in heads]
    for h in heads:
        o_ref[:, sls[h]] = o[h].astype(o_ref.dtype)


def mem_attn(proj, q_col_block, mem_k, mem_v, rows_per_seq, tr):
    m = proj.shape[0]
    tiles_per_seq = rows_per_seq // tr
    return pl.pallas_call(
        _mem_attn_body,
        grid=(m // tr,),
        in_specs=[pl.BlockSpec((tr, MEM_W), lambda i: (i, q_col_block)),
                  pl.BlockSpec((None, N_MEM, MEM_W), lambda i: (i // tiles_per_seq, 0, 0)),
                  pl.BlockSpec((None, N_MEM, MEM_W), lambda i: (i // tiles_per_seq, 0, 0))],
        out_specs=pl.BlockSpec((tr, MEM_W), lambda i: (i, 0)),
        out_shape=jax.ShapeDtypeStruct((m, MEM_W), BF16),
        compiler_params=_cparams("parallel"),
        name="mem_attn",
    )(proj, mem_k, mem_v)


def _swa_core(q_ref, k_tiles, v_tiles, valid, sink_ref, o_ref, lq):
    lane = lax.broadcasted_iota(jnp.int32, (lq, LANES), 1)
    low = lane < HEAD_D
    groups = range(SWA_KV)
    s = []
    for j in groups:
        xs = []
        for i in range(SWA_G // 2):
            c0 = (j * SWA_G + 2 * i) * HEAD_D
            pair = q_ref[:, c0:c0 + LANES] * (HEAD_D ** -0.5)
            xs.append(jnp.where(low, pair, 0.0).astype(BF16))
            xs.append(jnp.where(low, 0.0, pair).astype(BF16))
        x = jnp.concatenate(xs, axis=0)
        kj = jnp.concatenate([t.astype(BF16) for t in k_tiles(j)], axis=0)
        sj = lax.dot_general(x, kj, _NT, preferred_element_type=F32)
        s.append(sj if valid is None else jnp.where(valid, sj, NEG))
    p = []
    for j in groups:
        sink = sink_ref[j]
        mx = jnp.maximum(jnp.max(s[j], axis=-1, keepdims=True), sink)
        e = jnp.exp(s[j] - mx)
        p.append((e / (jnp.sum(e, axis=-1, keepdims=True) + jnp.exp(sink - mx))).astype(BF16))
    o = [jnp.dot(p[j], jnp.concatenate([t.astype(BF16) for t in v_tiles(j)], axis=0),
                 preferred_element_type=F32) for j in groups]
    for j in groups:
        for i in range(SWA_G // 2):
            c0 = (j * SWA_G + 2 * i) * HEAD_D
            top = o[j][(2 * i) * lq:(2 * i + 1) * lq]
            bot = o[j][(2 * i + 1) * lq:(2 * i + 2) * lq]
            o_ref[:, c0:c0 + LANES] = jnp.where(low, top, bot).astype(o_ref.dtype)


def _swa_prompt_body(q_ref, k0, k1, k2, v0, v1, v2, sink_ref, o_ref):
    c = pl.program_id(1)
    col = lax.broadcasted_iota(jnp.int32, (SWA_G * CHUNK, 3 * CHUNK), 1)
    valid = col >= jnp.maximum(2 - c, 0) * CHUNK
    _swa_core(q_ref,
              lambda j: [r[:, j * LANES:(j + 1) * LANES] for r in (k0, k1, k2)],
              lambda j: [r[:, j * LANES:(j + 1) * LANES] for r in (v0, v1, v2)],
              valid, sink_ref, o_ref, CHUNK)


def swa_prompt(proj, kvd, sink_col, batch, seq):
    nc = seq // CHUNK
    kd_w = SWA_KV * LANES

    def kv_spec(back, col):
        return pl.BlockSpec((CHUNK, kd_w), lambda b, c: (b * nc + jnp.maximum(c - back, 0), col))

    return pl.pallas_call(
        _swa_prompt_body,
        grid=(batch, nc),
        in_specs=[pl.BlockSpec((CHUNK, TOK_W), lambda b, c: (b * nc + c, 0)),
                  kv_spec(2, 0), kv_spec(1, 0), kv_spec(0, 0),
                  kv_spec(2, 1), kv_spec(1, 1), kv_spec(0, 1),
                  pl.BlockSpec((SWA_KV, SWA_G * CHUNK, 1), lambda b, c: (0, 0, 0))],
        out_specs=pl.BlockSpec((CHUNK, TOK_W), lambda b, c: (b * nc + c, 0)),
        out_shape=jax.ShapeDtypeStruct((batch * seq, TOK_W), BF16),
        compiler_params=_cparams("parallel", "arbitrary"),
        name="swa_prompt",
    )(proj, kvd, kvd, kvd, kvd, kvd, kvd, sink_col)


def _swa_sample_body(q_ref, k_ref, v_ref, sink_ref, o_ref, *, lq):
    _swa_core(q_ref,
              lambda j: [k_ref[:, j * LANES:(j + 1) * LANES]],
              lambda j: [v_ref[:, j * LANES:(j + 1) * LANES]],
              None, sink_ref, o_ref, lq)


def swa_sample(proj, k_all, v_all, sink_col, batch, lq):
    keys = k_all.shape[1]
    kd_w = SWA_KV * LANES
    return pl.pallas_call(
        functools.partial(_swa_sample_body, lq=lq),
        grid=(batch,),
        in_specs=[pl.BlockSpec((lq, TOK_W), lambda b: (b, 0)),
                  pl.BlockSpec((None, keys, kd_w), lambda b: (b, 0, 0)),
                  pl.BlockSpec((None, keys, kd_w), lambda b: (b, 0, 0)),
                  pl.BlockSpec((SWA_KV, SWA_G * lq, 1), lambda b: (0, 0, 0))],
        out_specs=pl.BlockSpec((lq, TOK_W), lambda b: (b, 0)),
        out_shape=jax.ShapeDtypeStruct((batch * lq, TOK_W), BF16),
        compiler_params=_cparams("parallel"),
        name="swa_sample",
    )(proj, k_all, v_all, sink_col)


def _pair_ones():
    r = lax.broadcasted_iota(jnp.int32, (LANES, LANES), 0)
    c = lax.broadcasted_iota(jnp.int32, (LANES, LANES), 1)
    return jnp.where((r < HEAD_D) == (c < HEAD_D), 1.0, 0.0).astype(BF16)


def _head_sum(x, ones_bd):
    hi = x.astype(BF16)
    lo = (x - hi.astype(F32)).astype(BF16)
    return (jnp.dot(hi, ones_bd, preferred_element_type=F32)
            + jnp.dot(lo, ones_bd, preferred_element_type=F32))


def _rwkv_pre_body(p_ref, prev_ref, mu_ref, w0_ref, w2_ref, a0_ref, a2_ref, g2_ref, kk_ref, ka_ref,
                   rk_ref, *rest, has_vres):
    if has_vres:
        vf_ref, v0_ref, v1_ref, v2_ref = rest[:4]
        outs = rest[4:]
    else:
        outs = rest
    r_o, lw_o, k_o, v_o, kn_o, b_o, g_o, bonus_o = outs
    tr = p_ref.shape[0]
    row0 = lax.broadcasted_iota(jnp.int32, (tr, LANES), 0) == 0
    ones_bd = _pair_ones()

    def mixed(c0, width):
        outs_ = []
        for t in range(width // LANES):
            sl = slice(c0 + t * LANES, c0 + (t + 1) * LANES)
            p = p_ref[:, sl]
            pp = jnp.where(row0, prev_ref[:, sl], pltpu.roll(p, 1, 0))
            outs_.append(p + (pp - p) * mu_ref[:, sl])
        return outs_

    lora = mixed(LORA_OFF, LANES)[0]
    gate_in = jnp.concatenate(mixed(GATE_OFF, SHIFT_PAD - GATE_OFF), axis=1)
    w_pre = w0_ref[...] + _bdot(jnp.tanh(lora), w2_ref[...])
    lw_o[...] = -jnp.exp(-jax.nn.softplus(-w_pre) - 0.5)
    a = jax.nn.sigmoid(a0_ref[...] + _bdot(lora, a2_ref[...]))
    g_o[...] = _bdot(jax.nn.sigmoid(gate_in), g2_ref[...])

    r_t = mixed(0, TOK_W)
    k_t = mixed(TOK_W, TOK_W)
    v_t = mixed(2 * TOK_W, TOK_W)
    if has_vres:
        v_all = jnp.concatenate(v_t, axis=1)
        gate = jax.nn.sigmoid(v0_ref[...] + _bdot(_bdot(v_all, v1_ref[...]), v2_ref[...]))
        v_all = v_all + (vf_ref[...] - v_all) * gate
        v_t = [v_all[:, t * LANES:(t + 1) * LANES] for t in range(N_PAIRS)]
    for t in range(N_PAIRS):
        sl = slice(t * LANES, (t + 1) * LANES)
        kraw = k_t[t]
        kk = kraw * kk_ref[:, sl]
        kk = kk / jnp.maximum(jnp.sqrt(_head_sum(kk * kk, ones_bd)), 1e-12)
        a_t = a[:, sl]
        k = kraw * (1.0 + (a_t - 1.0) * ka_ref[:, sl])
        v = v_t[t]
        r_o[:, sl] = r_t[t]
        k_o[:, sl] = k
        v_o[:, sl] = v
        kn_o[:, sl] = kk
        b_o[:, sl] = kk * a_t
        bonus_o[:, sl] = _head_sum(r_t[t] * k * rk_ref[:, sl], ones_bd) * v


def rwkv_pre(proj, prev, pw, v_first, tr):
    m = proj.shape[0]
    has_vres = v_first is not None
    row = lambda w: pl.BlockSpec((1, w), lambda i: (0, 0))
    full = lambda a: pl.BlockSpec(a.shape, lambda i: (0,) * a.ndim)
    tile = pl.BlockSpec((tr, TOK_W), lambda i: (i, 0))
    in_specs = [pl.BlockSpec((tr, A_IN_PAD), lambda i: (i, 0)),
                pl.BlockSpec((None, 1, A_IN_PAD), lambda i: (i, 0, 0)),
                row(SHIFT_PAD), row(TOK_W), full(pw["w2"]), row(TOK_W), full(pw["a2"]), full(pw["g2"]),
                row(TOK_W), row(TOK_W), row(TOK_W)]
    args = [proj, prev, pw["mu"], pw["w0"], pw["w2"], pw["a0"], pw["a2"], pw["g2"], pw["k_k"],
            pw["k_a"], pw["r_k"]]
    if has_vres:
        in_specs += [tile, row(TOK_W), full(pw["v1"]), full(pw["v2"])]
        args += [v_first, pw["v0"], pw["v1"], pw["v2"]]
    return pl.pallas_call(
        functools.partial(_rwkv_pre_body, has_vres=has_vres),
        grid=(m // tr,),
        in_specs=in_specs,
        out_specs=[tile] * 8,
        out_shape=[jax.ShapeDtypeStruct((m, TOK_W), F32)] * 8,
        compiler_params=_cparams("parallel"),
        name="rwkv_pre",
    )(*args)


def _wkv_body(r_ref, lw_ref, k_ref, v_ref, kn_ref, b_ref, g_ref, bonus_ref, s0_ref, lg_ref, lb_ref,
              o_ref, s_out_ref, s_ref, *, chunk):
    c = pl.program_id(1)
    L = chunk
    W2 = 2 * L

    @pl.when(c == 0)
    def _():
        s_ref[...] = s0_ref[...]

    t_i = lax.broadcasted_iota(jnp.int32, (L, W2), 0)
    l_i = lax.broadcasted_iota(jnp.int32, (L, W2), 1)
    j_i = jnp.where(l_i >= L, l_i - L, l_i)
    strict = t_i > j_i
    incl = t_i >= j_i
    eye2 = jnp.where(t_i == j_i, 1.0, 0.0)
    levels = []
    s = 1
    while s < L:
        sh = s.bit_length() - 1
        levels.append(((t_i >> (sh + 1)) == (j_i >> (sh + 1)))
                      & (((t_i >> sh) & 1) == 1) & (((j_i >> sh) & 1) == 0))
        s *= 2
    first_t = l_i < L
    lane = lax.broadcasted_iota(jnp.int32, (L, LANES), 1)
    low = lane < HEAD_D
    rr = lax.broadcasted_iota(jnp.int32, (LANES, LANES), 0)
    cc = lax.broadcasted_iota(jnp.int32, (LANES, LANES), 1)
    bd = (rr < HEAD_D) == (cc < HEAD_D)
    ones_bd = jnp.where(bd, 1.0, 0.0).astype(BF16)
    tri = jnp.where(lax.broadcasted_iota(jnp.int32, (L, L), 0)
                    >= lax.broadcasted_iota(jnp.int32, (L, L), 1), 1.0, 0.0)

    def stack_t(x):
        return jnp.concatenate([jnp.where(first_t, x, 0.0), jnp.where(first_t, 0.0, x)],
                               axis=0).astype(BF16)

    def stack_d(x):
        return jnp.concatenate([jnp.where(low, x, 0.0), jnp.where(low, 0.0, x)], axis=0).astype(BF16)

    def tdot(m, stacked):
        return jnp.dot(m.astype(BF16), stacked, preferred_element_type=F32)

    cum_all = jnp.dot(tri, lw_ref[...], precision=lax.Precision.HIGHEST, preferred_element_type=F32)

    pairs = range(N_PAIRS)
    sls = [slice(p * LANES, (p + 1) * LANES) for p in pairs]
    a_t, r_t, sv, bh, kh, dec, cb, ck = [], [], [], [], [], [], [], []
    for p in pairs:
        sl = sls[p]
        lw = lw_ref[:, sl]
        cum = cum_all[:, sl]
        cum_l = cum[L - 1:L, :]
        p_inv = jnp.exp(-cum)
        p_end = jnp.exp(cum_l - cum)
        b = b_ref[:, sl]
        k = k_ref[:, sl]
        a_t.append(-kn_ref[:, sl] * jnp.exp(cum - lw))
        r_t.append(r_ref[:, sl] * jnp.exp(cum))
        sv.append(stack_d(v_ref[:, sl]))
        bh.append((b * p_end).astype(BF16))
        kh.append((k * p_end).astype(BF16))
        dec.append(jnp.exp(cum_l))
        x = jnp.concatenate([a_t[p], r_t[p]], axis=0).astype(BF16)
        cb.append(lax.dot_general(x, stack_d(b * p_inv), _NT, preferred_element_type=F32))
        ck.append(lax.dot_general(x, stack_d(k * p_inv), _NT, preferred_element_type=F32))
    a_ab = [jnp.where(strict, cb[p][:L], 0.0) for p in pairs]
    a_ak = [jnp.where(strict, ck[p][:L], 0.0).astype(BF16) for p in pairs]
    a_rb = [jnp.where(incl, cb[p][L:], 0.0).astype(BF16) for p in pairs]
    a_rk = [jnp.where(incl, ck[p][L:], 0.0).astype(BF16) for p in pairs]
    akv = [jnp.dot(a_ak[p], sv[p], preferred_element_type=F32) for p in pairs]
    t_m = [eye2 + jnp.where(levels[0], a_ab[p], 0.0) for p in pairs]
    for msk in levels[1:]:
        ta = [tdot(t_m[p], stack_t(jnp.where(msk, a_ab[p], 0.0))) for p in pairs]
        t_m = [t_m[p] + tdot(ta[p], stack_t(t_m[p])) for p in pairs]
    t_b = [t_m[p].astype(BF16) for p in pairs]
    w1 = [jnp.dot(t_b[p], stack_d(a_t[p]), preferred_element_type=F32) for p in pairs]
    w2 = [jnp.dot(t_b[p], stack_d(akv[p]), preferred_element_type=F32) for p in pairs]
    q1 = [r_t[p] + jnp.dot(a_rb[p], stack_d(w1[p]), preferred_element_type=F32) for p in pairs]
    q2 = [jnp.dot(a_rb[p], stack_d(w2[p]), preferred_element_type=F32)
          + jnp.dot(a_rk[p], sv[p], preferred_element_type=F32) for p in pairs]
    m1 = [jnp.where(bd, _bdot_tn(w1[p], bh[p]), 0.0) for p in pairs]
    m2 = [jnp.where(bd, _bdot_tn(w2[p], bh[p]) + _bdot_tn(v_ref[:, sls[p]], kh[p]), 0.0) for p in pairs]
    y = []
    for p in pairs:
        s_p = s_ref[p]
        y.append(_bdot_nt(q1[p], s_p) + q2[p])
        s_ref[p] = s_p * dec[p] + _bdot(s_p, m1[p]) + m2[p]
    mean = [_head_sum(y[p], ones_bd) * (1.0 / HEAD_D) for p in pairs]
    dv = [y[p] - mean[p] for p in pairs]
    var = [_head_sum(dv[p] * dv[p], ones_bd) * (1.0 / HEAD_D) for p in pairs]
    for p in pairs:
        sl = sls[p]
        yn = dv[p] * lax.rsqrt(var[p] + GN_EPS) * lg_ref[:, sl] + lb_ref[:, sl]
        o_ref[:, sl] = ((yn + bonus_ref[:, sl]) * g_ref[:, sl]).astype(o_ref.dtype)

    @pl.when(c == pl.num_programs(1) - 1)
    def _():
        s_out_ref[...] = s_ref[...]


def wkv(r, lw, k, v, kn, b, g, bonus, s0, lnx_g, lnx_b, batch, seq, chunk):
    nc = seq // chunk
    tile = pl.BlockSpec((chunk, TOK_W), lambda bi, c: (bi * nc + c, 0))
    state = pl.BlockSpec((None, N_PAIRS, LANES, LANES), lambda bi, c: (bi, 0, 0, 0))
    row = pl.BlockSpec((1, TOK_W), lambda bi, c: (0, 0))
    return pl.pallas_call(
        functools.partial(_wkv_body, chunk=chunk),
        grid=(batch, nc),
        in_specs=[tile] * 8 + [state, row, row],
        out_specs=[tile, state],
        out_shape=[jax.ShapeDtypeStruct((batch * seq, TOK_W), BF16),
                   jax.ShapeDtypeStruct((batch, N_PAIRS, LANES, LANES), F32)],
        scratch_shapes=[pltpu.VMEM((N_PAIRS, LANES, LANES), F32)],
        compiler_params=_cparams("parallel", "arbitrary"),
        name="wkv",
    )(r, lw, k, v, kn, b, g, bonus, s0, lnx_g.reshape(1, TOK_W), lnx_b.reshape(1, TOK_W))


def _state_to_pairs(s):
    b = s.shape[0]
    s = s.reshape(b, N_PAIRS, 2, HEAD_D, HEAD_D)
    z = jnp.zeros_like(s[:, :, 0])
    top = jnp.concatenate([s[:, :, 0], z], axis=-1)
    bot = jnp.concatenate([z, s[:, :, 1]], axis=-1)
    return jnp.concatenate([top, bot], axis=-2)


def _pairs_to_state(sp):
    b = sp.shape[0]
    h0 = sp[:, :, :HEAD_D, :HEAD_D]
    h1 = sp[:, :, HEAD_D:, HEAD_D:]
    return jnp.stack([h0, h1], axis=2).reshape(b, N_HEADS, HEAD_D, HEAD_D)


def _dup_heads(w):
    lead = w.shape[:-1]
    w = w.reshape(lead + (SWA_KV, 1, HEAD_D))
    return jnp.broadcast_to(w, lead + (SWA_KV, 2, HEAD_D)).reshape(lead + (SWA_KV * LANES,))


def _undup_heads(x):
    lead = x.shape[:-1]
    return x.reshape(lead + (SWA_KV, 2, HEAD_D))[..., 0, :]


def _pad_cols(a, width):
    return jnp.pad(a, [(0, 0)] * (a.ndim - 1) + [(0, width - a.shape[-1])])


def _sink_col(sinks, lq):
    return jnp.repeat(sinks.astype(F32).reshape(SWA_KV, SWA_G), lq, axis=1)[..., None]


def _trunk(x, mem_k, mem_v, wkv0, shift0, past_k, past_v, P, batch, seq, tm, tr_pre, tr_mem):
    m = batch * seq
    chunk = min(CHUNK, seq)
    new_wkv, new_shift = [], []
    v_first = None
    kvd = None
    for l in range(DEPTH):
        if l < N_A:
            pw = P["a"][l]
            proj = norm_matmul(x, P["ln_mix_pre"][l], pw["w_in"], tm, 512)
            p3 = proj.reshape(batch, seq // tr_pre, tr_pre, A_IN_PAD)
            first = _pad_cols(shift0[l].astype(F32), A_IN_PAD).reshape(batch, 1, A_IN_PAD)
            prev = jnp.concatenate([first, p3[:, :-1, -1, :]], axis=1).reshape(m // tr_pre, 1, A_IN_PAD)
            new_shift.append(p3[:, -1, -1, :A_SHIFT].reshape(batch, 1, A_SHIFT))
            r, lw, k, v, kn, b, g, bonus = rwkv_pre(proj, prev, pw, v_first, tr_pre)
            if l == 0:
                v_first = v
            tok, s_new = wkv(r, lw, k, v, kn, b, g, bonus, _state_to_pairs(wkv0[l].astype(F32)),
                             pw["lnx_g"], pw["lnx_b"], batch, seq, chunk)
            new_wkv.append(_pairs_to_state(s_new))
            q_col = A_QMEM_OFF // MEM_W
            w_out = pw["w_out"]
        else:
            i = l - N_A
            if kvd is None:
                kvd = norm_matmul(x, P["kv_norm_g"], P["w_kvd"], tm, 512)
            proj = norm_matmul(x, P["ln_mix_pre"][l], P["b_w_in"][i], tm, 512)
            if past_k is None:
                tok = swa_prompt(proj, kvd, _sink_col(P["b_sinks"][i], CHUNK), batch, seq)
            else:
                kd_w = SWA_KV * LANES
                k_all = jnp.concatenate([_dup_heads(past_k.reshape(batch, -1, SWA_KV * HEAD_D)),
                                         kvd[:, :kd_w].reshape(batch, seq, kd_w)], axis=1)
                v_all = jnp.concatenate([_dup_heads(past_v.reshape(batch, -1, SWA_KV * HEAD_D)),
                                         kvd[:, kd_w:].reshape(batch, seq, kd_w)], axis=1)
                tok = swa_sample(proj, k_all, v_all, _sink_col(P["b_sinks"][i], seq), batch, seq)
            q_col = TOK_W // MEM_W
            w_out = P["b_w_out"][i]
        mem_o = mem_attn(proj, q_col, mem_k[l], mem_v[l], seq, tr_mem)
        x = mix_out(tok, mem_o, w_out, P["ln_mix_post"][l], x, tm)
        x = mlp(x, P["ln_mlp_pre"][l], P["w_up"][l], P["w_down"][l], P["ln_mlp_post"][l], tm, 512)
    kd_w = SWA_KV * LANES
    k_new = _undup_heads(kvd[:, :kd_w]).reshape(batch, seq, SWA_KV, HEAD_D)
    v_new = _undup_heads(kvd[:, kd_w:]).reshape(batch, seq, SWA_KV, HEAD_D)
    return x, jnp.stack(new_wkv), jnp.stack(new_shift), k_new, v_new


def _prep_params(ln_mix_pre, ln_mix_post, ln_mlp_pre, ln_mlp_post, w_up, w_down, a_w_in, a_mu, a_w0, a_w2,
                 a_a0, a_a2, a_g2, a_k_k, a_k_a, a_r_k, a_lnx_g, a_lnx_b, a_v0, a_v1, a_v2, a_w_out,
                 kv_norm_g, w_kv, b_w_in, b_sinks, b_w_out):
    d = D_MODEL
    zrow = lambda n: jnp.zeros((n, TOK_W), F32)
    a_params = []
    for i in range(N_A):
        w_in = a_w_in[i]
        w_in = jnp.concatenate([w_in[:, :A_SHIFT], jnp.zeros((d, A_QMEM_OFF - A_SHIFT), F32),
                                w_in[:, A_SHIFT:]], axis=1).astype(BF16)
        pw = {
            "w_in": w_in,
            "mu": _pad_cols(a_mu[i].reshape(1, A_SHIFT), SHIFT_PAD),
            "w0": a_w0[i].reshape(1, TOK_W), "a0": a_a0[i].reshape(1, TOK_W),
            "w2": jnp.concatenate([a_w2[i], zrow(A_LORA)], axis=0).astype(BF16),
            "a2": jnp.concatenate([zrow(W_LORA), a_a2[i]], axis=0).astype(BF16),
            "g2": jnp.concatenate([a_g2[i], zrow(SHIFT_PAD - A_SHIFT)], axis=0).astype(BF16),
            "k_k": a_k_k[i].reshape(1, TOK_W), "k_a": a_k_a[i].reshape(1, TOK_W),
            "r_k": a_r_k[i].reshape(1, TOK_W),
            "lnx_g": a_lnx_g[i], "lnx_b": a_lnx_b[i],
            "w_out": a_w_out[i].astype(BF16),
        }
        if i > 0:
            pw["v0"] = a_v0[i - 1].reshape(1, TOK_W)
            pw["v1"] = a_v1[i - 1].astype(BF16)
            pw["v2"] = a_v2[i - 1].astype(BF16)
        a_params.append(pw)
    kw = SWA_KV * HEAD_D
    P = {
        "a": a_params,
        "ln_mix_pre": ln_mix_pre, "ln_mix_post": ln_mix_post, "ln_mlp_pre": ln_mlp_pre,
        "ln_mlp_post": ln_mlp_post,
        "w_up": w_up.astype(BF16), "w_down": w_down.astype(BF16),
        "kv_norm_g": kv_norm_g,
        "w_kvd": jnp.concatenate([_dup_heads(w_kv[:, :kw]), _dup_heads(w_kv[:, kw:])], axis=1).astype(BF16),
        "b_w_in": b_w_in.astype(BF16), "b_sinks": b_sinks, "b_w_out": b_w_out.astype(BF16),
    }
    return P


def kernel(x_prompt, x_sample, mem_prompt, state_wkv, state_shift, cache_win_k, cache_win_v, cache_mem_k, cache_mem_v, ln_mix_pre, ln_mix_post, ln_mlp_pre, ln_mlp_post, mem_norm_g, w_mem_kv, w_up, w_down, a_w_in, a_mu, a_w0, a_w2, a_a0, a_a2, a_g2, a_k_k, a_k_a, a_r_k, a_lnx_g, a_lnx_b, a_v0, a_v1, a_v2, a_w_out, kv_norm_g, w_kv, b_w_in, b_sinks, b_w_out):
    bp, sp, d = x_prompt.shape
    bs, ss, _ = x_sample.shape
    P = _prep_params(ln_mix_pre, ln_mix_post, ln_mlp_pre, ln_mlp_post, w_up, w_down, a_w_in, a_mu, a_w0,
                     a_w2, a_a0, a_a2, a_g2, a_k_k, a_k_a, a_r_k, a_lnx_g, a_lnx_b, a_v0, a_v1, a_v2,
                     a_w_out, kv_norm_g, w_kv, b_w_in, b_sinks, b_w_out)

    mem2d = mem_prompt.reshape(bp * N_MEM, d)
    mkv = [norm_matmul(mem2d, mem_norm_g[l], w_mem_kv[l].astype(BF16), bp * N_MEM, 512) for l in range(DEPTH)]
    mem_k_p = jnp.stack([t[:, :MEM_W] for t in mkv]).reshape(DEPTH, bp, N_MEM, MEM_W)
    mem_v_p = jnp.stack([t[:, MEM_W:] for t in mkv]).reshape(DEPTH, bp, N_MEM, MEM_W)
    wkv0 = jnp.zeros((N_A, bp, N_HEADS, HEAD_D, HEAD_D), F32)
    shift0 = jnp.zeros((N_A, bp, 1, A_SHIFT), F32)
    y_p, wkv_p, shift_p, k_p, v_p = _trunk(
        x_prompt.reshape(bp * sp, d), mem_k_p, mem_v_p, wkv0, shift0, None, None, P, bp, sp,
        tm=512, tr_pre=256, tr_mem=512)

    y_s, wkv_s, shift_s, k_s, v_s = _trunk(
        x_sample.reshape(bs * ss, d), cache_mem_k.reshape(DEPTH, bs, N_MEM, MEM_W),
        cache_mem_v.reshape(DEPTH, bs, N_MEM, MEM_W), state_wkv, state_shift, cache_win_k, cache_win_v,
        P, bs, ss, tm=512, tr_pre=ss, tr_mem=ss)
    past = cache_win_k.shape[1]
    win_k_s = jnp.concatenate([cache_win_k, k_s], axis=1)[:, -past:]
    win_v_s = jnp.concatenate([cache_win_v, v_s], axis=1)[:, -past:]

    return (y_p.reshape(bp, sp, d), y_s.reshape(bs, ss, d), wkv_p, shift_p,
            k_p[:, -WINDOW:], v_p[:, -WINDOW:],
            mem_k_p.reshape(DEPTH, bp, N_MEM, MEM_HEADS, MEM_HD),
            mem_v_p.reshape(DEPTH, bp, N_MEM, MEM_HEADS, MEM_HD),
            wkv_s, shift_s, win_k_s, win_v_s)
```

```python
import functools

import jax
import jax.numpy as jnp
from jax import lax
from jax.experimental import pallas as pl
from jax.experimental.pallas import tpu as pltpu

F32 = jnp.float32
BF16 = jnp.bfloat16

D_MODEL = 2048
DEPTH = 4
N_A = 2
CHUNK = 64
N_MEM = 256
MEM_HEADS = 4
MEM_HD = 128
MEM_W = MEM_HEADS * MEM_HD
TOK_W = D_MODEL - MEM_W
HEAD_D = 64
N_HEADS = TOK_W // HEAD_D
N_PAIRS = N_HEADS // 2
W_LORA = 64
A_LORA = 64
G_LORA = 224
A_SHIFT = 3 * TOK_W + W_LORA + A_LORA + G_LORA
LORA_OFF = 3 * TOK_W
GATE_OFF = LORA_OFF + 128
SHIFT_PAD = 4992
A_QMEM_OFF = 5120
A_IN_PAD = A_QMEM_OFF + MEM_W
SWA_KV = 4
SWA_G = N_HEADS // SWA_KV
WINDOW = 128
D_FF = 4 * D_MODEL
RMS_EPS = 1e-6
GN_EPS = 64e-5
NEG = -1e30
LANES = 128
VMEM_LIMIT = 56 * 1024 * 1024

_NT = (((1,), (1,)), ((), ()))
_TN = (((0,), (0,)), ((), ()))


def _cparams(*sem):
    return pltpu.CompilerParams(dimension_semantics=sem, vmem_limit_bytes=VMEM_LIMIT)


def _bdot(a, b):
    return jnp.dot(a.astype(BF16), b.astype(BF16), preferred_element_type=F32)


def _bdot_nt(a, b):
    return lax.dot_general(a.astype(BF16), b.astype(BF16), _NT, preferred_element_type=F32)


def _bdot_tn(a, b):
    return lax.dot_general(a.astype(BF16), b.astype(BF16), _TN, preferred_element_type=F32)


def _rms(x, g):
    return x * lax.rsqrt(jnp.mean(x * x, axis=-1, keepdims=True) + RMS_EPS) * g


def _norm_matmul_body(x_ref, g_ref, w_ref, o_ref, xn_ref):
    @pl.when(pl.program_id(1) == 0)
    def _():
        xn_ref[...] = _rms(x_ref[...], g_ref[...]).astype(BF16)

    o_ref[...] = jnp.dot(xn_ref[...], w_ref[...], preferred_element_type=F32).astype(o_ref.dtype)


def _proj_tiles(m, n):
    tm = min(m, 1024)
    for tn in (1408, 1024, 512):
        if n % tn == 0:
            return tm, tn
    raise ValueError(f"unsupported projection width {n}")


def norm_matmul(x, g, gl, w, wl):
    m, d = x.shape
    n = w.shape[2]
    tm, tn = _proj_tiles(m, n)
    return pl.pallas_call(
        _norm_matmul_body,
        grid=(m // tm, n // tn),
        in_specs=[pl.BlockSpec((tm, d), lambda i, j: (i, 0)),
                  pl.BlockSpec((None, 1, d), lambda i, j: (gl, 0, 0)),
                  pl.BlockSpec((None, d, tn), lambda i, j: (wl, 0, j))],
        out_specs=pl.BlockSpec((tm, tn), lambda i, j: (i, j)),
        out_shape=jax.ShapeDtypeStruct((m, n), F32),
        scratch_shapes=[pltpu.VMEM((tm, d), BF16)],
        compiler_params=_cparams("parallel", "arbitrary"),
        name="norm_matmul",
    )(x, g, w)


def norm_matmul_layers(x, g, w):
    m, d = x.shape
    nl, _, n = w.shape
    _, tn = _proj_tiles(m, n)
    return pl.pallas_call(
        _norm_matmul_body,
        grid=(nl, n // tn),
        in_specs=[pl.BlockSpec((m, d), lambda l, j: (0, 0)),
                  pl.BlockSpec((None, 1, d), lambda l, j: (l, 0, 0)),
                  pl.BlockSpec((None, d, tn), lambda l, j: (l, 0, j))],
        out_specs=pl.BlockSpec((None, m, tn), lambda l, j: (l, 0, j)),
        out_shape=jax.ShapeDtypeStruct((nl, m, n), F32),
        scratch_shapes=[pltpu.VMEM((m, d), BF16)],
        compiler_params=_cparams("parallel", "arbitrary"),
        name="norm_matmul_layers",
    )(x, g, w)


def _mix_out_body(tok_ref, mem_ref, wt_ref, wm_ref, g_ref, x_ref, o_ref):
    y = jnp.dot(tok_ref[...], wt_ref[...], preferred_element_type=F32)
    y = y + jnp.dot(mem_ref[...], wm_ref[...], preferred_element_type=F32)
    o_ref[...] = x_ref[...] + _rms(y, g_ref[...])


def mix_out(tok, mem, w_out, wl, g, gl, x):
    m, d = x.shape
    tm = min(m, 512)
    return pl.pallas_call(
        _mix_out_body,
        grid=(m // tm,),
        in_specs=[pl.BlockSpec((tm, TOK_W), lambda i: (i, 0)),
                  pl.BlockSpec((tm, MEM_W), lambda i: (i, 0)),
                  pl.BlockSpec((None, TOK_W, d), lambda i: (wl, 0, 0)),
                  pl.BlockSpec((None, MEM_W, d), lambda i: (wl, TOK_W // MEM_W, 0)),
                  pl.BlockSpec((None, 1, d), lambda i: (gl, 0, 0)),
                  pl.BlockSpec((tm, d), lambda i: (i, 0))],
        out_specs=pl.BlockSpec((tm, d), lambda i: (i, 0)),
        out_shape=jax.ShapeDtypeStruct((m, d), F32),
        compiler_params=_cparams("parallel"),
        name="mix_out",
    )(tok, mem, w_out, w_out, g, x)


def _mlp_body(x_ref, g1_ref, wu_ref, wd_ref, g2_ref, o_ref, xn_ref, acc_ref):
    f = pl.program_id(1)

    @pl.when(f == 0)
    def _():
        xn_ref[...] = _rms(x_ref[...], g1_ref[...]).astype(BF16)
        acc_ref[...] = jnp.zeros_like(acc_ref)

    h = jnp.maximum(jnp.dot(xn_ref[...], wu_ref[...], preferred_element_type=F32), 0.0)
    acc_ref[...] += jnp.dot((h * h).astype(BF16), wd_ref[...], preferred_element_type=F32)

    @pl.when(f == pl.num_programs(1) - 1)
    def _():
        o_ref[...] = x_ref[...] + _rms(acc_ref[...], g2_ref[...])


def mlp(x, g1, w_up, w_down, g2, l):
    m, d = x.shape
    ff = w_up.shape[2]
    tm, tf = min(m, 512), 1024
    return pl.pallas_call(
        _mlp_body,
        grid=(m // tm, ff // tf),
        in_specs=[pl.BlockSpec((tm, d), lambda i, f: (i, 0)),
                  pl.BlockSpec((None, 1, d), lambda i, f: (l, 0, 0)),
                  pl.BlockSpec((None, d, tf), lambda i, f: (l, 0, f)),
                  pl.BlockSpec((None, tf, d), lambda i, f: (l, f, 0)),
                  pl.BlockSpec((None, 1, d), lambda i, f: (l, 0, 0))],
        out_specs=pl.BlockSpec((tm, d), lambda i, f: (i, 0)),
        out_shape=jax.ShapeDtypeStruct((m, d), F32),
        scratch_shapes=[pltpu.VMEM((tm, d), BF16), pltpu.VMEM((tm, d), F32)],
        compiler_params=_cparams("parallel", "arbitrary"),
        name="mlp",
    )(x, g1, w_up, w_down, g2)


def _mem_attn_body(q_ref, k_ref, v_ref, o_ref):
    scale = MEM_HD ** -0.5
    heads = range(MEM_HEADS)
    sls = [slice(h * MEM_HD, (h + 1) * MEM_HD) for h in heads]
    head = (lambda ref, h: ref[:, h, :]) if len(k_ref.shape) == 3 else (lambda ref, h: ref[:, sls[h]])
    s = [_bdot_nt(q_ref[:, sls[h]], head(k_ref, h)) * scale for h in heads]
    e = [jnp.exp(s[h] - jnp.max(s[h], axis=-1, keepdims=True)) for h in heads]
    p = [e[h] / jnp.sum(e[h], axis=-1, keepdims=True) for h in heads]
    o = [_bdot(p[h], head(v_ref, h)) for h in heads]
    for h in heads:
        o_ref[:, sls[h]] = o[h].astype(o_ref.dtype)


def mem_attn(proj, q_col_block, mem_k, mem_v, v_col_block, l, rows_per_seq, tr):
    m = proj.shape[0]
    tiles_per_seq = rows_per_seq // tr
    if mem_k.ndim == 5:
        blk = (None, None, N_MEM, MEM_HEADS, MEM_HD)
        k_spec = pl.BlockSpec(blk, lambda i: (l, i // tiles_per_seq, 0, 0, 0))
        v_spec = k_spec
    else:
        blk = (None, N_MEM, MEM_W)
        k_spec = pl.BlockSpec(blk, lambda i: (l, i // tiles_per_seq, 0))
        v_spec = pl.BlockSpec(blk, lambda i: (l, i // tiles_per_seq, v_col_block))
    return pl.pallas_call(
        _mem_attn_body,
        grid=(m // tr,),
        in_specs=[pl.BlockSpec((tr, MEM_W), lambda i: (i, q_col_block)), k_spec, v_spec],
        out_specs=pl.BlockSpec((tr, MEM_W), lambda i: (i, 0)),
        out_shape=jax.ShapeDtypeStruct((m, MEM_W), BF16),
        compiler_params=_cparams("parallel"),
        name="mem_attn",
    )(proj, mem_k, mem_v)


def _swa_core(q_ref, k_tiles, v_tiles, valid, sink_ref, o_ref, lq):
    lane = lax.broadcasted_iota(jnp.int32, (lq, LANES), 1)
    low = lane < HEAD_D
    groups = range(SWA_KV)
    s = []
    for j in groups:
        xs = []
        for i in range(SWA_G // 2):
            c0 = (j * SWA_G + 2 * i) * HEAD_D
            pair = q_ref[:, c0:c0 + LANES] * (HEAD_D ** -0.5)
            xs.append(jnp.where(low, pair, 0.0).astype(BF16))
            xs.append(jnp.where(low, 0.0, pair).astype(BF16))
        x = jnp.concatenate(xs, axis=0)
        kj = jnp.concatenate([t.astype(BF16) for t in k_tiles(j)], axis=0)
        sj = lax.dot_general(x, kj, _NT, preferred_element_type=F32)
        s.append(sj if valid is None else jnp.where(valid, sj, NEG))
    p = []
    for j in groups:
        sink = sink_ref[j]
        mx = jnp.maximum(jnp.max(s[j], axis=-1, keepdims=True), sink)
        e = jnp.exp(s[j] - mx)
        p.append((e / (jnp.sum(e, axis=-1, keepdims=True) + jnp.exp(sink - mx))).astype(BF16))
    o = [jnp.dot(p[j], jnp.concatenate([t.astype(BF16) for t in v_tiles(j)], axis=0),
                 preferred_element_type=F32) for j in groups]
    for j in groups:
        for i in range(SWA_G // 2):
            c0 = (j * SWA_G + 2 * i) * HEAD_D
            top = o[j][(2 * i) * lq:(2 * i + 1) * lq]
            bot = o[j][(2 * i + 1) * lq:(2 * i + 2) * lq]
            o_ref[:, c0:c0 + LANES] = jnp.where(low, top, bot).astype(o_ref.dtype)


def _swa_prompt_body(q_ref, k0, k1, k2, v0, v1, v2, sink_ref, o_ref):
    c = pl.program_id(1)
    col = lax.broadcasted_iota(jnp.int32, (SWA_G * CHUNK, 3 * CHUNK), 1)
    valid = col >= jnp.maximum(2 - c, 0) * CHUNK
    _swa_core(q_ref,
              lambda j: [r[:, j * LANES:(j + 1) * LANES] for r in (k0, k1, k2)],
              lambda j: [r[:, j * LANES:(j + 1) * LANES] for r in (v0, v1, v2)],
              valid, sink_ref, o_ref, CHUNK)


def swa_prompt(proj, kvd, sink_col, batch, seq):
    nc = seq // CHUNK
    kd_w = SWA_KV * LANES

    def kv_spec(back, col):
        return pl.BlockSpec((CHUNK, kd_w), lambda b, c: (b * nc + jnp.maximum(c - back, 0), col))

    return pl.pallas_call(
        _swa_prompt_body,
        grid=(batch, nc),
        in_specs=[pl.BlockSpec((CHUNK, TOK_W), lambda b, c: (b * nc + c, 0)),
                  kv_spec(2, 0), kv_spec(1, 0), kv_spec(0, 0),
                  kv_spec(2, 1), kv_spec(1, 1), kv_spec(0, 1),
                  pl.BlockSpec((SWA_KV, SWA_G * CHUNK, 1), lambda b, c: (0, 0, 0))],
        out_specs=pl.BlockSpec((CHUNK, TOK_W), lambda b, c: (b * nc + c, 0)),
        out_shape=jax.ShapeDtypeStruct((batch * seq, TOK_W), BF16),
        compiler_params=_cparams("parallel", "arbitrary"),
        name="swa_prompt",
    )(proj, kvd, kvd, kvd, kvd, kvd, kvd, sink_col)


def _swa_sample_body(q_ref, k_ref, v_ref, sink_ref, o_ref, *, lq):
    _swa_core(q_ref,
              lambda j: [k_ref[:, j * LANES:(j + 1) * LANES]],
              lambda j: [v_ref[:, j * LANES:(j + 1) * LANES]],
              None, sink_ref, o_ref, lq)


def swa_sample(proj, k_all, v_all, sink_col, batch, lq):
    keys = k_all.shape[1]
    kd_w = SWA_KV * LANES
    return pl.pallas_call(
        functools.partial(_swa_sample_body, lq=lq),
        grid=(batch,),
        in_specs=[pl.BlockSpec((lq, TOK_W), lambda b: (b, 0)),
                  pl.BlockSpec((None, keys, kd_w), lambda b: (b, 0, 0)),
                  pl.BlockSpec((None, keys, kd_w), lambda b: (b, 0, 0)),
                  pl.BlockSpec((SWA_KV, SWA_G * lq, 1), lambda b: (0, 0, 0))],
        out_specs=pl.BlockSpec((lq, TOK_W), lambda b: (b, 0)),
        out_shape=jax.ShapeDtypeStruct((batch * lq, TOK_W), BF16),
        compiler_params=_cparams("parallel"),
        name="swa_sample",
    )(proj, k_all, v_all, sink_col)


def _pair_ones():
    r = lax.broadcasted_iota(jnp.int32, (LANES, LANES), 0)
    c = lax.broadcasted_iota(jnp.int32, (LANES, LANES), 1)
    return jnp.where((r < HEAD_D) == (c < HEAD_D), 1.0, 0.0).astype(BF16)


def _head_sum(x, ones_bd):
    return jnp.dot(x.astype(BF16), ones_bd, preferred_element_type=F32)


def _rwkv_pre_body(p_ref, prev_ref, mu_ref, w0_ref, w2_ref, a0_ref, a2_ref, g2_ref, kk_ref, ka_ref,
                   rk_ref, *rest, has_vres):
    if has_vres:
        vf_ref, v0_ref, v1_ref, v2_ref = rest[:4]
        outs = rest[4:]
    else:
        outs = rest
    r_o, lw_o, k_o, v_o, kn_o, b_o, g_o, bonus_o = outs
    tr = p_ref.shape[0]
    row0 = lax.broadcasted_iota(jnp.int32, (tr, LANES), 0) == 0
    ones_bd = _pair_ones()

    def mixed(c0, width):
        outs_ = []
        for t in range(width // LANES):
            sl = slice(c0 + t * LANES, c0 + (t + 1) * LANES)
            p = p_ref[:, sl]
            pp = jnp.where(row0, prev_ref[:, sl], pltpu.roll(p, 1, 0))
            outs_.append(p + (pp - p) * mu_ref[:, sl])
        return outs_

    lora = mixed(LORA_OFF, LANES)[0]
    gate_in = jnp.concatenate(mixed(GATE_OFF, SHIFT_PAD - GATE_OFF), axis=1)
    w_pre = w0_ref[...] + _bdot(jnp.tanh(lora), w2_ref[...])
    lw_o[...] = -jnp.exp(-jax.nn.softplus(-w_pre) - 0.5)
    a = jax.nn.sigmoid(a0_ref[...] + _bdot(lora, a2_ref[...]))
    g_o[...] = _bdot(jax.nn.sigmoid(gate_in), g2_ref[...])

    r_t = mixed(0, TOK_W)
    k_t = mixed(TOK_W, TOK_W)
    v_t = mixed(2 * TOK_W, TOK_W)
    if has_vres:
        v_all = jnp.concatenate(v_t, axis=1)
        gate = jax.nn.sigmoid(v0_ref[...] + _bdot(_bdot(v_all, v1_ref[...]), v2_ref[...]))
        v_all = v_all + (vf_ref[...] - v_all) * gate
        v_t = [v_all[:, t * LANES:(t + 1) * LANES] for t in range(N_PAIRS)]
    for t in range(N_PAIRS):
        sl = slice(t * LANES, (t + 1) * LANES)
        kraw = k_t[t]
        kk = kraw * kk_ref[:, sl]
        kk = kk / jnp.maximum(jnp.sqrt(_head_sum(kk * kk, ones_bd)), 1e-12)
        a_t = a[:, sl]
        k = kraw * (1.0 + (a_t - 1.0) * ka_ref[:, sl])
        v = v_t[t]
        r_o[:, sl] = r_t[t]
        k_o[:, sl] = k
        v_o[:, sl] = v
        kn_o[:, sl] = kk
        b_o[:, sl] = kk * a_t
        bonus_o[:, sl] = _head_sum(r_t[t] * k * rk_ref[:, sl], ones_bd) * v


def rwkv_pre(proj, prev, pw, v_first, tr):
    m = proj.shape[0]
    has_vres = v_first is not None
    row = lambda w: pl.BlockSpec((1, w), lambda i: (0, 0))
    full = lambda a: pl.BlockSpec(a.shape, lambda i: (0,) * a.ndim)
    tile = pl.BlockSpec((tr, TOK_W), lambda i: (i, 0))
    in_specs = [pl.BlockSpec((tr, A_IN_PAD), lambda i: (i, 0)),
                pl.BlockSpec((None, 1, A_IN_PAD), lambda i: (i, 0, 0)),
                row(SHIFT_PAD), row(TOK_W), full(pw["w2"]), row(TOK_W), full(pw["a2"]), full(pw["g2"]),
                row(TOK_W), row(TOK_W), row(TOK_W)]
    args = [proj, prev, pw["mu"], pw["w0"], pw["w2"], pw["a0"], pw["a2"], pw["g2"], pw["k_k"],
            pw["k_a"], pw["r_k"]]
    if has_vres:
        in_specs += [tile, row(TOK_W), full(pw["v1"]), full(pw["v2"])]
        args += [v_first, pw["v0"], pw["v1"], pw["v2"]]
    return pl.pallas_call(
        functools.partial(_rwkv_pre_body, has_vres=has_vres),
        grid=(m // tr,),
        in_specs=in_specs,
        out_specs=[tile] * 8,
        out_shape=[jax.ShapeDtypeStruct((m, TOK_W), F32)] * 8,
        compiler_params=_cparams("parallel"),
        name="rwkv_pre",
    )(*args)


def _wkv_body(r_ref, lw_ref, k_ref, v_ref, kn_ref, b_ref, g_ref, bonus_ref, s0_ref, lg_ref, lb_ref,
              o_ref, s_out_ref, s_ref, *, chunk):
    c = pl.program_id(1)
    L = chunk
    W2 = 2 * L

    @pl.when(c == 0)
    def _():
        s_ref[...] = s0_ref[...]

    t_i = lax.broadcasted_iota(jnp.int32, (L, W2), 0)
    l_i = lax.broadcasted_iota(jnp.int32, (L, W2), 1)
    j_i = jnp.where(l_i >= L, l_i - L, l_i)
    strict = t_i > j_i
    incl = t_i >= j_i
    eye2 = jnp.where(t_i == j_i, 1.0, 0.0)
    levels = []
    s = 1
    while s < L:
        sh = s.bit_length() - 1
        levels.append(((t_i >> (sh + 1)) == (j_i >> (sh + 1)))
                      & (((t_i >> sh) & 1) == 1) & (((j_i >> sh) & 1) == 0))
        s *= 2
    first_t = l_i < L
    lane = lax.broadcasted_iota(jnp.int32, (L, LANES), 1)
    low = lane < HEAD_D
    rr = lax.broadcasted_iota(jnp.int32, (LANES, LANES), 0)
    cc = lax.broadcasted_iota(jnp.int32, (LANES, LANES), 1)
    bd = (rr < HEAD_D) == (cc < HEAD_D)
    ones_bd = jnp.where(bd, 1.0, 0.0).astype(BF16)
    tri = jnp.where(lax.broadcasted_iota(jnp.int32, (L, L), 0)
                    >= lax.broadcasted_iota(jnp.int32, (L, L), 1), 1.0, 0.0)

    def stack_t(x):
        return jnp.concatenate([jnp.where(first_t, x, 0.0), jnp.where(first_t, 0.0, x)],
                               axis=0).astype(BF16)

    def stack_d(x):
        return jnp.concatenate([jnp.where(low, x, 0.0), jnp.where(low, 0.0, x)], axis=0).astype(BF16)

    def tdot(m, stacked):
        return jnp.dot(m.astype(BF16), stacked, preferred_element_type=F32)

    cum_all = jnp.dot(tri, lw_ref[...], precision=lax.Precision.HIGHEST, preferred_element_type=F32)

    pairs = range(N_PAIRS)
    sls = [slice(p * LANES, (p + 1) * LANES) for p in pairs]
    a_t, r_t, sv, bh, kh, dec, cb, ck = [], [], [], [], [], [], [], []
    for p in pairs:
        sl = sls[p]
        lw = lw_ref[:, sl]
        cum = cum_all[:, sl]
        cum_l = cum[L - 1:L, :]
        p_inv = jnp.exp(-cum)
        p_end = jnp.exp(cum_l - cum)
        b = b_ref[:, sl]
        k = k_ref[:, sl]
        a_t.append(-kn_ref[:, sl] * jnp.exp(cum - lw))
        r_t.append(r_ref[:, sl] * jnp.exp(cum))
        sv.append(stack_d(v_ref[:, sl]))
        bh.append((b * p_end).astype(BF16))
        kh.append((k * p_end).astype(BF16))
        dec.append(jnp.exp(cum_l))
        x = jnp.concatenate([a_t[p], r_t[p]], axis=0).astype(BF16)
        cb.append(lax.dot_general(x, stack_d(b * p_inv), _NT, preferred_element_type=F32))
        ck.append(lax.dot_general(x, stack_d(k * p_inv), _NT, preferred_element_type=F32))
    a_ab = [jnp.where(strict, cb[p][:L], 0.0) for p in pairs]
    a_rb = [jnp.where(incl, cb[p][L:], 0.0).astype(BF16) for p in pairs]
    a_k = [jnp.concatenate([jnp.where(strict, ck[p][:L], 0.0), jnp.where(incl, ck[p][L:], 0.0)],
                           axis=0).astype(BF16) for p in pairs]
    kv = [jnp.dot(a_k[p], sv[p], preferred_element_type=F32) for p in pairs]
    t_m = [eye2 + jnp.where(levels[0], a_ab[p], 0.0) for p in pairs]
    for msk in levels[1:]:
        ta = [tdot(t_m[p], stack_t(jnp.where(msk, a_ab[p], 0.0))) for p in pairs]
        t_m = [t_m[p] + tdot(ta[p], stack_t(t_m[p])) for p in pairs]
    t_b = [t_m[p].astype(BF16) for p in pairs]
    w12 = [jnp.dot(t_b[p], jnp.concatenate([stack_d(a_t[p]), stack_d(kv[p][:L])], axis=1),
                   preferred_element_type=F32) for p in pairs]
    qq = [jnp.dot(a_rb[p], jnp.concatenate([stack_d(w12[p][:, :LANES]), stack_d(w12[p][:, LANES:])],
                                           axis=1), preferred_element_type=F32) for p in pairs]
    mm = [_bdot_tn(w12[p], bh[p]) for p in pairs]
    mk = [_bdot_tn(v_ref[:, sls[p]], kh[p]) for p in pairs]
    y = []
    for p in pairs:
        s_p = s_ref[p]
        q1 = r_t[p] + qq[p][:, :LANES]
        q2 = qq[p][:, LANES:] + kv[p][L:]
        m1 = jnp.where(bd, mm[p][:LANES], 0.0)
        m2 = jnp.where(bd, mm[p][LANES:] + mk[p], 0.0)
        y.append(_bdot_nt(q1, s_p) + q2)
        s_ref[p] = s_p * dec[p] + _bdot(s_p, m1) + m2
    mean = [_head_sum(y[p], ones_bd) * (1.0 / HEAD_D) for p in pairs]
    dv = [y[p] - mean[p] for p in pairs]
    var = [_head_sum(dv[p] * dv[p], ones_bd) * (1.0 / HEAD_D) for p in pairs]
    for p in pairs:
        sl = sls[p]
        yn = dv[p] * lax.rsqrt(var[p] + GN_EPS) * lg_ref[:, sl] + lb_ref[:, sl]
        o_ref[:, sl] = ((yn + bonus_ref[:, sl]) * g_ref[:, sl]).astype(o_ref.dtype)

    @pl.when(c == pl.num_programs(1) - 1)
    def _():
        s_out_ref[...] = s_ref[...]


def wkv(r, lw, k, v, kn, b, g, bonus, s0, lnx_g, lnx_b, batch, seq, chunk):
    nc = seq // chunk
    tile = pl.BlockSpec((chunk, TOK_W), lambda bi, c: (bi * nc + c, 0))
    state = pl.BlockSpec((None, N_PAIRS, LANES, LANES), lambda bi, c: (bi, 0, 0, 0))
    row = pl.BlockSpec((1, TOK_W), lambda bi, c: (0, 0))
    return pl.pallas_call(
        functools.partial(_wkv_body, chunk=chunk),
        grid=(batch, nc),
        in_specs=[tile] * 8 + [state, row, row],
        out_specs=[tile, state],
        out_shape=[jax.ShapeDtypeStruct((batch * seq, TOK_W), BF16),
                   jax.ShapeDtypeStruct((batch, N_PAIRS, LANES, LANES), F32)],
        scratch_shapes=[pltpu.VMEM((N_PAIRS, LANES, LANES), F32)],
        compiler_params=_cparams("parallel", "arbitrary"),
        name="wkv",
    )(r, lw, k, v, kn, b, g, bonus, s0, lnx_g.reshape(1, TOK_W), lnx_b.reshape(1, TOK_W))


def _state_to_pairs(s):
    b = s.shape[0]
    s = s.reshape(b, N_PAIRS, 2, HEAD_D, HEAD_D)
    z = jnp.zeros_like(s[:, :, 0])
    top = jnp.concatenate([s[:, :, 0], z], axis=-1)
    bot = jnp.concatenate([z, s[:, :, 1]], axis=-1)
    return jnp.concatenate([top, bot], axis=-2)


def _pairs_to_state(sp):
    b = sp.shape[0]
    h0 = sp[:, :, :HEAD_D, :HEAD_D]
    h1 = sp[:, :, HEAD_D:, HEAD_D:]
    return jnp.stack([h0, h1], axis=2).reshape(b, N_HEADS, HEAD_D, HEAD_D)


def _dup_heads(w):
    lead = w.shape[:-1]
    w = w.reshape(lead + (SWA_KV, 1, HEAD_D))
    return jnp.broadcast_to(w, lead + (SWA_KV, 2, HEAD_D)).reshape(lead + (SWA_KV * LANES,))


def _undup_heads(x):
    lead = x.shape[:-1]
    return x.reshape(lead + (SWA_KV, 2, HEAD_D))[..., 0, :]


def _pad_cols(a, width):
    return jnp.pad(a, [(0, 0)] * (a.ndim - 1) + [(0, width - a.shape[-1])])


def _sink_col(sinks, lq):
    return jnp.repeat(sinks.astype(F32).reshape(SWA_KV, SWA_G), lq, axis=1)[..., None]


def _trunk(x, mem_k, mem_v, wkv0, shift0, past_k, past_v, P, batch, seq, tr_pre, tr_mem):
    m = batch * seq
    chunk = min(CHUNK, seq)
    new_wkv, new_shift = [], []
    v_first = None
    kvd = None
    for l in range(DEPTH):
        if l < N_A:
            pw = P["a"][l]
            proj = norm_matmul(x, P["ln_mix_pre"], l, pw["w_in"], 0)
            p3 = proj.reshape(batch, seq // tr_pre, tr_pre, A_IN_PAD)
            first = _pad_cols(shift0[l].astype(F32), A_IN_PAD).reshape(batch, 1, A_IN_PAD)
            prev = jnp.concatenate([first, p3[:, :-1, -1, :]], axis=1).reshape(m // tr_pre, 1, A_IN_PAD)
            new_shift.append(p3[:, -1, -1, :A_SHIFT].reshape(batch, 1, A_SHIFT))
            r, lw, k, v, kn, b, g, bonus = rwkv_pre(proj, prev, pw, v_first, tr_pre)
            if l == 0:
                v_first = v
            tok, s_new = wkv(r, lw, k, v, kn, b, g, bonus, _state_to_pairs(wkv0[l].astype(F32)),
                             pw["lnx_g"], pw["lnx_b"], batch, seq, chunk)
            new_wkv.append(_pairs_to_state(s_new))
            q_col = A_QMEM_OFF // MEM_W
            w_out, wi = P["a_w_out"], l
        else:
            i = l - N_A
            if kvd is None:
                kvd = norm_matmul(x, P["kv_norm_g"], 0, P["w_kvd"], 0)
            proj = norm_matmul(x, P["ln_mix_pre"], l, P["b_w_in"], i)
            if past_k is None:
                tok = swa_prompt(proj, kvd, _sink_col(P["b_sinks"][i], CHUNK), batch, seq)
            else:
                kd_w = SWA_KV * LANES
                k_all = jnp.concatenate([_dup_heads(past_k.reshape(batch, -1, SWA_KV * HEAD_D)),
                                         kvd[:, :kd_w].reshape(batch, seq, kd_w)], axis=1)
                v_all = jnp.concatenate([_dup_heads(past_v.reshape(batch, -1, SWA_KV * HEAD_D)),
                                         kvd[:, kd_w:].reshape(batch, seq, kd_w)], axis=1)
                tok = swa_sample(proj, k_all, v_all, _sink_col(P["b_sinks"][i], seq), batch, seq)
            q_col = TOK_W // MEM_W
            w_out, wi = P["b_w_out"], i
        mem_o = mem_attn(proj, q_col, mem_k, mem_v, 1, l, seq, tr_mem)
        x = mix_out(tok, mem_o, w_out, wi, P["ln_mix_post"], l, x)
        x = mlp(x, P["ln_mlp_pre"], P["w_up"], P["w_down"], P["ln_mlp_post"], l)
    kd_w = SWA_KV * LANES
    k_new = _undup_heads(kvd[:, :kd_w]).reshape(batch, seq, SWA_KV, HEAD_D)
    v_new = _undup_heads(kvd[:, kd_w:]).reshape(batch, seq, SWA_KV, HEAD_D)
    return x, jnp.stack(new_wkv), jnp.stack(new_shift), k_new, v_new


def _prep_params(ln_mix_pre, ln_mix_post, ln_mlp_pre, ln_mlp_post, mem_norm_g, w_mem_kv, w_up, w_down,
                 a_w_in, a_mu, a_w0, a_w2, a_a0, a_a2, a_g2, a_k_k, a_k_a, a_r_k, a_lnx_g, a_lnx_b,
                 a_v0, a_v1, a_v2, a_w_out, kv_norm_g, w_kv, b_w_in, b_sinks, b_w_out):
    d = D_MODEL
    zrow = lambda n: jnp.zeros((n, TOK_W), F32)
    a_params = []
    for i in range(N_A):
        w_in = a_w_in[i]
        w_in = jnp.concatenate([w_in[:, :A_SHIFT], jnp.zeros((d, A_QMEM_OFF - A_SHIFT), F32),
                                w_in[:, A_SHIFT:]], axis=1).astype(BF16)
        pw = {
            "w_in": w_in[None],
            "mu": _pad_cols(a_mu[i].reshape(1, A_SHIFT), SHIFT_PAD),
            "w0": a_w0[i].reshape(1, TOK_W), "a0": a_a0[i].reshape(1, TOK_W),
            "w2": jnp.concatenate([a_w2[i], zrow(A_LORA)], axis=0).astype(BF16),
            "a2": jnp.concatenate([zrow(W_LORA), a_a2[i]], axis=0).astype(BF16),
            "g2": jnp.concatenate([a_g2[i], zrow(SHIFT_PAD - A_SHIFT)], axis=0).astype(BF16),
            "k_k": a_k_k[i].reshape(1, TOK_W), "k_a": a_k_a[i].reshape(1, TOK_W),
            "r_k": a_r_k[i].reshape(1, TOK_W),
            "lnx_g": a_lnx_g[i], "lnx_b": a_lnx_b[i],
        }
        if i > 0:
            pw["v0"] = a_v0[i - 1].reshape(1, TOK_W)
            pw["v1"] = a_v1[i - 1].astype(BF16)
            pw["v2"] = a_v2[i - 1].astype(BF16)
        a_params.append(pw)
    kw = SWA_KV * HEAD_D
    gain = lambda g: g.reshape(-1, 1, d)
    P = {
        "a": a_params,
        "ln_mix_pre": gain(ln_mix_pre), "ln_mix_post": gain(ln_mix_post), "ln_mlp_pre": gain(ln_mlp_pre),
        "ln_mlp_post": gain(ln_mlp_post), "mem_norm_g": gain(mem_norm_g), "kv_norm_g": gain(kv_norm_g),
        "w_up": w_up.astype(BF16), "w_down": w_down.astype(BF16), "w_mem_kv": w_mem_kv.astype(BF16),
        "w_kvd": jnp.concatenate([_dup_heads(w_kv[:, :kw]), _dup_heads(w_kv[:, kw:])],
                                 axis=1).astype(BF16)[None],
        "a_w_out": a_w_out.astype(BF16),
        "b_w_in": b_w_in.astype(BF16), "b_sinks": b_sinks, "b_w_out": b_w_out.astype(BF16),
    }
    return P


def kernel(x_prompt, x_sample, mem_prompt, state_wkv, state_shift, cache_win_k, cache_win_v, cache_mem_k, cache_mem_v, ln_mix_pre, ln_mix_post, ln_mlp_pre, ln_mlp_post, mem_norm_g, w_mem_kv, w_up, w_down, a_w_in, a_mu, a_w0, a_w2, a_a0, a_a2, a_g2, a_k_k, a_k_a, a_r_k, a_lnx_g, a_lnx_b, a_v0, a_v1, a_v2, a_w_out, kv_norm_g, w_kv, b_w_in, b_sinks, b_w_out):
    bp, sp, d = x_prompt.shape
    bs, ss, _ = x_sample.shape
    P = _prep_params(ln_mix_pre, ln_mix_post, ln_mlp_pre, ln_mlp_post, mem_norm_g, w_mem_kv, w_up, w_down,
                     a_w_in, a_mu, a_w0, a_w2, a_a0, a_a2, a_g2, a_k_k, a_k_a, a_r_k, a_lnx_g, a_lnx_b,
                     a_v0, a_v1, a_v2, a_w_out, kv_norm_g, w_kv, b_w_in, b_sinks, b_w_out)

    mkv = norm_matmul_layers(mem_prompt.reshape(bp * N_MEM, d), P["mem_norm_g"], P["w_mem_kv"])
    wkv0 = jnp.zeros((N_A, bp, N_HEADS, HEAD_D, HEAD_D), F32)
    shift0 = jnp.zeros((N_A, bp, 1, A_SHIFT), F32)
    y_p, wkv_p, shift_p, k_p, v_p = _trunk(
        x_prompt.reshape(bp * sp, d), mkv, mkv, wkv0, shift0, None, None, P, bp, sp,
        tr_pre=256, tr_mem=512)

    y_s, wkv_s, shift_s, k_s, v_s = _trunk(
        x_sample.reshape(bs * ss, d), cache_mem_k, cache_mem_v, state_wkv, state_shift, cache_win_k,
        cache_win_v, P, bs, ss, tr_pre=ss, tr_mem=ss)
    past = cache_win_k.shape[1]
    win_k_s = jnp.concatenate([cache_win_k, k_s], axis=1)[:, -past:]
    win_v_s = jnp.concatenate([cache_win_v, v_s], axis=1)[:, -past:]

    return (y_p.reshape(bp, sp, d), y_s.reshape(bs, ss, d), wkv_p, shift_p,
            k_p[:, -WINDOW:], v_p[:, -WINDOW:],
            mkv[:, :, :MEM_W].reshape(DEPTH, bp, N_MEM, MEM_HEADS, MEM_HD),
            mkv[:, :, MEM_W:].reshape(DEPTH, bp, N_MEM, MEM_HEADS, MEM_HD),
            wkv_s, shift_s, win_k_s, win_v_s)
```

```python
import functools

import jax
import jax.numpy as jnp
from jax import lax
from jax.experimental import pallas as pl
from jax.experimental.pallas import tpu as pltpu

F32 = jnp.float32
BF16 = jnp.bfloat16

D_MODEL = 2048
DEPTH = 4
N_A = 2
CHUNK = 64
N_MEM = 256
MEM_HEADS = 4
MEM_HD = 128
MEM_W = MEM_HEADS * MEM_HD
TOK_W = D_MODEL - MEM_W
HEAD_D = 64
N_HEADS = TOK_W // HEAD_D
N_PAIRS = N_HEADS // 2
W_LORA = 64
A_LORA = 64
G_LORA = 224
A_SHIFT = 3 * TOK_W + W_LORA + A_LORA + G_LORA
LORA_OFF = 3 * TOK_W
GATE_OFF = LORA_OFF + 128
SHIFT_PAD = 4992
A_QMEM_OFF = 5120
A_IN_PAD = A_QMEM_OFF + MEM_W
SWA_KV = 4
SWA_G = N_HEADS // SWA_KV
WINDOW = 128
D_FF = 4 * D_MODEL
RMS_EPS = 1e-6
GN_EPS = 64e-5
NEG = -1e30
LANES = 128
VMEM_LIMIT = 56 * 1024 * 1024

_NT = (((1,), (1,)), ((), ()))
_TN = (((0,), (0,)), ((), ()))


def _cparams(*sem):
    return pltpu.CompilerParams(dimension_semantics=sem, vmem_limit_bytes=VMEM_LIMIT)


def _bdot(a, b):
    return jnp.dot(a.astype(BF16), b.astype(BF16), preferred_element_type=F32)


def _bdot_nt(a, b):
    return lax.dot_general(a.astype(BF16), b.astype(BF16), _NT, preferred_element_type=F32)


def _bdot_tn(a, b):
    return lax.dot_general(a.astype(BF16), b.astype(BF16), _TN, preferred_element_type=F32)


def _rms(x, g):
    return x * lax.rsqrt(jnp.mean(x * x, axis=-1, keepdims=True) + RMS_EPS) * g


def _norm_matmul_body(x_ref, g_ref, w_ref, o_ref, xn_ref):
    @pl.when(pl.program_id(1) == 0)
    def _():
        xn_ref[...] = _rms(x_ref[...], g_ref[...]).astype(BF16)

    o_ref[...] = jnp.dot(xn_ref[...], w_ref[...], preferred_element_type=F32).astype(o_ref.dtype)


def _proj_tiles(m, n):
    tm = min(m, 1024)
    for tn in (1408, 1024, 512):
        if n % tn == 0:
            return tm, tn
    raise ValueError(f"unsupported projection width {n}")


def norm_matmul(x, g, gl, w, wl):
    m, d = x.shape
    n = w.shape[2]
    tm, tn = _proj_tiles(m, n)
    return pl.pallas_call(
        _norm_matmul_body,
        grid=(m // tm, n // tn),
        in_specs=[pl.BlockSpec((tm, d), lambda i, j: (i, 0)),
                  pl.BlockSpec((None, 1, d), lambda i, j: (gl, 0, 0)),
                  pl.BlockSpec((None, d, tn), lambda i, j: (wl, 0, j))],
        out_specs=pl.BlockSpec((tm, tn), lambda i, j: (i, j)),
        out_shape=jax.ShapeDtypeStruct((m, n), F32),
        scratch_shapes=[pltpu.VMEM((tm, d), BF16)],
        compiler_params=_cparams("parallel", "arbitrary"),
        name="norm_matmul",
    )(x, g, w)


def norm_matmul_layers(x, g, w):
    m, d = x.shape
    nl, _, n = w.shape
    _, tn = _proj_tiles(m, n)
    return pl.pallas_call(
        _norm_matmul_body,
        grid=(nl, n // tn),
        in_specs=[pl.BlockSpec((m, d), lambda l, j: (0, 0)),
                  pl.BlockSpec((None, 1, d), lambda l, j: (l, 0, 0)),
                  pl.BlockSpec((None, d, tn), lambda l, j: (l, 0, j))],
        out_specs=pl.BlockSpec((None, m, tn), lambda l, j: (l, 0, j)),
        out_shape=jax.ShapeDtypeStruct((nl, m, n), F32),
        scratch_shapes=[pltpu.VMEM((m, d), BF16)],
        compiler_params=_cparams("parallel", "arbitrary"),
        name="norm_matmul_layers",
    )(x, g, w)


def _mix_out_body(tok_ref, mem_ref, wt_ref, wm_ref, g_ref, x_ref, o_ref):
    y = jnp.dot(tok_ref[...], wt_ref[...], preferred_element_type=F32)
    y = y + jnp.dot(mem_ref[...], wm_ref[...], preferred_element_type=F32)
    o_ref[...] = x_ref[...] + _rms(y, g_ref[...])


def mix_out(tok, mem, w_out, wl, g, gl, x):
    m, d = x.shape
    tm = min(m, 512)
    return pl.pallas_call(
        _mix_out_body,
        grid=(m // tm,),
        in_specs=[pl.BlockSpec((tm, TOK_W), lambda i: (i, 0)),
                  pl.BlockSpec((tm, MEM_W), lambda i: (i, 0)),
                  pl.BlockSpec((None, TOK_W, d), lambda i: (wl, 0, 0)),
                  pl.BlockSpec((None, MEM_W, d), lambda i: (wl, TOK_W // MEM_W, 0)),
                  pl.BlockSpec((None, 1, d), lambda i: (gl, 0, 0)),
                  pl.BlockSpec((tm, d), lambda i: (i, 0))],
        out_specs=pl.BlockSpec((tm, d), lambda i: (i, 0)),
        out_shape=jax.ShapeDtypeStruct((m, d), F32),
        compiler_params=_cparams("parallel"),
        name="mix_out",
    )(tok, mem, w_out, w_out, g, x)


def _mlp_body(x_ref, g1_ref, wu_ref, wd_ref, g2_ref, o_ref, xn_ref):
    f = pl.program_id(1)

    @pl.when(f == 0)
    def _():
        xn_ref[...] = _rms(x_ref[...], g1_ref[...]).astype(BF16)
        o_ref[...] = jnp.zeros_like(o_ref)

    h = jnp.maximum(jnp.dot(xn_ref[...], wu_ref[...].astype(BF16), preferred_element_type=F32), 0.0)
    o_ref[...] += jnp.dot((h * h).astype(BF16), wd_ref[...].astype(BF16), preferred_element_type=F32)

    @pl.when(f == pl.num_programs(1) - 1)
    def _():
        o_ref[...] = x_ref[...] + _rms(o_ref[...], g2_ref[...])


def mlp(x, g1, w_up, w_down, g2, l):
    m, d = x.shape
    ff = w_up.shape[2]
    tm, tf = min(m, 1024), 512
    once = pl.Buffered(1)
    return pl.pallas_call(
        _mlp_body,
        grid=(m // tm, ff // tf),
        in_specs=[pl.BlockSpec((tm, d), lambda i, f: (i, 0), pipeline_mode=once),
                  pl.BlockSpec((None, 1, d), lambda i, f: (l, 0, 0)),
                  pl.BlockSpec((None, d, tf), lambda i, f: (l, 0, f)),
                  pl.BlockSpec((None, tf, d), lambda i, f: (l, f, 0)),
                  pl.BlockSpec((None, 1, d), lambda i, f: (l, 0, 0))],
        out_specs=pl.BlockSpec((tm, d), lambda i, f: (i, 0)),
        out_shape=jax.ShapeDtypeStruct((m, d), F32),
        scratch_shapes=[pltpu.VMEM((tm, d), BF16)],
        compiler_params=_cparams("parallel", "arbitrary"),
        name="mlp",
    )(x, g1, w_up, w_down, g2)


def _mem_attn_body(q_ref, k_ref, v_ref, o_ref):
    scale = MEM_HD ** -0.5
    heads = range(MEM_HEADS)
    sls = [slice(h * MEM_HD, (h + 1) * MEM_HD) for h in heads]
    head = (lambda ref, h: ref[:, h, :]) if len(k_ref.shape) == 3 else (lambda ref, h: ref[:, sls[h]])
    s = [_bdot_nt(q_ref[:, sls[h]], head(k_ref, h)) * scale for h in heads]
    e = [jnp.exp(s[h] - jnp.max(s[h], axis=-1, keepdims=True)) for h in heads]
    p = [e[h] / jnp.sum(e[h], axis=-1, keepdims=True) for h in heads]
    o = [_bdot(p[h], head(v_ref, h)) for h in heads]
    for h in heads:
        o_ref[:, sls[h]] = o[h].astype(o_ref.dtype)


def mem_attn(proj, q_col_block, mem_k, mem_v, v_col_block, l, rows_per_seq, tr):
    m = proj.shape[0]
    tiles_per_seq = rows_per_seq // tr
    if mem_k.ndim == 5:
        blk = (None, None, N_MEM, MEM_HEADS, MEM_HD)
        k_spec = pl.BlockSpec(blk, lambda i: (l, i // tiles_per_seq, 0, 0, 0))
        v_spec = k_spec
    else:
        blk = (None, N_MEM, MEM_W)
        k_spec = pl.BlockSpec(blk, lambda i: (l, i // tiles_per_seq, 0))
        v_spec = pl.BlockSpec(blk, lambda i: (l, i // tiles_per_seq, v_col_block))
    return pl.pallas_call(
        _mem_attn_body,
        grid=(m // tr,),
        in_specs=[pl.BlockSpec((tr, MEM_W), lambda i: (i, q_col_block)), k_spec, v_spec],
        out_specs=pl.BlockSpec((tr, MEM_W), lambda i: (i, 0)),
        out_shape=jax.ShapeDtypeStruct((m, MEM_W), BF16),
        compiler_params=_cparams("parallel"),
        name="mem_attn",
    )(proj, mem_k, mem_v)


def _swa_core(q_ref, k_tiles, v_tiles, valid, sink_ref, o_ref, nb, lq):
    lane = lax.broadcasted_iota(jnp.int32, (lq, LANES), 1)
    low = lane < HEAD_D
    probs = [(b, j) for b in range(nb) for j in range(SWA_KV)]
    s = []
    for b, j in probs:
        xs = []
        for i in range(SWA_G // 2):
            c0 = (j * SWA_G + 2 * i) * HEAD_D
            pair = q_ref[b, :, c0:c0 + LANES] * (HEAD_D ** -0.5)
            xs.append(jnp.where(low, pair, 0.0).astype(BF16))
            xs.append(jnp.where(low, 0.0, pair).astype(BF16))
        x = jnp.concatenate(xs, axis=0)
        kj = jnp.concatenate([t.astype(BF16) for t in k_tiles(b, j)], axis=0)
        sj = lax.dot_general(x, kj, _NT, preferred_element_type=F32)
        s.append(sj if valid is None else jnp.where(valid, sj, NEG))
    p = []
    for n, (b, j) in enumerate(probs):
        sink = sink_ref[j]
        mx = jnp.maximum(jnp.max(s[n], axis=-1, keepdims=True), sink)
        e = jnp.exp(s[n] - mx)
        p.append((e / (jnp.sum(e, axis=-1, keepdims=True) + jnp.exp(sink - mx))).astype(BF16))
    o = [jnp.dot(p[n], jnp.concatenate([t.astype(BF16) for t in v_tiles(b, j)], axis=0),
                 preferred_element_type=F32) for n, (b, j) in enumerate(probs)]
    for n, (b, j) in enumerate(probs):
        for i in range(SWA_G // 2):
            c0 = (j * SWA_G + 2 * i) * HEAD_D
            top = o[n][(2 * i) * lq:(2 * i + 1) * lq]
            bot = o[n][(2 * i + 1) * lq:(2 * i + 2) * lq]
            o_ref[b, :, c0:c0 + LANES] = jnp.where(low, top, bot).astype(o_ref.dtype)


def _swa_prompt_body(q_ref, k0, k1, k2, v0, v1, v2, sink_ref, o_ref):
    c = pl.program_id(1)
    col = lax.broadcasted_iota(jnp.int32, (SWA_G * CHUNK, 3 * CHUNK), 1)
    valid = col >= jnp.maximum(2 - c, 0) * CHUNK
    _swa_core(q_ref,
              lambda b, j: [r[b, :, j * LANES:(j + 1) * LANES] for r in (k0, k1, k2)],
              lambda b, j: [r[b, :, j * LANES:(j + 1) * LANES] for r in (v0, v1, v2)],
              valid, sink_ref, o_ref, q_ref.shape[0], CHUNK)


def _batch_block(batch, want):
    return want if batch % want == 0 else 1


def swa_prompt(proj, kvd, sink_col, batch, seq):
    nc = seq // CHUNK
    kd_w = SWA_KV * LANES
    nb = _batch_block(batch, 2)

    def kv_spec(back, col):
        return pl.BlockSpec((nb, CHUNK, kd_w), lambda b, c: (b, jnp.maximum(c - back, 0), col))

    kv3 = kvd.reshape(batch, seq, 2 * kd_w)
    out = pl.pallas_call(
        _swa_prompt_body,
        grid=(batch // nb, nc),
        in_specs=[pl.BlockSpec((nb, CHUNK, TOK_W), lambda b, c: (b, c, 0)),
                  kv_spec(2, 0), kv_spec(1, 0), kv_spec(0, 0),
                  kv_spec(2, 1), kv_spec(1, 1), kv_spec(0, 1),
                  pl.BlockSpec((SWA_KV, SWA_G * CHUNK, 1), lambda b, c: (0, 0, 0))],
        out_specs=pl.BlockSpec((nb, CHUNK, TOK_W), lambda b, c: (b, c, 0)),
        out_shape=jax.ShapeDtypeStruct((batch, seq, TOK_W), BF16),
        compiler_params=_cparams("parallel", "arbitrary"),
        name="swa_prompt",
    )(proj.reshape(batch, seq, -1), kv3, kv3, kv3, kv3, kv3, kv3, sink_col)
    return out.reshape(batch * seq, TOK_W)


def _swa_sample_body(q_ref, k_ref, v_ref, sink_ref, o_ref):
    nb, lq, _ = q_ref.shape
    _swa_core(q_ref,
              lambda b, j: [k_ref[b, :, j * LANES:(j + 1) * LANES]],
              lambda b, j: [v_ref[b, :, j * LANES:(j + 1) * LANES]],
              None, sink_ref, o_ref, nb, lq)


def swa_sample(proj, k_all, v_all, sink_col, batch, lq):
    keys = k_all.shape[1]
    kd_w = SWA_KV * LANES
    nb = _batch_block(batch, 4)
    out = pl.pallas_call(
        _swa_sample_body,
        grid=(batch // nb,),
        in_specs=[pl.BlockSpec((nb, lq, TOK_W), lambda b: (b, 0, 0)),
                  pl.BlockSpec((nb, keys, kd_w), lambda b: (b, 0, 0)),
                  pl.BlockSpec((nb, keys, kd_w), lambda b: (b, 0, 0)),
                  pl.BlockSpec((SWA_KV, SWA_G * lq, 1), lambda b: (0, 0, 0))],
        out_specs=pl.BlockSpec((nb, lq, TOK_W), lambda b: (b, 0, 0)),
        out_shape=jax.ShapeDtypeStruct((batch, lq, TOK_W), BF16),
        compiler_params=_cparams("parallel"),
        name="swa_sample",
    )(proj.reshape(batch, lq, -1), k_all, v_all, sink_col)
    return out.reshape(batch * lq, TOK_W)


def _pair_ones():
    r = lax.broadcasted_iota(jnp.int32, (LANES, LANES), 0)
    c = lax.broadcasted_iota(jnp.int32, (LANES, LANES), 1)
    return jnp.where((r < HEAD_D) == (c < HEAD_D), 1.0, 0.0).astype(BF16)


def _head_sum(x, ones_bd):
    return jnp.dot(x.astype(BF16), ones_bd, preferred_element_type=F32)


def _rwkv_pre_body(p_ref, prev_ref, mu_ref, w0_ref, w2_ref, a0_ref, a2_ref, g2_ref, kk_ref, ka_ref,
                   rk_ref, *rest, has_vres):
    if has_vres:
        vf_ref, v0_ref, v1_ref, v2_ref = rest[:4]
        outs = rest[4:]
    else:
        outs = rest
    r_o, lw_o, k_o, v_o, kn_o, b_o, g_o, bonus_o = outs
    tr = p_ref.shape[0]
    row0 = lax.broadcasted_iota(jnp.int32, (tr, LANES), 0) == 0
    ones_bd = _pair_ones()

    def mixed(c0, width):
        outs_ = []
        for t in range(width // LANES):
            sl = slice(c0 + t * LANES, c0 + (t + 1) * LANES)
            p = p_ref[:, sl]
            pp = jnp.where(row0, prev_ref[:, sl], pltpu.roll(p, 1, 0))
            outs_.append(p + (pp - p) * mu_ref[:, sl])
        return outs_

    lora = mixed(LORA_OFF, LANES)[0]
    gate_in = jnp.concatenate(mixed(GATE_OFF, SHIFT_PAD - GATE_OFF), axis=1)
    w_pre = w0_ref[...] + _bdot(jnp.tanh(lora), w2_ref[...])
    lw_o[...] = -jnp.exp(-jax.nn.softplus(-w_pre) - 0.5)
    a = jax.nn.sigmoid(a0_ref[...] + _bdot(lora, a2_ref[...]))
    g_o[...] = _bdot(jax.nn.sigmoid(gate_in), g2_ref[...])

    r_t = mixed(0, TOK_W)
    k_t = mixed(TOK_W, TOK_W)
    v_t = mixed(2 * TOK_W, TOK_W)
    if has_vres:
        v_all = jnp.concatenate(v_t, axis=1)
        gate = jax.nn.sigmoid(v0_ref[...] + _bdot(_bdot(v_all, v1_ref[...]), v2_ref[...]))
        v_all = v_all + (vf_ref[...] - v_all) * gate
        v_t = [v_all[:, t * LANES:(t + 1) * LANES] for t in range(N_PAIRS)]
    for t in range(N_PAIRS):
        sl = slice(t * LANES, (t + 1) * LANES)
        kraw = k_t[t]
        kk = kraw * kk_ref[:, sl]
        kk = kk / jnp.maximum(jnp.sqrt(_head_sum(kk * kk, ones_bd)), 1e-12)
        a_t = a[:, sl]
        k = kraw * (1.0 + (a_t - 1.0) * ka_ref[:, sl])
        v = v_t[t]
        r_o[:, sl] = r_t[t]
        k_o[:, sl] = k
        v_o[:, sl] = v
        kn_o[:, sl] = kk
        b_o[:, sl] = kk * a_t
        bonus_o[:, sl] = _head_sum(r_t[t] * k * rk_ref[:, sl], ones_bd) * v


def rwkv_pre(proj, prev, pw, v_first, tr):
    m = proj.shape[0]
    has_vres = v_first is not None
    row = lambda w: pl.BlockSpec((1, w), lambda i: (0, 0))
    full = lambda a: pl.BlockSpec(a.shape, lambda i: (0,) * a.ndim)
    tile = pl.BlockSpec((tr, TOK_W), lambda i: (i, 0))
    in_specs = [pl.BlockSpec((tr, A_IN_PAD), lambda i: (i, 0)),
                pl.BlockSpec((None, 1, A_IN_PAD), lambda i: (i, 0, 0)),
                row(SHIFT_PAD), row(TOK_W), full(pw["w2"]), row(TOK_W), full(pw["a2"]), full(pw["g2"]),
                row(TOK_W), row(TOK_W), row(TOK_W)]
    args = [proj, prev, pw["mu"], pw["w0"], pw["w2"], pw["a0"], pw["a2"], pw["g2"], pw["k_k"],
            pw["k_a"], pw["r_k"]]
    if has_vres:
        in_specs += [tile, row(TOK_W), full(pw["v1"]), full(pw["v2"])]
        args += [v_first, pw["v0"], pw["v1"], pw["v2"]]
    return pl.pallas_call(
        functools.partial(_rwkv_pre_body, has_vres=has_vres),
        grid=(m // tr,),
        in_specs=in_specs,
        out_specs=[tile] * 8,
        out_shape=[jax.ShapeDtypeStruct((m, TOK_W), F32)] * 8,
        compiler_params=_cparams("parallel"),
        name="rwkv_pre",
    )(*args)


def _wkv_body(r_ref, lw_ref, k_ref, v_ref, kn_ref, b_ref, g_ref, bonus_ref, s0_ref, lg_ref, lb_ref,
              o_ref, s_out_ref, s_ref, *, chunk):
    c = pl.program_id(1)
    L = chunk
    W2 = 2 * L

    @pl.when(c == 0)
    def _():
        s_ref[...] = s0_ref[...]

    t_i = lax.broadcasted_iota(jnp.int32, (L, W2), 0)
    l_i = lax.broadcasted_iota(jnp.int32, (L, W2), 1)
    j_i = jnp.where(l_i >= L, l_i - L, l_i)
    strict = t_i > j_i
    incl = t_i >= j_i
    eye2 = jnp.where(t_i == j_i, 1.0, 0.0)
    levels = []
    s = 1
    while s < L:
        sh = s.bit_length() - 1
        levels.append(((t_i >> (sh + 1)) == (j_i >> (sh + 1)))
                      & (((t_i >> sh) & 1) == 1) & (((j_i >> sh) & 1) == 0))
        s *= 2
    first_t = l_i < L
    lane = lax.broadcasted_iota(jnp.int32, (L, LANES), 1)
    low = lane < HEAD_D
    rr = lax.broadcasted_iota(jnp.int32, (LANES, LANES), 0)
    cc = lax.broadcasted_iota(jnp.int32, (LANES, LANES), 1)
    bd = (rr < HEAD_D) == (cc < HEAD_D)
    ones_bd = jnp.where(bd, 1.0, 0.0).astype(BF16)
    tri = jnp.where(lax.broadcasted_iota(jnp.int32, (L, L), 0)
                    >= lax.broadcasted_iota(jnp.int32, (L, L), 1), 1.0, 0.0)

    def stack_t(x):
        return jnp.concatenate([jnp.where(first_t, x, 0.0), jnp.where(first_t, 0.0, x)],
                               axis=0).astype(BF16)

    def stack_d(x):
        return jnp.concatenate([jnp.where(low, x, 0.0), jnp.where(low, 0.0, x)], axis=0).astype(BF16)

    def tdot(m, stacked):
        return jnp.dot(m.astype(BF16), stacked, preferred_element_type=F32)

    cum_all = jnp.dot(tri, lw_ref[...], precision=lax.Precision.HIGHEST, preferred_element_type=F32)

    pairs = range(N_PAIRS)
    sls = [slice(p * LANES, (p + 1) * LANES) for p in pairs]
    a_t, r_t, sv, bh, kh, dec, cb, ck = [], [], [], [], [], [], [], []
    for p in pairs:
        sl = sls[p]
        lw = lw_ref[:, sl]
        cum = cum_all[:, sl]
        cum_l = cum[L - 1:L, :]
        p_inv = jnp.exp(-cum)
        p_end = jnp.exp(cum_l - cum)
        b = b_ref[:, sl]
        k = k_ref[:, sl]
        a_t.append(-kn_ref[:, sl] * jnp.exp(cum - lw))
        r_t.append(r_ref[:, sl] * jnp.exp(cum))
        sv.append(stack_d(v_ref[:, sl]))
        bh.append((b * p_end).astype(BF16))
        kh.append((k * p_end).astype(BF16))
        dec.append(jnp.exp(cum_l))
        x = jnp.concatenate([a_t[p], r_t[p]], axis=0).astype(BF16)
        cb.append(lax.dot_general(x, stack_d(b * p_inv), _NT, preferred_element_type=F32))
        ck.append(lax.dot_general(x, stack_d(k * p_inv), _NT, preferred_element_type=F32))
    a_ab = [jnp.where(strict, cb[p][:L], 0.0) for p in pairs]
    a_rb = [jnp.where(incl, cb[p][L:], 0.0).astype(BF16) for p in pairs]
    a_k = [jnp.concatenate([jnp.where(strict, ck[p][:L], 0.0), jnp.where(incl, ck[p][L:], 0.0)],
                           axis=0).astype(BF16) for p in pairs]
    kv = [jnp.dot(a_k[p], sv[p], preferred_element_type=F32) for p in pairs]
    t_m = [eye2 + jnp.where(levels[0], a_ab[p], 0.0) for p in pairs]
    for msk in levels[1:]:
        ta = [tdot(t_m[p], stack_t(jnp.where(msk, a_ab[p], 0.0))) for p in pairs]
        t_m = [t_m[p] + tdot(ta[p], stack_t(t_m[p])) for p in pairs]
    t_b = [t_m[p].astype(BF16) for p in pairs]
    w12 = [jnp.dot(t_b[p], jnp.concatenate([stack_d(a_t[p]), stack_d(kv[p][:L])], axis=1),
                   preferred_element_type=F32) for p in pairs]
    qq = [jnp.dot(a_rb[p], jnp.concatenate([stack_d(w12[p][:, :LANES]), stack_d(w12[p][:, LANES:])],
                                           axis=1), preferred_element_type=F32) for p in pairs]
    mm = [_bdot_tn(w12[p], bh[p]) for p in pairs]
    mk = [_bdot_tn(v_ref[:, sls[p]], kh[p]) for p in pairs]
    y = []
    for p in pairs:
        s_p = s_ref[p]
        q1 = r_t[p] + qq[p][:, :LANES]
        q2 = qq[p][:, LANES:] + kv[p][L:]
        m1 = jnp.where(bd, mm[p][:LANES], 0.0)
        m2 = jnp.where(bd, mm[p][LANES:] + mk[p], 0.0)
        y.append(_bdot_nt(q1, s_p) + q2)
        s_ref[p] = s_p * dec[p] + _bdot(s_p, m1) + m2
    mean = [_head_sum(y[p], ones_bd) * (1.0 / HEAD_D) for p in pairs]
    dv = [y[p] - mean[p] for p in pairs]
    var = [_head_sum(dv[p] * dv[p], ones_bd) * (1.0 / HEAD_D) for p in pairs]
    for p in pairs:
        sl = sls[p]
        yn = dv[p] * lax.rsqrt(var[p] + GN_EPS) * lg_ref[:, sl] + lb_ref[:, sl]
        o_ref[:, sl] = ((yn + bonus_ref[:, sl]) * g_ref[:, sl]).astype(o_ref.dtype)

    @pl.when(c == pl.num_programs(1) - 1)
    def _():
        s_out_ref[...] = s_ref[...]


def wkv(r, lw, k, v, kn, b, g, bonus, s0, lnx_g, lnx_b, batch, seq, chunk):
    nc = seq // chunk
    tile = pl.BlockSpec((chunk, TOK_W), lambda bi, c: (bi * nc + c, 0))
    state = pl.BlockSpec((None, N_PAIRS, LANES, LANES), lambda bi, c: (bi, 0, 0, 0))
    row = pl.BlockSpec((1, TOK_W), lambda bi, c: (0, 0))
    return pl.pallas_call(
        functools.partial(_wkv_body, chunk=chunk),
        grid=(batch, nc),
        in_specs=[tile] * 8 + [state, row, row],
        out_specs=[tile, state],
        out_shape=[jax.ShapeDtypeStruct((batch * seq, TOK_W), BF16),
                   jax.ShapeDtypeStruct((batch, N_PAIRS, LANES, LANES), F32)],
        scratch_shapes=[pltpu.VMEM((N_PAIRS, LANES, LANES), F32)],
        compiler_params=_cparams("parallel", "arbitrary"),
        name="wkv",
    )(r, lw, k, v, kn, b, g, bonus, s0, lnx_g.reshape(1, TOK_W), lnx_b.reshape(1, TOK_W))


def _state_to_pairs(s):
    b = s.shape[0]
    s = s.reshape(b, N_PAIRS, 2, HEAD_D, HEAD_D)
    z = jnp.zeros_like(s[:, :, 0])
    top = jnp.concatenate([s[:, :, 0], z], axis=-1)
    bot = jnp.concatenate([z, s[:, :, 1]], axis=-1)
    return jnp.concatenate([top, bot], axis=-2)


def _pairs_to_state(sp):
    b = sp.shape[0]
    h0 = sp[:, :, :HEAD_D, :HEAD_D]
    h1 = sp[:, :, HEAD_D:, HEAD_D:]
    return jnp.stack([h0, h1], axis=2).reshape(b, N_HEADS, HEAD_D, HEAD_D)


def _dup_heads(w):
    lead = w.shape[:-1]
    w = w.reshape(lead + (SWA_KV, 1, HEAD_D))
    return jnp.broadcast_to(w, lead + (SWA_KV, 2, HEAD_D)).reshape(lead + (SWA_KV * LANES,))


def _undup_heads(x):
    lead = x.shape[:-1]
    return x.reshape(lead + (SWA_KV, 2, HEAD_D))[..., 0, :]


def _pad_cols(a, width):
    return jnp.pad(a, [(0, 0)] * (a.ndim - 1) + [(0, width - a.shape[-1])])


def _sink_col(sinks, lq):
    return jnp.repeat(sinks.astype(F32).reshape(SWA_KV, SWA_G), lq, axis=1)[..., None]


def _trunk(x, mem_k, mem_v, wkv0, shift0, past_k, past_v, P, batch, seq, tr_pre, tr_mem):
    m = batch * seq
    chunk = min(CHUNK, seq)
    new_wkv, new_shift = [], []
    v_first = None
    kvd = None
    for l in range(DEPTH):
        if l < N_A:
            pw = P["a"][l]
            proj = norm_matmul(x, P["ln_mix_pre"], l, pw["w_in"], 0)
            p3 = proj.reshape(batch, seq // tr_pre, tr_pre, A_IN_PAD)
            first = _pad_cols(shift0[l].astype(F32), A_IN_PAD).reshape(batch, 1, A_IN_PAD)
            prev = jnp.concatenate([first, p3[:, :-1, -1, :]], axis=1).reshape(m // tr_pre, 1, A_IN_PAD)
            new_shift.append(p3[:, -1, -1, :A_SHIFT].reshape(batch, 1, A_SHIFT))
            r, lw, k, v, kn, b, g, bonus = rwkv_pre(proj, prev, pw, v_first, tr_pre)
            if l == 0:
                v_first = v
            tok, s_new = wkv(r, lw, k, v, kn, b, g, bonus, _state_to_pairs(wkv0[l].astype(F32)),
                             pw["lnx_g"], pw["lnx_b"], batch, seq, chunk)
            new_wkv.append(_pairs_to_state(s_new))
            q_col = A_QMEM_OFF // MEM_W
            w_out, wi = P["a_w_out"], l
        else:
            i = l - N_A
            if kvd is None:
                kvd = norm_matmul(x, P["kv_norm_g"], 0, P["w_kvd"], 0)
            proj = norm_matmul(x, P["ln_mix_pre"], l, P["b_w_in"], i)
            if past_k is None:
                tok = swa_prompt(proj, kvd, _sink_col(P["b_sinks"][i], CHUNK), batch, seq)
            else:
                kd_w = SWA_KV * LANES
                k_all = jnp.concatenate([_dup_heads(past_k.reshape(batch, -1, SWA_KV * HEAD_D)),
                                         kvd[:, :kd_w].reshape(batch, seq, kd_w)], axis=1)
                v_all = jnp.concatenate([_dup_heads(past_v.reshape(batch, -1, SWA_KV * HEAD_D)),
                                         kvd[:, kd_w:].reshape(batch, seq, kd_w)], axis=1)
                tok = swa_sample(proj, k_all, v_all, _sink_col(P["b_sinks"][i], seq), batch, seq)
            q_col = TOK_W // MEM_W
            w_out, wi = P["b_w_out"], i
        mem_o = mem_attn(proj, q_col, mem_k, mem_v, 1, l, seq, tr_mem)
        x = mix_out(tok, mem_o, w_out, wi, P["ln_mix_post"], l, x)
        x = mlp(x, P["ln_mlp_pre"], P["w_up"], P["w_down"], P["ln_mlp_post"], l)
    kd_w = SWA_KV * LANES
    k_new = _undup_heads(kvd[:, :kd_w]).reshape(batch, seq, SWA_KV, HEAD_D)
    v_new = _undup_heads(kvd[:, kd_w:]).reshape(batch, seq, SWA_KV, HEAD_D)
    return x, jnp.stack(new_wkv), jnp.stack(new_shift), k_new, v_new


def _prep_params(ln_mix_pre, ln_mix_post, ln_mlp_pre, ln_mlp_post, mem_norm_g, w_mem_kv, w_up, w_down,
                 a_w_in, a_mu, a_w0, a_w2, a_a0, a_a2, a_g2, a_k_k, a_k_a, a_r_k, a_lnx_g, a_lnx_b,
                 a_v0, a_v1, a_v2, a_w_out, kv_norm_g, w_kv, b_w_in, b_sinks, b_w_out):
    d = D_MODEL
    zrow = lambda n: jnp.zeros((n, TOK_W), F32)
    a_params = []
    for i in range(N_A):
        w_in = a_w_in[i]
        w_in = jnp.concatenate([w_in[:, :A_SHIFT], jnp.zeros((d, A_QMEM_OFF - A_SHIFT), F32),
                                w_in[:, A_SHIFT:]], axis=1).astype(BF16)
        pw = {
            "w_in": w_in[None],
            "mu": _pad_cols(a_mu[i].reshape(1, A_SHIFT), SHIFT_PAD),
            "w0": a_w0[i].reshape(1, TOK_W), "a0": a_a0[i].reshape(1, TOK_W),
            "w2": jnp.concatenate([a_w2[i], zrow(A_LORA)], axis=0).astype(BF16),
            "a2": jnp.concatenate([zrow(W_LORA), a_a2[i]], axis=0).astype(BF16),
            "g2": jnp.concatenate([a_g2[i], zrow(SHIFT_PAD - A_SHIFT)], axis=0).astype(BF16),
            "k_k": a_k_k[i].reshape(1, TOK_W), "k_a": a_k_a[i].reshape(1, TOK_W),
            "r_k": a_r_k[i].reshape(1, TOK_W),
            "lnx_g": a_lnx_g[i], "lnx_b": a_lnx_b[i],
        }
        if i > 0:
            pw["v0"] = a_v0[i - 1].reshape(1, TOK_W)
            pw["v1"] = a_v1[i - 1].astype(BF16)
            pw["v2"] = a_v2[i - 1].astype(BF16)
        a_params.append(pw)
    kw = SWA_KV * HEAD_D
    gain = lambda g: g.reshape(-1, 1, d)
    P = {
        "a": a_params,
        "ln_mix_pre": gain(ln_mix_pre), "ln_mix_post": gain(ln_mix_post), "ln_mlp_pre": gain(ln_mlp_pre),
        "ln_mlp_post": gain(ln_mlp_post), "mem_norm_g": gain(mem_norm_g), "kv_norm_g": gain(kv_norm_g),
        "w_up": w_up, "w_down": w_down, "w_mem_kv": w_mem_kv.astype(BF16),
        "w_kvd": jnp.concatenate([_dup_heads(w_kv[:, :kw]), _dup_heads(w_kv[:, kw:])],
                                 axis=1).astype(BF16)[None],
        "a_w_out": a_w_out.astype(BF16),
        "b_w_in": b_w_in.astype(BF16), "b_sinks": b_sinks, "b_w_out": b_w_out.astype(BF16),
    }
    return P


def kernel(x_prompt, x_sample, mem_prompt, state_wkv, state_shift, cache_win_k, cache_win_v, cache_mem_k, cache_mem_v, ln_mix_pre, ln_mix_post, ln_mlp_pre, ln_mlp_post, mem_norm_g, w_mem_kv, w_up, w_down, a_w_in, a_mu, a_w0, a_w2, a_a0, a_a2, a_g2, a_k_k, a_k_a, a_r_k, a_lnx_g, a_lnx_b, a_v0, a_v1, a_v2, a_w_out, kv_norm_g, w_kv, b_w_in, b_sinks, b_w_out):
    bp, sp, d = x_prompt.shape
    bs, ss, _ = x_sample.shape
    P = _prep_params(ln_mix_pre, ln_mix_post, ln_mlp_pre, ln_mlp_post, mem_norm_g, w_mem_kv, w_up, w_down,
                     a_w_in, a_mu, a_w0, a_w2, a_a0, a_a2, a_g2, a_k_k, a_k_a, a_r_k, a_lnx_g, a_lnx_b,
                     a_v0, a_v1, a_v2, a_w_out, kv_norm_g, w_kv, b_w_in, b_sinks, b_w_out)

    mkv = norm_matmul_layers(mem_prompt.reshape(bp * N_MEM, d), P["mem_norm_g"], P["w_mem_kv"])
    wkv0 = jnp.zeros((N_A, bp, N_HEADS, HEAD_D, HEAD_D), F32)
    shift0 = jnp.zeros((N_A, bp, 1, A_SHIFT), F32)
    y_p, wkv_p, shift_p, k_p, v_p = _trunk(
        x_prompt.reshape(bp * sp, d), mkv, mkv, wkv0, shift0, None, None, P, bp, sp,
        tr_pre=256, tr_mem=512)

    y_s, wkv_s, shift_s, k_s, v_s = _trunk(
        x_sample.reshape(bs * ss, d), cache_mem_k, cache_mem_v, state_wkv, state_shift, cache_win_k,
        cache_win_v, P, bs, ss, tr_pre=ss, tr_mem=ss)
    past = cache_win_k.shape[1]
    win_k_s = jnp.concatenate([cache_win_k, k_s], axis=1)[:, -past:]
    win_v_s = jnp.concatenate([cache_win_v, v_s], axis=1)[:, -past:]

    return (y_p.reshape(bp, sp, d), y_s.reshape(bs, ss, d), wkv_p, shift_p,
            k_p[:, -WINDOW:], v_p[:, -WINDOW:],
            mkv[:, :, :MEM_W].reshape(DEPTH, bp, N_MEM, MEM_HEADS, MEM_HD),
            mkv[:, :, MEM_W:].reshape(DEPTH, bp, N_MEM, MEM_HEADS, MEM_HD),
            wkv_s, shift_s, win_k_s, win_v_s)
```

```python
import functools

import jax
import jax.numpy as jnp
from jax import lax
from jax.experimental import pallas as pl
from jax.experimental.pallas import tpu as pltpu

F32 = jnp.float32
BF16 = jnp.bfloat16

D_MODEL = 2048
DEPTH = 4
N_A = 2
CHUNK = 64
N_MEM = 256
MEM_HEADS = 4
MEM_HD = 128
MEM_W = MEM_HEADS * MEM_HD
TOK_W = D_MODEL - MEM_W
HEAD_D = 64
N_HEADS = TOK_W // HEAD_D
N_PAIRS = N_HEADS // 2
W_LORA = 64
A_LORA = 64
G_LORA = 224
A_SHIFT = 3 * TOK_W + W_LORA + A_LORA + G_LORA
LORA_OFF = 3 * TOK_W
GATE_OFF = LORA_OFF + 128
SHIFT_PAD = 4992
A_QMEM_OFF = 5120
A_IN_PAD = A_QMEM_OFF + MEM_W
SWA_KV = 4
SWA_G = N_HEADS // SWA_KV
WINDOW = 128
D_FF = 4 * D_MODEL
RMS_EPS = 1e-6
GN_EPS = 64e-5
NEG = -1e30
LANES = 128
VMEM_LIMIT = 56 * 1024 * 1024

_NT = (((1,), (1,)), ((), ()))
_TN = (((0,), (0,)), ((), ()))


def _cparams(*sem):
    return pltpu.CompilerParams(dimension_semantics=sem, vmem_limit_bytes=VMEM_LIMIT)


def _bdot(a, b):
    return jnp.dot(a.astype(BF16), b.astype(BF16), preferred_element_type=F32)


def _bdot_nt(a, b):
    return lax.dot_general(a.astype(BF16), b.astype(BF16), _NT, preferred_element_type=F32)


def _bdot_tn(a, b):
    return lax.dot_general(a.astype(BF16), b.astype(BF16), _TN, preferred_element_type=F32)


def _rms(x, g):
    return x * lax.rsqrt(jnp.mean(x * x, axis=-1, keepdims=True) + RMS_EPS) * g


def _norm_matmul_body(x_ref, g_ref, w_ref, o_ref, xn_ref):
    @pl.when(pl.program_id(1) == 0)
    def _():
        xn_ref[...] = _rms(x_ref[...], g_ref[...]).astype(BF16)

    o_ref[...] = jnp.dot(xn_ref[...], w_ref[...], preferred_element_type=F32).astype(o_ref.dtype)


def _proj_tiles(m, n):
    tm = min(m, 1024)
    for tn in (1408, 1024, 512):
        if n % tn == 0:
            return tm, tn
    raise ValueError(f"unsupported projection width {n}")


def norm_matmul(x, g, gl, w, wl):
    m, d = x.shape
    n = w.shape[2]
    tm, tn = _proj_tiles(m, n)
    return pl.pallas_call(
        _norm_matmul_body,
        grid=(m // tm, n // tn),
        in_specs=[pl.BlockSpec((tm, d), lambda i, j: (i, 0)),
                  pl.BlockSpec((None, 1, d), lambda i, j: (gl, 0, 0)),
                  pl.BlockSpec((None, d, tn), lambda i, j: (wl, 0, j))],
        out_specs=pl.BlockSpec((tm, tn), lambda i, j: (i, j)),
        out_shape=jax.ShapeDtypeStruct((m, n), F32),
        scratch_shapes=[pltpu.VMEM((tm, d), BF16)],
        compiler_params=_cparams("parallel", "arbitrary"),
        name="norm_matmul",
    )(x, g, w)


def norm_matmul_layers(x, g, w):
    m, d = x.shape
    nl, _, n = w.shape
    _, tn = _proj_tiles(m, n)
    return pl.pallas_call(
        _norm_matmul_body,
        grid=(nl, n // tn),
        in_specs=[pl.BlockSpec((m, d), lambda l, j: (0, 0)),
                  pl.BlockSpec((None, 1, d), lambda l, j: (l, 0, 0)),
                  pl.BlockSpec((None, d, tn), lambda l, j: (l, 0, j))],
        out_specs=pl.BlockSpec((None, m, tn), lambda l, j: (l, 0, j)),
        out_shape=jax.ShapeDtypeStruct((nl, m, n), F32),
        scratch_shapes=[pltpu.VMEM((m, d), BF16)],
        compiler_params=_cparams("parallel", "arbitrary"),
        name="norm_matmul_layers",
    )(x, g, w)


def _mix_out_body(tok_ref, mem_ref, wt_ref, wm_ref, g_ref, x_ref, o_ref):
    y = jnp.dot(tok_ref[...], wt_ref[...], preferred_element_type=F32)
    y = y + jnp.dot(mem_ref[...], wm_ref[...], preferred_element_type=F32)
    o_ref[...] = x_ref[...] + _rms(y, g_ref[...])


def mix_out(tok, mem, w_out, wl, g, gl, x):
    m, d = x.shape
    tm = min(m, 512)
    return pl.pallas_call(
        _mix_out_body,
        grid=(m // tm,),
        in_specs=[pl.BlockSpec((tm, TOK_W), lambda i: (i, 0)),
                  pl.BlockSpec((tm, MEM_W), lambda i: (i, 0)),
                  pl.BlockSpec((None, TOK_W, d), lambda i: (wl, 0, 0)),
                  pl.BlockSpec((None, MEM_W, d), lambda i: (wl, TOK_W // MEM_W, 0)),
                  pl.BlockSpec((None, 1, d), lambda i: (gl, 0, 0)),
                  pl.BlockSpec((tm, d), lambda i: (i, 0))],
        out_specs=pl.BlockSpec((tm, d), lambda i: (i, 0)),
        out_shape=jax.ShapeDtypeStruct((m, d), F32),
        compiler_params=_cparams("parallel"),
        name="mix_out",
    )(tok, mem, w_out, w_out, g, x)


def _mlp_body(x_ref, g1_ref, wu_ref, wd_ref, g2_ref, o_ref, xn_ref):
    f = pl.program_id(1)

    @pl.when(f == 0)
    def _():
        xn_ref[...] = _rms(x_ref[...], g1_ref[...]).astype(BF16)
        o_ref[...] = jnp.zeros_like(o_ref)

    h = jnp.maximum(jnp.dot(xn_ref[...], wu_ref[...].astype(BF16), preferred_element_type=F32), 0.0)
    o_ref[...] += jnp.dot((h * h).astype(BF16), wd_ref[...].astype(BF16), preferred_element_type=F32)

    @pl.when(f == pl.num_programs(1) - 1)
    def _():
        o_ref[...] = x_ref[...] + _rms(o_ref[...], g2_ref[...])


def mlp(x, g1, w_up, w_down, g2, l):
    m, d = x.shape
    ff = w_up.shape[2]
    tm, tf = min(m, 1024), 512
    once = pl.Buffered(1)
    return pl.pallas_call(
        _mlp_body,
        grid=(m // tm, ff // tf),
        in_specs=[pl.BlockSpec((tm, d), lambda i, f: (i, 0), pipeline_mode=once),
                  pl.BlockSpec((None, 1, d), lambda i, f: (l, 0, 0)),
                  pl.BlockSpec((None, d, tf), lambda i, f: (l, 0, f)),
                  pl.BlockSpec((None, tf, d), lambda i, f: (l, f, 0)),
                  pl.BlockSpec((None, 1, d), lambda i, f: (l, 0, 0))],
        out_specs=pl.BlockSpec((tm, d), lambda i, f: (i, 0)),
        out_shape=jax.ShapeDtypeStruct((m, d), F32),
        scratch_shapes=[pltpu.VMEM((tm, d), BF16)],
        compiler_params=_cparams("parallel", "arbitrary"),
        name="mlp",
    )(x, g1, w_up, w_down, g2)


def _mem_attn_body(q_ref, k_ref, v_ref, o_ref):
    scale = MEM_HD ** -0.5
    heads = range(MEM_HEADS)
    sls = [slice(h * MEM_HD, (h + 1) * MEM_HD) for h in heads]
    head = (lambda ref, h: ref[:, h, :]) if len(k_ref.shape) == 3 else (lambda ref, h: ref[:, sls[h]])
    s = [_bdot_nt(q_ref[:, sls[h]], head(k_ref, h)) * scale for h in heads]
    e = [jnp.exp(s[h] - jnp.max(s[h], axis=-1, keepdims=True)) for h in heads]
    p = [e[h] / jnp.sum(e[h], axis=-1, keepdims=True) for h in heads]
    o = [_bdot(p[h], head(v_ref, h)) for h in heads]
    for h in heads:
        o_ref[:, sls[h]] = o[h].astype(o_ref.dtype)


def mem_attn(proj, q_col_block, mem_k, mem_v, v_col_block, l, rows_per_seq, tr):
    m = proj.shape[0]
    tiles_per_seq = rows_per_seq // tr
    if mem_k.ndim == 5:
        blk = (None, None, N_MEM, MEM_HEADS, MEM_HD)
        k_spec = pl.BlockSpec(blk, lambda i: (l, i // tiles_per_seq, 0, 0, 0))
        v_spec = k_spec
    else:
        blk = (None, N_MEM, MEM_W)
        k_spec = pl.BlockSpec(blk, lambda i: (l, i // tiles_per_seq, 0))
        v_spec = pl.BlockSpec(blk, lambda i: (l, i // tiles_per_seq, v_col_block))
    return pl.pallas_call(
        _mem_attn_body,
        grid=(m // tr,),
        in_specs=[pl.BlockSpec((tr, MEM_W), lambda i: (i, q_col_block)), k_spec, v_spec],
        out_specs=pl.BlockSpec((tr, MEM_W), lambda i: (i, 0)),
        out_shape=jax.ShapeDtypeStruct((m, MEM_W), BF16),
        compiler_params=_cparams("parallel"),
        name="mem_attn",
    )(proj, mem_k, mem_v)


def _swa_core(q_ref, k_tiles, v_tiles, valid, sink_ref, o_ref, nb, lq):
    lane = lax.broadcasted_iota(jnp.int32, (lq, LANES), 1)
    low = lane < HEAD_D
    probs = [(b, j) for b in range(nb) for j in range(SWA_KV)]
    s = []
    for b, j in probs:
        xs = []
        for i in range(SWA_G // 2):
            c0 = (j * SWA_G + 2 * i) * HEAD_D
            pair = q_ref[b, :, c0:c0 + LANES] * (HEAD_D ** -0.5)
            xs.append(jnp.where(low, pair, 0.0).astype(BF16))
            xs.append(jnp.where(low, 0.0, pair).astype(BF16))
        x = jnp.concatenate(xs, axis=0)
        kj = jnp.concatenate([t.astype(BF16) for t in k_tiles(b, j)], axis=0)
        sj = lax.dot_general(x, kj, _NT, preferred_element_type=F32)
        s.append(sj if valid is None else jnp.where(valid, sj, NEG))
    p = []
    for n, (b, j) in enumerate(probs):
        sink = sink_ref[j]
        mx = jnp.maximum(jnp.max(s[n], axis=-1, keepdims=True), sink)
        e = jnp.exp(s[n] - mx)
        p.append((e / (jnp.sum(e, axis=-1, keepdims=True) + jnp.exp(sink - mx))).astype(BF16))
    o = [jnp.dot(p[n], jnp.concatenate([t.astype(BF16) for t in v_tiles(b, j)], axis=0),
                 preferred_element_type=F32) for n, (b, j) in enumerate(probs)]
    for n, (b, j) in enumerate(probs):
        for i in range(SWA_G // 2):
            c0 = (j * SWA_G + 2 * i) * HEAD_D
            top = o[n][(2 * i) * lq:(2 * i + 1) * lq]
            bot = o[n][(2 * i + 1) * lq:(2 * i + 2) * lq]
            o_ref[b, :, c0:c0 + LANES] = jnp.where(low, top, bot).astype(o_ref.dtype)


def _swa_prompt_body(q_ref, k0, k1, k2, v0, v1, v2, sink_ref, o_ref):
    c = pl.program_id(1)
    col = lax.broadcasted_iota(jnp.int32, (SWA_G * CHUNK, 3 * CHUNK), 1)
    valid = col >= jnp.maximum(2 - c, 0) * CHUNK
    _swa_core(q_ref,
              lambda b, j: [r[b, :, j * LANES:(j + 1) * LANES] for r in (k0, k1, k2)],
              lambda b, j: [r[b, :, j * LANES:(j + 1) * LANES] for r in (v0, v1, v2)],
              valid, sink_ref, o_ref, q_ref.shape[0], CHUNK)


def _batch_block(batch, want):
    return want if batch % want == 0 else 1


def swa_prompt(proj, kvd, sink_col, batch, seq):
    nc = seq // CHUNK
    kd_w = SWA_KV * LANES
    nb = _batch_block(batch, 2)

    def kv_spec(back, col):
        return pl.BlockSpec((nb, CHUNK, kd_w), lambda b, c: (b, jnp.maximum(c - back, 0), col))

    kv3 = kvd.reshape(batch, seq, 2 * kd_w)
    out = pl.pallas_call(
        _swa_prompt_body,
        grid=(batch // nb, nc),
        in_specs=[pl.BlockSpec((nb, CHUNK, TOK_W), lambda b, c: (b, c, 0)),
                  kv_spec(2, 0), kv_spec(1, 0), kv_spec(0, 0),
                  kv_spec(2, 1), kv_spec(1, 1), kv_spec(0, 1),
                  pl.BlockSpec((SWA_KV, SWA_G * CHUNK, 1), lambda b, c: (0, 0, 0))],
        out_specs=pl.BlockSpec((nb, CHUNK, TOK_W), lambda b, c: (b, c, 0)),
        out_shape=jax.ShapeDtypeStruct((batch, seq, TOK_W), BF16),
        compiler_params=_cparams("parallel", "arbitrary"),
        name="swa_prompt",
    )(proj.reshape(batch, seq, -1), kv3, kv3, kv3, kv3, kv3, kv3, sink_col)
    return out.reshape(batch * seq, TOK_W)


def _swa_sample_body(q_ref, k_ref, v_ref, sink_ref, o_ref):
    nb, lq, _ = q_ref.shape
    _swa_core(q_ref,
              lambda b, j: [k_ref[b, :, j * LANES:(j + 1) * LANES]],
              lambda b, j: [v_ref[b, :, j * LANES:(j + 1) * LANES]],
              None, sink_ref, o_ref, nb, lq)


def swa_sample(proj, k_all, v_all, sink_col, batch, lq):
    keys = k_all.shape[1]
    kd_w = SWA_KV * LANES
    nb = _batch_block(batch, 4)
    out = pl.pallas_call(
        _swa_sample_body,
        grid=(batch // nb,),
        in_specs=[pl.BlockSpec((nb, lq, TOK_W), lambda b: (b, 0, 0)),
                  pl.BlockSpec((nb, keys, kd_w), lambda b: (b, 0, 0)),
                  pl.BlockSpec((nb, keys, kd_w), lambda b: (b, 0, 0)),
                  pl.BlockSpec((SWA_KV, SWA_G * lq, 1), lambda b: (0, 0, 0))],
        out_specs=pl.BlockSpec((nb, lq, TOK_W), lambda b: (b, 0, 0)),
        out_shape=jax.ShapeDtypeStruct((batch, lq, TOK_W), BF16),
        compiler_params=_cparams("parallel"),
        name="swa_sample",
    )(proj.reshape(batch, lq, -1), k_all, v_all, sink_col)
    return out.reshape(batch * lq, TOK_W)


def _pair_ones():
    r = lax.broadcasted_iota(jnp.int32, (LANES, LANES), 0)
    c = lax.broadcasted_iota(jnp.int32, (LANES, LANES), 1)
    return jnp.where((r < HEAD_D) == (c < HEAD_D), 1.0, 0.0).astype(BF16)


def _head_sum(x, ones_bd):
    return jnp.dot(x.astype(BF16), ones_bd, preferred_element_type=F32)


def _pre_stage(p_ref, prev_ref, wr, vf_ref, outs):
    has_vres = vf_ref is not None
    mu_ref, w0_ref, w2_ref, a0_ref, a2_ref, g2_ref = (wr[n] for n in ("mu", "w0", "w2", "a0", "a2", "g2"))
    kk_ref, ka_ref, rk_ref = wr["k_k"], wr["k_a"], wr["r_k"]
    if has_vres:
        v0_ref, v1_ref, v2_ref = wr["v0"], wr["v1"], wr["v2"]
    r_o, lw_o, k_o, v_o, kn_o, b_o, g_o, bonus_o = outs
    tr = p_ref.shape[0]
    row0 = lax.broadcasted_iota(jnp.int32, (tr, LANES), 0) == 0
    ones_bd = _pair_ones()

    def mixed(c0, width):
        outs_ = []
        for t in range(width // LANES):
            sl = slice(c0 + t * LANES, c0 + (t + 1) * LANES)
            p = p_ref[:, sl]
            pp = jnp.where(row0, prev_ref[:, sl], pltpu.roll(p, 1, 0))
            outs_.append(p + (pp - p) * mu_ref[:, sl])
        return outs_

    lora = mixed(LORA_OFF, LANES)[0]
    gate_in = jnp.concatenate(mixed(GATE_OFF, SHIFT_PAD - GATE_OFF), axis=1)
    w_pre = w0_ref[...] + _bdot(jnp.tanh(lora), w2_ref[...])
    lw_o[...] = -jnp.exp(-jax.nn.softplus(-w_pre) - 0.5)
    a = jax.nn.sigmoid(a0_ref[...] + _bdot(lora, a2_ref[...]))
    g_o[...] = _bdot(jax.nn.sigmoid(gate_in), g2_ref[...])

    r_t = mixed(0, TOK_W)
    k_t = mixed(TOK_W, TOK_W)
    v_t = mixed(2 * TOK_W, TOK_W)
    if has_vres:
        v_all = jnp.concatenate(v_t, axis=1)
        gate = jax.nn.sigmoid(v0_ref[...] + _bdot(_bdot(v_all, v1_ref[...]), v2_ref[...]))
        v_all = v_all + (vf_ref[...] - v_all) * gate
        v_t = [v_all[:, t * LANES:(t + 1) * LANES] for t in range(N_PAIRS)]
    for t in range(N_PAIRS):
        sl = slice(t * LANES, (t + 1) * LANES)
        kraw = k_t[t]
        kk = kraw * kk_ref[:, sl]
        kk = kk / jnp.maximum(jnp.sqrt(_head_sum(kk * kk, ones_bd)), 1e-12)
        a_t = a[:, sl]
        k = kraw * (1.0 + (a_t - 1.0) * ka_ref[:, sl])
        v = v_t[t]
        r_o[:, sl] = r_t[t]
        k_o[:, sl] = k
        v_o[:, sl] = v
        kn_o[:, sl] = kk
        b_o[:, sl] = kk * a_t
        bonus_o[:, sl] = _head_sum(r_t[t] * k * rk_ref[:, sl], ones_bd) * v


def _wkv_stage(mid, lg_ref, lb_ref, o_ref, s_ref, L):
    r_ref, lw_ref, k_ref, v_ref, kn_ref, b_ref, g_ref, bonus_ref = mid
    nch = r_ref.shape[0] // L
    W2 = 2 * L

    t_i = lax.broadcasted_iota(jnp.int32, (L, W2), 0)
    l_i = lax.broadcasted_iota(jnp.int32, (L, W2), 1)
    j_i = jnp.where(l_i >= L, l_i - L, l_i)
    strict = t_i > j_i
    incl = t_i >= j_i
    eye2 = jnp.where(t_i == j_i, 1.0, 0.0)
    levels = []
    s = 1
    while s < L:
        sh = s.bit_length() - 1
        levels.append(((t_i >> (sh + 1)) == (j_i >> (sh + 1)))
                      & (((t_i >> sh) & 1) == 1) & (((j_i >> sh) & 1) == 0))
        s *= 2
    first_t = l_i < L
    lane = lax.broadcasted_iota(jnp.int32, (L, LANES), 1)
    low = lane < HEAD_D
    rr = lax.broadcasted_iota(jnp.int32, (LANES, LANES), 0)
    cc = lax.broadcasted_iota(jnp.int32, (LANES, LANES), 1)
    bd = (rr < HEAD_D) == (cc < HEAD_D)
    ones_bd = jnp.where(bd, 1.0, 0.0).astype(BF16)
    tri = jnp.where(lax.broadcasted_iota(jnp.int32, (L, L), 0)
                    >= lax.broadcasted_iota(jnp.int32, (L, L), 1), 1.0, 0.0)

    def stack_t(x):
        return jnp.concatenate([jnp.where(first_t, x, 0.0), jnp.where(first_t, 0.0, x)],
                               axis=0).astype(BF16)

    def stack_d(x):
        return jnp.concatenate([jnp.where(low, x, 0.0), jnp.where(low, 0.0, x)], axis=0).astype(BF16)

    def tdot(m, stacked):
        return jnp.dot(m.astype(BF16), stacked, preferred_element_type=F32)

    cum_all = [jnp.dot(tri, lw_ref[ch * L:(ch + 1) * L, :], precision=lax.Precision.HIGHEST,
                       preferred_element_type=F32) for ch in range(nch)]

    probs = [(ch, hp) for ch in range(nch) for hp in range(N_PAIRS)]
    pairs = range(len(probs))
    rows = [slice(ch * L, (ch + 1) * L) for ch, _ in probs]
    sls = [slice(hp * LANES, (hp + 1) * LANES) for _, hp in probs]
    a_t, r_t, sv, bh, kh, dec, cb, ck = [], [], [], [], [], [], [], []
    for p in pairs:
        rw, sl = rows[p], sls[p]
        lw = lw_ref[rw, sl]
        cum = cum_all[probs[p][0]][:, sl]
        cum_l = cum[L - 1:L, :]
        p_inv = jnp.exp(-cum)
        p_end = jnp.exp(cum_l - cum)
        b = b_ref[rw, sl]
        k = k_ref[rw, sl]
        a_t.append(-kn_ref[rw, sl] * jnp.exp(cum - lw))
        r_t.append(r_ref[rw, sl] * jnp.exp(cum))
        sv.append(stack_d(v_ref[rw, sl]))
        bh.append((b * p_end).astype(BF16))
        kh.append((k * p_end).astype(BF16))
        dec.append(jnp.exp(cum_l))
        x = jnp.concatenate([a_t[p], r_t[p]], axis=0).astype(BF16)
        cb.append(lax.dot_general(x, stack_d(b * p_inv), _NT, preferred_element_type=F32))
        ck.append(lax.dot_general(x, stack_d(k * p_inv), _NT, preferred_element_type=F32))
    a_ab = [jnp.where(strict, cb[p][:L], 0.0) for p in pairs]
    a_rb = [jnp.where(incl, cb[p][L:], 0.0).astype(BF16) for p in pairs]
    a_k = [jnp.concatenate([jnp.where(strict, ck[p][:L], 0.0), jnp.where(incl, ck[p][L:], 0.0)],
                           axis=0).astype(BF16) for p in pairs]
    kv = [jnp.dot(a_k[p], sv[p], preferred_element_type=F32) for p in pairs]
    t_m = [eye2 + jnp.where(levels[0], a_ab[p], 0.0) for p in pairs]
    for msk in levels[1:]:
        ta = [tdot(t_m[p], stack_t(jnp.where(msk, a_ab[p], 0.0))) for p in pairs]
        t_m = [t_m[p] + tdot(ta[p], stack_t(t_m[p])) for p in pairs]
    t_b = [t_m[p].astype(BF16) for p in pairs]
    w12 = [jnp.dot(t_b[p], jnp.concatenate([stack_d(a_t[p]), stack_d(kv[p][:L])], axis=1),
                   preferred_element_type=F32) for p in pairs]
    qq = [jnp.dot(a_rb[p], jnp.concatenate([stack_d(w12[p][:, :LANES]), stack_d(w12[p][:, LANES:])],
                                           axis=1), preferred_element_type=F32) for p in pairs]
    mm = [_bdot_tn(w12[p], bh[p]) for p in pairs]
    mk = [_bdot_tn(v_ref[rows[p], sls[p]], kh[p]) for p in pairs]
    y = []
    for p in pairs:
        hp = probs[p][1]
        s_p = s_ref[hp]
        q1 = r_t[p] + qq[p][:, :LANES]
        q2 = qq[p][:, LANES:] + kv[p][L:]
        m1 = jnp.where(bd, mm[p][:LANES], 0.0)
        m2 = jnp.where(bd, mm[p][LANES:] + mk[p], 0.0)
        y.append(_bdot_nt(q1, s_p) + q2)
        s_ref[hp] = s_p * dec[p] + _bdot(s_p, m1) + m2
    mean = [_head_sum(y[p], ones_bd) * (1.0 / HEAD_D) for p in pairs]
    dv = [y[p] - mean[p] for p in pairs]
    var = [_head_sum(dv[p] * dv[p], ones_bd) * (1.0 / HEAD_D) for p in pairs]
    for p in pairs:
        rw, sl = rows[p], sls[p]
        yn = dv[p] * lax.rsqrt(var[p] + GN_EPS) * lg_ref[:, sl] + lb_ref[:, sl]
        o_ref[rw, sl] = ((yn + bonus_ref[rw, sl]) * g_ref[rw, sl]).astype(o_ref.dtype)


_PRE_W = ("mu", "w0", "w2", "a0", "a2", "g2", "k_k", "k_a", "r_k")
_VRES_W = ("v0", "v1", "v2")
_MID = 8


def _rwkv_body(*refs, chunk, has_vres):
    names = _PRE_W + (_VRES_W if has_vres else ())
    it = iter(refs)
    p_ref, shift_ref = next(it), next(it)
    vf_ref = next(it) if has_vres else None
    wr = {n: next(it) for n in names}
    s0_ref, lg_ref, lb_ref = next(it), next(it), next(it)
    o_ref = next(it)
    v_out_ref = None if has_vres else next(it)
    s_out_ref, s_ref, prev_ref = next(it), next(it), next(it)
    mid = [next(it) for _ in range(_MID)]
    c = pl.program_id(1)

    @pl.when(c == 0)
    def _():
        zero = jnp.zeros((HEAD_D, HEAD_D), F32)
        for hp in range(N_PAIRS):
            top = jnp.concatenate([s0_ref[2 * hp], zero], axis=1)
            bot = jnp.concatenate([zero, s0_ref[2 * hp + 1]], axis=1)
            s_ref[hp] = jnp.concatenate([top, bot], axis=0)
        prev_ref[...] = shift_ref[...]

    _pre_stage(p_ref, prev_ref, wr, vf_ref, mid)
    prev_ref[...] = p_ref[p_ref.shape[0] - 1:, :]
    if v_out_ref is not None:
        v_out_ref[...] = mid[3][...]
    _wkv_stage(mid, lg_ref, lb_ref, o_ref, s_ref, chunk)

    @pl.when(c == pl.num_programs(1) - 1)
    def _():
        for hp in range(N_PAIRS):
            s_p = s_ref[hp]
            s_out_ref[2 * hp] = s_p[:HEAD_D, :HEAD_D]
            s_out_ref[2 * hp + 1] = s_p[HEAD_D:, HEAD_D:]


def rwkv_mix(proj, shift0, pw, v_first, s0, batch, seq, chunk, rows):
    steps = seq // rows
    has_vres = v_first is not None
    names = _PRE_W + (_VRES_W if has_vres else ())
    full = lambda a: pl.BlockSpec(a.shape, lambda bi, c: (0,) * a.ndim)
    tile = pl.BlockSpec((rows, TOK_W), lambda bi, c: (bi * steps + c, 0))
    state = pl.BlockSpec((None, N_HEADS, HEAD_D, HEAD_D), lambda bi, c: (bi, 0, 0, 0))
    in_specs = [pl.BlockSpec((rows, SHIFT_PAD), lambda bi, c: (bi * steps + c, 0)),
                pl.BlockSpec((None, 1, SHIFT_PAD), lambda bi, c: (bi, 0, 0))]
    args = [proj, shift0]
    if has_vres:
        in_specs.append(tile)
        args.append(v_first)
    in_specs += [full(pw[n]) for n in names] + [state, full(pw["lnx_g"]), full(pw["lnx_b"])]
    args += [pw[n] for n in names] + [s0, pw["lnx_g"], pw["lnx_b"]]
    out_specs = [tile] + ([] if has_vres else [tile]) + [state]
    out_shape = ([jax.ShapeDtypeStruct((batch * seq, TOK_W), BF16)]
                 + ([] if has_vres else [jax.ShapeDtypeStruct((batch * seq, TOK_W), F32)])
                 + [jax.ShapeDtypeStruct((batch, N_HEADS, HEAD_D, HEAD_D), F32)])
    outs = pl.pallas_call(
        functools.partial(_rwkv_body, chunk=chunk, has_vres=has_vres),
        grid=(batch, steps),
        in_specs=in_specs,
        out_specs=out_specs,
        out_shape=out_shape,
        scratch_shapes=[pltpu.VMEM((N_PAIRS, LANES, LANES), F32), pltpu.VMEM((1, SHIFT_PAD), F32)]
                       + [pltpu.VMEM((rows, TOK_W), F32)] * _MID,
        compiler_params=_cparams("parallel", "arbitrary"),
        name="rwkv_mix",
    )(*args)
    if has_vres:
        return outs[0], None, outs[1]
    return outs


def _dup_heads(w):
    lead = w.shape[:-1]
    w = w.reshape(lead + (SWA_KV, 1, HEAD_D))
    return jnp.broadcast_to(w, lead + (SWA_KV, 2, HEAD_D)).reshape(lead + (SWA_KV * LANES,))


def _undup_heads(x):
    lead = x.shape[:-1]
    return x.reshape(lead + (SWA_KV, 2, HEAD_D))[..., 0, :]


def _pad_cols(a, width):
    return jnp.pad(a, [(0, 0)] * (a.ndim - 1) + [(0, width - a.shape[-1])])


def _sink_col(sinks, lq):
    return jnp.repeat(sinks.astype(F32).reshape(SWA_KV, SWA_G), lq, axis=1)[..., None]


def _trunk(x, mem_k, mem_v, wkv0, shift0, past_k, past_v, P, batch, seq, tr_mem):
    chunk = min(CHUNK, seq)
    rows_mix = min(2 * chunk, seq)
    new_wkv, new_shift = [], []
    v_first = None
    kvd = None
    for l in range(DEPTH):
        if l < N_A:
            pw = P["a"][l]
            proj = norm_matmul(x, P["ln_mix_pre"], l, pw["w_in"], 0)
            new_shift.append(proj.reshape(batch, seq, A_IN_PAD)[:, -1:, :A_SHIFT])
            tok, v, s_new = rwkv_mix(proj, _pad_cols(shift0[l].astype(F32), SHIFT_PAD), pw, v_first,
                                     wkv0[l].astype(F32), batch, seq, chunk, rows_mix)
            if l == 0:
                v_first = v
            new_wkv.append(s_new)
            q_col = A_QMEM_OFF // MEM_W
            w_out, wi = P["a_w_out"], l
        else:
            i = l - N_A
            if kvd is None:
                kvd = norm_matmul(x, P["kv_norm_g"], 0, P["w_kvd"], 0)
            proj = norm_matmul(x, P["ln_mix_pre"], l, P["b_w_in"], i)
            if past_k is None:
                tok = swa_prompt(proj, kvd, _sink_col(P["b_sinks"][i], CHUNK), batch, seq)
            else:
                kd_w = SWA_KV * LANES
                k_all = jnp.concatenate([_dup_heads(past_k.reshape(batch, -1, SWA_KV * HEAD_D)),
                                         kvd[:, :kd_w].reshape(batch, seq, kd_w)], axis=1)
                v_all = jnp.concatenate([_dup_heads(past_v.reshape(batch, -1, SWA_KV * HEAD_D)),
                                         kvd[:, kd_w:].reshape(batch, seq, kd_w)], axis=1)
                tok = swa_sample(proj, k_all, v_all, _sink_col(P["b_sinks"][i], seq), batch, seq)
            q_col = TOK_W // MEM_W
            w_out, wi = P["b_w_out"], i
        mem_o = mem_attn(proj, q_col, mem_k, mem_v, 1, l, seq, tr_mem)
        x = mix_out(tok, mem_o, w_out, wi, P["ln_mix_post"], l, x)
        x = mlp(x, P["ln_mlp_pre"], P["w_up"], P["w_down"], P["ln_mlp_post"], l)
    kd_w = SWA_KV * LANES
    k_new = _undup_heads(kvd[:, :kd_w]).reshape(batch, seq, SWA_KV, HEAD_D)
    v_new = _undup_heads(kvd[:, kd_w:]).reshape(batch, seq, SWA_KV, HEAD_D)
    return x, jnp.stack(new_wkv), jnp.stack(new_shift), k_new, v_new


def _prep_params(ln_mix_pre, ln_mix_post, ln_mlp_pre, ln_mlp_post, mem_norm_g, w_mem_kv, w_up, w_down,
                 a_w_in, a_mu, a_w0, a_w2, a_a0, a_a2, a_g2, a_k_k, a_k_a, a_r_k, a_lnx_g, a_lnx_b,
                 a_v0, a_v1, a_v2, a_w_out, kv_norm_g, w_kv, b_w_in, b_sinks, b_w_out):
    d = D_MODEL
    zrow = lambda n: jnp.zeros((n, TOK_W), F32)
    a_params = []
    for i in range(N_A):
        w_in = a_w_in[i]
        w_in = jnp.concatenate([w_in[:, :A_SHIFT], jnp.zeros((d, A_QMEM_OFF - A_SHIFT), F32),
                                w_in[:, A_SHIFT:]], axis=1).astype(BF16)
        pw = {
            "w_in": w_in[None],
            "mu": _pad_cols(a_mu[i].reshape(1, A_SHIFT), SHIFT_PAD),
            "w0": a_w0[i].reshape(1, TOK_W), "a0": a_a0[i].reshape(1, TOK_W),
            "w2": jnp.concatenate([a_w2[i], zrow(A_LORA)], axis=0).astype(BF16),
            "a2": jnp.concatenate([zrow(W_LORA), a_a2[i]], axis=0).astype(BF16),
            "g2": jnp.concatenate([a_g2[i], zrow(SHIFT_PAD - A_SHIFT)], axis=0).astype(BF16),
            "k_k": a_k_k[i].reshape(1, TOK_W), "k_a": a_k_a[i].reshape(1, TOK_W),
            "r_k": a_r_k[i].reshape(1, TOK_W),
            "lnx_g": a_lnx_g[i].reshape(1, TOK_W), "lnx_b": a_lnx_b[i].reshape(1, TOK_W),
        }
        if i > 0:
            pw["v0"] = a_v0[i - 1].reshape(1, TOK_W)
            pw["v1"] = a_v1[i - 1].astype(BF16)
            pw["v2"] = a_v2[i - 1].astype(BF16)
        a_params.append(pw)
    kw = SWA_KV * HEAD_D
    gain = lambda g: g.reshape(-1, 1, d)
    P = {
        "a": a_params,
        "ln_mix_pre": gain(ln_mix_pre), "ln_mix_post": gain(ln_mix_post), "ln_mlp_pre": gain(ln_mlp_pre),
        "ln_mlp_post": gain(ln_mlp_post), "mem_norm_g": gain(mem_norm_g), "kv_norm_g": gain(kv_norm_g),
        "w_up": w_up, "w_down": w_down, "w_mem_kv": w_mem_kv.astype(BF16),
        "w_kvd": jnp.concatenate([_dup_heads(w_kv[:, :kw]), _dup_heads(w_kv[:, kw:])],
                                 axis=1).astype(BF16)[None],
        "a_w_out": a_w_out.astype(BF16),
        "b_w_in": b_w_in.astype(BF16), "b_sinks": b_sinks, "b_w_out": b_w_out.astype(BF16),
    }
    return P


def kernel(x_prompt, x_sample, mem_prompt, state_wkv, state_shift, cache_win_k, cache_win_v, cache_mem_k, cache_mem_v, ln_mix_pre, ln_mix_post, ln_mlp_pre, ln_mlp_post, mem_norm_g, w_mem_kv, w_up, w_down, a_w_in, a_mu, a_w0, a_w2, a_a0, a_a2, a_g2, a_k_k, a_k_a, a_r_k, a_lnx_g, a_lnx_b, a_v0, a_v1, a_v2, a_w_out, kv_norm_g, w_kv, b_w_in, b_sinks, b_w_out):
    bp, sp, d = x_prompt.shape
    bs, ss, _ = x_sample.shape
    P = _prep_params(ln_mix_pre, ln_mix_post, ln_mlp_pre, ln_mlp_post, mem_norm_g, w_mem_kv, w_up, w_down,
                     a_w_in, a_mu, a_w0, a_w2, a_a0, a_a2, a_g2, a_k_k, a_k_a, a_r_k, a_lnx_g, a_lnx_b,
                     a_v0, a_v1, a_v2, a_w_out, kv_norm_g, w_kv, b_w_in, b_sinks, b_w_out)

    mkv = norm_matmul_layers(mem_prompt.reshape(bp * N_MEM, d), P["mem_norm_g"], P["w_mem_kv"])
    wkv0 = jnp.zeros((N_A, bp, N_HEADS, HEAD_D, HEAD_D), F32)
    shift0 = jnp.zeros((N_A, bp, 1, A_SHIFT), F32)
    y_p, wkv_p, shift_p, k_p, v_p = _trunk(
        x_prompt.reshape(bp * sp, d), mkv, mkv, wkv0, shift0, None, None, P, bp, sp, tr_mem=512)

    y_s, wkv_s, shift_s, k_s, v_s = _trunk(
        x_sample.reshape(bs * ss, d), cache_mem_k, cache_mem_v, state_wkv, state_shift, cache_win_k,
        cache_win_v, P, bs, ss, tr_mem=ss)
    past = cache_win_k.shape[1]
    win_k_s = jnp.concatenate([cache_win_k, k_s], axis=1)[:, -past:]
    win_v_s = jnp.concatenate([cache_win_v, v_s], axis=1)[:, -past:]

    return (y_p.reshape(bp, sp, d), y_s.reshape(bs, ss, d), wkv_p, shift_p,
            k_p[:, -WINDOW:], v_p[:, -WINDOW:],
            mkv[:, :, :MEM_W].reshape(DEPTH, bp, N_MEM, MEM_HEADS, MEM_HD),
            mkv[:, :, MEM_W:].reshape(DEPTH, bp, N_MEM, MEM_HEADS, MEM_HD),
            wkv_s, shift_s, win_k_s, win_v_s)
```

```python
import functools

import jax
import jax.numpy as jnp
from jax import lax
from jax.experimental import pallas as pl
from jax.experimental.pallas import tpu as pltpu

F32 = jnp.float32
BF16 = jnp.bfloat16

D_MODEL = 2048
DEPTH = 4
N_A = 2
CHUNK = 64
N_MEM = 256
MEM_HEADS = 4
MEM_HD = 128
MEM_W = MEM_HEADS * MEM_HD
TOK_W = D_MODEL - MEM_W
HEAD_D = 64
N_HEADS = TOK_W // HEAD_D
N_PAIRS = N_HEADS // 2
W_LORA = 64
A_LORA = 64
G_LORA = 224
A_SHIFT = 3 * TOK_W + W_LORA + A_LORA + G_LORA
LORA_OFF = 3 * TOK_W
GATE_OFF = LORA_OFF + 128
SHIFT_PAD = 4992
A_QMEM_OFF = 5120
A_IN_PAD = A_QMEM_OFF + MEM_W
SWA_KV = 4
SWA_G = N_HEADS // SWA_KV
WINDOW = 128
D_FF = 4 * D_MODEL
RMS_EPS = 1e-6
GN_EPS = 64e-5
NEG = -1e30
LANES = 128
VMEM_LIMIT = 56 * 1024 * 1024

_NT = (((1,), (1,)), ((), ()))
_TN = (((0,), (0,)), ((), ()))


def _cparams(*sem):
    return pltpu.CompilerParams(dimension_semantics=sem, vmem_limit_bytes=VMEM_LIMIT)


def _bdot(a, b):
    return jnp.dot(a.astype(BF16), b.astype(BF16), preferred_element_type=F32)


def _bdot_nt(a, b):
    return lax.dot_general(a.astype(BF16), b.astype(BF16), _NT, preferred_element_type=F32)


def _bdot_tn(a, b):
    return lax.dot_general(a.astype(BF16), b.astype(BF16), _TN, preferred_element_type=F32)


def _rms(x, g):
    return x * lax.rsqrt(jnp.mean(x * x, axis=-1, keepdims=True) + RMS_EPS) * g


def _norm_matmul_body(x_ref, g_ref, w_ref, o_ref, xn_ref):
    @pl.when(pl.program_id(1) == 0)
    def _():
        xn_ref[...] = _rms(x_ref[...], g_ref[...]).astype(BF16)

    o_ref[...] = jnp.dot(xn_ref[...], w_ref[...], preferred_element_type=F32).astype(o_ref.dtype)


def _proj_tiles(m, n):
    tm = min(m, 1024)
    for tn in (1408, 1024, 512):
        if n % tn == 0:
            return tm, tn
    raise ValueError(f"unsupported projection width {n}")


def norm_matmul(x, g, gl, w, wl):
    m, d = x.shape
    n = w.shape[2]
    tm, tn = _proj_tiles(m, n)
    return pl.pallas_call(
        _norm_matmul_body,
        grid=(m // tm, n // tn),
        in_specs=[pl.BlockSpec((tm, d), lambda i, j: (i, 0)),
                  pl.BlockSpec((None, 1, d), lambda i, j: (gl, 0, 0)),
                  pl.BlockSpec((None, d, tn), lambda i, j: (wl, 0, j))],
        out_specs=pl.BlockSpec((tm, tn), lambda i, j: (i, j)),
        out_shape=jax.ShapeDtypeStruct((m, n), F32),
        scratch_shapes=[pltpu.VMEM((tm, d), BF16)],
        compiler_params=_cparams("parallel", "arbitrary"),
        name="norm_matmul",
    )(x, g, w)


def norm_matmul_layers(x, g, w):
    m, d = x.shape
    nl, _, n = w.shape
    _, tn = _proj_tiles(m, n)
    return pl.pallas_call(
        _norm_matmul_body,
        grid=(nl, n // tn),
        in_specs=[pl.BlockSpec((m, d), lambda l, j: (0, 0)),
                  pl.BlockSpec((None, 1, d), lambda l, j: (l, 0, 0)),
                  pl.BlockSpec((None, d, tn), lambda l, j: (l, 0, j))],
        out_specs=pl.BlockSpec((None, m, tn), lambda l, j: (l, 0, j)),
        out_shape=jax.ShapeDtypeStruct((nl, m, n), F32),
        scratch_shapes=[pltpu.VMEM((m, d), BF16)],
        compiler_params=_cparams("parallel", "arbitrary"),
        name="norm_matmul_layers",
    )(x, g, w)


def _mix_out_body(tok_ref, mem_ref, wt_ref, wm_ref, g_ref, x_ref, o_ref):
    y = jnp.dot(tok_ref[...], wt_ref[...], preferred_element_type=F32)
    y = y + jnp.dot(mem_ref[...], wm_ref[...], preferred_element_type=F32)
    o_ref[...] = x_ref[...] + _rms(y, g_ref[...])


def mix_out(tok, mem, w_out, wl, g, gl, x):
    m, d = x.shape
    tm = min(m, 512)
    return pl.pallas_call(
        _mix_out_body,
        grid=(m // tm,),
        in_specs=[pl.BlockSpec((tm, TOK_W), lambda i: (i, 0)),
                  pl.BlockSpec((tm, MEM_W), lambda i: (i, 0)),
                  pl.BlockSpec((None, TOK_W, d), lambda i: (wl, 0, 0)),
                  pl.BlockSpec((None, MEM_W, d), lambda i: (wl, TOK_W // MEM_W, 0)),
                  pl.BlockSpec((None, 1, d), lambda i: (gl, 0, 0)),
                  pl.BlockSpec((tm, d), lambda i: (i, 0))],
        out_specs=pl.BlockSpec((tm, d), lambda i: (i, 0)),
        out_shape=jax.ShapeDtypeStruct((m, d), F32),
        compiler_params=_cparams("parallel"),
        name="mix_out",
    )(tok, mem, w_out, w_out, g, x)


def _mlp_body(x_ref, g1_ref, wu_ref, wd_ref, g2_ref, o_ref, *rest):
    xn_ref = rest[-1]
    f = pl.program_id(1)

    @pl.when(f == 0)
    def _():
        xn_ref[...] = _rms(x_ref[...], g1_ref[...]).astype(BF16)
        o_ref[...] = jnp.zeros_like(o_ref)

    wu = wu_ref[...].astype(BF16)
    wd = wd_ref[...].astype(BF16)
    if len(rest) == 3:
        rest[0][...] = wu
        rest[1][...] = wd
    h = jnp.maximum(jnp.dot(xn_ref[...], wu, preferred_element_type=F32), 0.0)
    o_ref[...] += jnp.dot((h * h).astype(BF16), wd, preferred_element_type=F32)

    @pl.when(f == pl.num_programs(1) - 1)
    def _():
        o_ref[...] = x_ref[...] + _rms(o_ref[...], g2_ref[...])


def mlp(x, g1, w_up, w_down, g2, l, wl):
    m, d = x.shape
    ff = w_up.shape[2]
    emit = w_up.dtype != BF16
    tm, tf = min(m, 1024), 512
    once = pl.Buffered(1)
    out_specs = [pl.BlockSpec((tm, d), lambda i, f: (i, 0))]
    out_shape = [jax.ShapeDtypeStruct((m, d), F32)]
    if emit:
        out_specs += [pl.BlockSpec((d, tf), lambda i, f: (0, f)), pl.BlockSpec((tf, d), lambda i, f: (f, 0))]
        out_shape += [jax.ShapeDtypeStruct((d, ff), BF16), jax.ShapeDtypeStruct((ff, d), BF16)]
    outs = pl.pallas_call(
        _mlp_body,
        grid=(m // tm, ff // tf),
        in_specs=[pl.BlockSpec((tm, d), lambda i, f: (i, 0), pipeline_mode=once),
                  pl.BlockSpec((None, 1, d), lambda i, f: (l, 0, 0)),
                  pl.BlockSpec((None, d, tf), lambda i, f: (wl, 0, f)),
                  pl.BlockSpec((None, tf, d), lambda i, f: (wl, f, 0)),
                  pl.BlockSpec((None, 1, d), lambda i, f: (l, 0, 0))],
        out_specs=out_specs,
        out_shape=out_shape,
        scratch_shapes=[pltpu.VMEM((tm, d), BF16)],
        compiler_params=_cparams("parallel", "arbitrary"),
        name="mlp",
    )(x, g1, w_up, w_down, g2)
    return outs if emit else outs[0]


def _mem_attn_body(q_ref, k_ref, v_ref, o_ref):
    scale = MEM_HD ** -0.5
    nb = k_ref.shape[0]
    rq = q_ref.shape[0] // nb
    sls = [slice(h * MEM_HD, (h + 1) * MEM_HD) for h in range(MEM_HEADS)]
    rows = [slice(b * rq, (b + 1) * rq) for b in range(nb)]
    head = ((lambda ref, b, h: ref[b, :, h, :]) if len(k_ref.shape) == 4
            else (lambda ref, b, h: ref[b, :, sls[h]]))
    probs = [(b, h) for b in range(nb) for h in range(MEM_HEADS)]
    s = [_bdot_nt(q_ref[rows[b], sls[h]], head(k_ref, b, h)) * scale for b, h in probs]
    e = [jnp.exp(t - jnp.max(t, axis=-1, keepdims=True)) for t in s]
    p = [t / jnp.sum(t, axis=-1, keepdims=True) for t in e]
    o = [_bdot(p[n], head(v_ref, b, h)) for n, (b, h) in enumerate(probs)]
    for n, (b, h) in enumerate(probs):
        o_ref[rows[b], sls[h]] = o[n].astype(o_ref.dtype)


def mem_attn(proj, q_col_block, mem_k, mem_v, l, batch, rows_per_seq):
    m = proj.shape[0]
    if mem_k.ndim == 5:
        nb = _batch_block(batch, 4)
        tr = nb * rows_per_seq
        blk = (None, nb, N_MEM, MEM_HEADS, MEM_HD)
        k_spec = pl.BlockSpec(blk, lambda i: (l, i, 0, 0, 0))
        v_spec = k_spec
    else:
        tr = min(rows_per_seq, 512)
        tiles_per_seq = rows_per_seq // tr
        blk = (None, 1, N_MEM, MEM_W)
        k_spec = pl.BlockSpec(blk, lambda i: (l, i // tiles_per_seq, 0, 0))
        v_spec = pl.BlockSpec(blk, lambda i: (l, i // tiles_per_seq, 0, 1))
    return pl.pallas_call(
        _mem_attn_body,
        grid=(m // tr,),
        in_specs=[pl.BlockSpec((tr, MEM_W), lambda i: (i, q_col_block)), k_spec, v_spec],
        out_specs=pl.BlockSpec((tr, MEM_W), lambda i: (i, 0)),
        out_shape=jax.ShapeDtypeStruct((m, MEM_W), BF16),
        compiler_params=_cparams("parallel"),
        name="mem_attn",
    )(proj, mem_k, mem_v)


def _swa_core(units, q_tile, k_tiles, v_tiles, valid_of, sink_ref, o_store, lq):
    lane = lax.broadcasted_iota(jnp.int32, (lq, LANES), 1)
    low = lane < HEAD_D
    probs = [(u, j) for u in units for j in range(SWA_KV)]
    s = []
    for u, j in probs:
        xs = []
        for i in range(SWA_G // 2):
            pair = q_tile(u, (j * SWA_G + 2 * i) * HEAD_D) * (HEAD_D ** -0.5)
            xs.append(jnp.where(low, pair, 0.0).astype(BF16))
            xs.append(jnp.where(low, 0.0, pair).astype(BF16))
        x = jnp.concatenate(xs, axis=0)
        kj = jnp.concatenate([t.astype(BF16) for t in k_tiles(u, j)], axis=0)
        sj = lax.dot_general(x, kj, _NT, preferred_element_type=F32)
        valid = valid_of(u)
        s.append(sj if valid is None else jnp.where(valid, sj, NEG))
    p = []
    for n, (u, j) in enumerate(probs):
        sink = sink_ref[j]
        mx = jnp.maximum(jnp.max(s[n], axis=-1, keepdims=True), sink)
        e = jnp.exp(s[n] - mx)
        p.append((e / (jnp.sum(e, axis=-1, keepdims=True) + jnp.exp(sink - mx))).astype(BF16))
    o = [jnp.dot(p[n], jnp.concatenate([t.astype(BF16) for t in v_tiles(u, j)], axis=0),
                 preferred_element_type=F32) for n, (u, j) in enumerate(probs)]
    for n, (u, j) in enumerate(probs):
        for i in range(SWA_G // 2):
            top = o[n][(2 * i) * lq:(2 * i + 1) * lq]
            bot = o[n][(2 * i + 1) * lq:(2 * i + 2) * lq]
            o_store(u, (j * SWA_G + 2 * i) * HEAD_D, jnp.where(low, top, bot))


def _swa_prompt_body(q_ref, ka, kb, va, vb, sink_ref, o_ref):
    c2 = pl.program_id(1)
    col = lax.broadcasted_iota(jnp.int32, (SWA_G * CHUNK, 3 * CHUNK), 1)
    units = [(b, h) for b in range(q_ref.shape[0]) for h in range(2)]
    rows = lambda h: slice(h * CHUNK, (h + 1) * CHUNK)

    def tiles(ra, rb):
        def get(u, j):
            b, h = u
            ls = slice(j * LANES, (j + 1) * LANES)
            blocks = [ra[b, rows(0), ls], ra[b, rows(1), ls], rb[b, rows(0), ls], rb[b, rows(1), ls]]
            return blocks[h:h + 3]
        return get

    def store(u, c0, tile):
        o_ref[u[0], rows(u[1]), c0:c0 + LANES] = tile.astype(o_ref.dtype)

    _swa_core(units, lambda u, c0: q_ref[u[0], rows(u[1]), c0:c0 + LANES], tiles(ka, kb), tiles(va, vb),
              lambda u: col >= jnp.maximum(2 - (2 * c2 + u[1]), 0) * CHUNK,
              sink_ref, store, CHUNK)


def _batch_block(batch, want):
    return want if batch % want == 0 else 1


def swa_prompt(proj, kvd, sink_col, batch, seq):
    rows = 2 * CHUNK
    kd_w = SWA_KV * LANES
    nb = _batch_block(batch, 2)
    kv_prev = lambda col: pl.BlockSpec((nb, rows, kd_w), lambda b, c: (b, jnp.maximum(c - 1, 0), col))
    kv_same = lambda col: pl.BlockSpec((nb, rows, kd_w), lambda b, c: (b, c, col))
    kv3 = kvd.reshape(batch, seq, 2 * kd_w)
    out = pl.pallas_call(
        _swa_prompt_body,
        grid=(batch // nb, seq // rows),
        in_specs=[pl.BlockSpec((nb, rows, TOK_W), lambda b, c: (b, c, 0)),
                  kv_prev(0), kv_same(0), kv_prev(1), kv_same(1),
                  pl.BlockSpec((SWA_KV, SWA_G * CHUNK, 1), lambda b, c: (0, 0, 0))],
        out_specs=pl.BlockSpec((nb, rows, TOK_W), lambda b, c: (b, c, 0)),
        out_shape=jax.ShapeDtypeStruct((batch, seq, TOK_W), BF16),
        compiler_params=_cparams("parallel", "arbitrary"),
        name="swa_prompt",
    )(proj.reshape(batch, seq, -1), kv3, kv3, kv3, kv3, sink_col)
    return out.reshape(batch * seq, TOK_W)


def _swa_sample_body(q_ref, k_ref, v_ref, sink_ref, o_ref):
    nb, lq, _ = q_ref.shape

    def store(b, c0, tile):
        o_ref[b, :, c0:c0 + LANES] = tile.astype(o_ref.dtype)

    _swa_core(range(nb), lambda b, c0: q_ref[b, :, c0:c0 + LANES],
              lambda b, j: [k_ref[b, :, j * LANES:(j + 1) * LANES]],
              lambda b, j: [v_ref[b, :, j * LANES:(j + 1) * LANES]],
              lambda b: None, sink_ref, store, lq)


def swa_sample(proj, k_all, v_all, sink_col, batch, lq):
    keys = k_all.shape[1]
    kd_w = SWA_KV * LANES
    nb = _batch_block(batch, 4)
    out = pl.pallas_call(
        _swa_sample_body,
        grid=(batch // nb,),
        in_specs=[pl.BlockSpec((nb, lq, TOK_W), lambda b: (b, 0, 0)),
                  pl.BlockSpec((nb, keys, kd_w), lambda b: (b, 0, 0)),
                  pl.BlockSpec((nb, keys, kd_w), lambda b: (b, 0, 0)),
                  pl.BlockSpec((SWA_KV, SWA_G * lq, 1), lambda b: (0, 0, 0))],
        out_specs=pl.BlockSpec((nb, lq, TOK_W), lambda b: (b, 0, 0)),
        out_shape=jax.ShapeDtypeStruct((batch, lq, TOK_W), BF16),
        compiler_params=_cparams("parallel"),
        name="swa_sample",
    )(proj.reshape(batch, lq, -1), k_all, v_all, sink_col)
    return out.reshape(batch * lq, TOK_W)


def _pair_ones():
    r = lax.broadcasted_iota(jnp.int32, (LANES, LANES), 0)
    c = lax.broadcasted_iota(jnp.int32, (LANES, LANES), 1)
    return jnp.where((r < HEAD_D) == (c < HEAD_D), 1.0, 0.0).astype(BF16)


def _head_sum(x, ones_bd):
    return jnp.dot(x.astype(BF16), ones_bd, preferred_element_type=F32)


def _pre_stage(p_ref, prev_ref, wr, vf_ref, outs):
    has_vres = vf_ref is not None
    mu_ref, w0_ref, w2_ref, a0_ref, a2_ref, g2_ref = (wr[n] for n in ("mu", "w0", "w2", "a0", "a2", "g2"))
    kk_ref, ka_ref, rk_ref = wr["k_k"], wr["k_a"], wr["r_k"]
    if has_vres:
        v0_ref, v1_ref, v2_ref = wr["v0"], wr["v1"], wr["v2"]
    r_o, lw_o, k_o, v_o, kn_o, b_o, g_o, bonus_o = outs
    tr = p_ref.shape[0]
    row0 = lax.broadcasted_iota(jnp.int32, (tr, LANES), 0) == 0
    ones_bd = _pair_ones()

    def mixed(c0, width):
        outs_ = []
        for t in range(width // LANES):
            sl = slice(c0 + t * LANES, c0 + (t + 1) * LANES)
            p = p_ref[:, sl]
            pp = jnp.where(row0, prev_ref[:, sl], pltpu.roll(p, 1, 0))
            outs_.append(p + (pp - p) * mu_ref[:, sl])
        return outs_

    lora = mixed(LORA_OFF, LANES)[0]
    gate_in = jnp.concatenate(mixed(GATE_OFF, SHIFT_PAD - GATE_OFF), axis=1)
    w_pre = w0_ref[...] + _bdot(jnp.tanh(lora), w2_ref[...])
    lw_o[...] = -jnp.exp(-jax.nn.softplus(-w_pre) - 0.5)
    a = jax.nn.sigmoid(a0_ref[...] + _bdot(lora, a2_ref[...]))
    g_o[...] = _bdot(jax.nn.sigmoid(gate_in), g2_ref[...])

    r_t = mixed(0, TOK_W)
    k_t = mixed(TOK_W, TOK_W)
    v_t = mixed(2 * TOK_W, TOK_W)
    if has_vres:
        v_all = jnp.concatenate(v_t, axis=1)
        gate = jax.nn.sigmoid(v0_ref[...] + _bdot(_bdot(v_all, v1_ref[...]), v2_ref[...]))
        v_all = v_all + (vf_ref[...] - v_all) * gate
        v_t = [v_all[:, t * LANES:(t + 1) * LANES] for t in range(N_PAIRS)]
    for t in range(N_PAIRS):
        sl = slice(t * LANES, (t + 1) * LANES)
        kraw = k_t[t]
        kk = kraw * kk_ref[:, sl]
        kk = kk / jnp.maximum(jnp.sqrt(_head_sum(kk * kk, ones_bd)), 1e-12)
        a_t = a[:, sl]
        k = kraw * (1.0 + (a_t - 1.0) * ka_ref[:, sl])
        v = v_t[t]
        r_o[:, sl] = r_t[t]
        k_o[:, sl] = k
        v_o[:, sl] = v
        kn_o[:, sl] = kk
        b_o[:, sl] = kk * a_t
        bonus_o[:, sl] = _head_sum(r_t[t] * k * rk_ref[:, sl], ones_bd) * v


def _wkv_stage(mid, lg_ref, lb_ref, o_ref, s_ref, L):
    r_ref, lw_ref, k_ref, v_ref, kn_ref, b_ref, g_ref, bonus_ref = mid
    nch = r_ref.shape[0] // L
    W2 = 2 * L

    t_i = lax.broadcasted_iota(jnp.int32, (L, W2), 0)
    l_i = lax.broadcasted_iota(jnp.int32, (L, W2), 1)
    j_i = jnp.where(l_i >= L, l_i - L, l_i)
    strict = t_i > j_i
    incl = t_i >= j_i
    eye2 = jnp.where(t_i == j_i, 1.0, 0.0)
    levels = []
    s = 1
    while s < L:
        sh = s.bit_length() - 1
        levels.append(((t_i >> (sh + 1)) == (j_i >> (sh + 1)))
                      & (((t_i >> sh) & 1) == 1) & (((j_i >> sh) & 1) == 0))
        s *= 2
    first_t = l_i < L
    lane = lax.broadcasted_iota(jnp.int32, (L, LANES), 1)
    low = lane < HEAD_D
    rr = lax.broadcasted_iota(jnp.int32, (LANES, LANES), 0)
    cc = lax.broadcasted_iota(jnp.int32, (LANES, LANES), 1)
    bd = (rr < HEAD_D) == (cc < HEAD_D)
    ones_bd = jnp.where(bd, 1.0, 0.0).astype(BF16)
    tri = jnp.where(lax.broadcasted_iota(jnp.int32, (L, L), 0)
                    >= lax.broadcasted_iota(jnp.int32, (L, L), 1), 1.0, 0.0)

    def stack_t(x):
        return jnp.concatenate([jnp.where(first_t, x, 0.0), jnp.where(first_t, 0.0, x)],
                               axis=0).astype(BF16)

    def stack_d(x):
        return jnp.concatenate([jnp.where(low, x, 0.0), jnp.where(low, 0.0, x)], axis=0).astype(BF16)

    def tdot(m, stacked):
        return jnp.dot(m.astype(BF16), stacked, preferred_element_type=F32)

    cum_all = [jnp.dot(tri, lw_ref[ch * L:(ch + 1) * L, :], precision=lax.Precision.HIGHEST,
                       preferred_element_type=F32) for ch in range(nch)]

    probs = [(ch, hp) for ch in range(nch) for hp in range(N_PAIRS)]
    pairs = range(len(probs))
    rows = [slice(ch * L, (ch + 1) * L) for ch, _ in probs]
    sls = [slice(hp * LANES, (hp + 1) * LANES) for _, hp in probs]
    a_t, r_t, sv, bh, kh, dec, cb, ck = [], [], [], [], [], [], [], []
    for p in pairs:
        rw, sl = rows[p], sls[p]
        lw = lw_ref[rw, sl]
        cum = cum_all[probs[p][0]][:, sl]
        cum_l = cum[L - 1:L, :]
        p_inv = jnp.exp(-cum)
        p_end = jnp.exp(cum_l - cum)
        b = b_ref[rw, sl]
        k = k_ref[rw, sl]
        a_t.append(-kn_ref[rw, sl] * jnp.exp(cum - lw))
        r_t.append(r_ref[rw, sl] * jnp.exp(cum))
        sv.append(stack_d(v_ref[rw, sl]))
        bh.append((b * p_end).astype(BF16))
        kh.append((k * p_end).astype(BF16))
        dec.append(jnp.exp(cum_l))
        x = jnp.concatenate([a_t[p], r_t[p]], axis=0).astype(BF16)
        cb.append(lax.dot_general(x, stack_d(b * p_inv), _NT, preferred_element_type=F32))
        ck.append(lax.dot_general(x, stack_d(k * p_inv), _NT, preferred_element_type=F32))
    a_ab = [jnp.where(strict, cb[p][:L], 0.0) for p in pairs]
    a_rb = [jnp.where(incl, cb[p][L:], 0.0).astype(BF16) for p in pairs]
    a_k = [jnp.concatenate([jnp.where(strict, ck[p][:L], 0.0), jnp.where(incl, ck[p][L:], 0.0)],
                           axis=0).astype(BF16) for p in pairs]
    kv = [jnp.dot(a_k[p], sv[p], preferred_element_type=F32) for p in pairs]
    t_m = [eye2 + jnp.where(levels[0], a_ab[p], 0.0) for p in pairs]
    for msk in levels[1:]:
        ta = [tdot(t_m[p], stack_t(jnp.where(msk, a_ab[p], 0.0))) for p in pairs]
        t_m = [t_m[p] + tdot(ta[p], stack_t(t_m[p])) for p in pairs]
    t_b = [t_m[p].astype(BF16) for p in pairs]
    w12 = [jnp.dot(t_b[p], jnp.concatenate([stack_d(a_t[p]), stack_d(kv[p][:L])], axis=1),
                   preferred_element_type=F32) for p in pairs]
    qq = [jnp.dot(a_rb[p], jnp.concatenate([stack_d(w12[p][:, :LANES]), stack_d(w12[p][:, LANES:])],
                                           axis=1), preferred_element_type=F32) for p in pairs]
    mm = [_bdot_tn(w12[p], bh[p]) for p in pairs]
    mk = [_bdot_tn(v_ref[rows[p], sls[p]], kh[p]) for p in pairs]
    y = []
    for p in pairs:
        hp = probs[p][1]
        s_p = s_ref[hp]
        q1 = r_t[p] + qq[p][:, :LANES]
        q2 = qq[p][:, LANES:] + kv[p][L:]
        m1 = jnp.where(bd, mm[p][:LANES], 0.0)
        m2 = jnp.where(bd, mm[p][LANES:] + mk[p], 0.0)
        y.append(_bdot_nt(q1, s_p) + q2)
        s_ref[hp] = s_p * dec[p] + _bdot(s_p, m1) + m2
    mean = [_head_sum(y[p], ones_bd) * (1.0 / HEAD_D) for p in pairs]
    dv = [y[p] - mean[p] for p in pairs]
    var = [_head_sum(dv[p] * dv[p], ones_bd) * (1.0 / HEAD_D) for p in pairs]
    for p in pairs:
        rw, sl = rows[p], sls[p]
        yn = dv[p] * lax.rsqrt(var[p] + GN_EPS) * lg_ref[:, sl] + lb_ref[:, sl]
        o_ref[rw, sl] = ((yn + bonus_ref[rw, sl]) * g_ref[rw, sl]).astype(o_ref.dtype)


_PRE_W = ("mu", "w0", "w2", "a0", "a2", "g2", "k_k", "k_a", "r_k")
_VRES_W = ("v0", "v1", "v2")
_MID = 8


def _rwkv_body(*refs, chunk, has_vres):
    names = _PRE_W + (_VRES_W if has_vres else ())
    it = iter(refs)
    p_ref, shift_ref = next(it), next(it)
    vf_ref = next(it) if has_vres else None
    wr = {n: next(it) for n in names}
    s0_ref, lg_ref, lb_ref = next(it), next(it), next(it)
    o_ref = next(it)
    v_out_ref = None if has_vres else next(it)
    s_out_ref, s_ref, prev_ref = next(it), next(it), next(it)
    mid = [next(it) for _ in range(_MID)]
    c = pl.program_id(1)

    @pl.when(c == 0)
    def _():
        zero = jnp.zeros((HEAD_D, HEAD_D), F32)
        for hp in range(N_PAIRS):
            top = jnp.concatenate([s0_ref[2 * hp], zero], axis=1)
            bot = jnp.concatenate([zero, s0_ref[2 * hp + 1]], axis=1)
            s_ref[hp] = jnp.concatenate([top, bot], axis=0)
        prev_ref[...] = shift_ref[...]

    _pre_stage(p_ref, prev_ref, wr, vf_ref, mid)
    prev_ref[...] = p_ref[p_ref.shape[0] - 1:, :]
    if v_out_ref is not None:
        v_out_ref[...] = mid[3][...]
    _wkv_stage(mid, lg_ref, lb_ref, o_ref, s_ref, chunk)

    @pl.when(c == pl.num_programs(1) - 1)
    def _():
        for hp in range(N_PAIRS):
            s_p = s_ref[hp]
            s_out_ref[2 * hp] = s_p[:HEAD_D, :HEAD_D]
            s_out_ref[2 * hp + 1] = s_p[HEAD_D:, HEAD_D:]


def rwkv_mix(proj, shift0, pw, v_first, s0, batch, seq, chunk, rows):
    steps = seq // rows
    has_vres = v_first is not None
    names = _PRE_W + (_VRES_W if has_vres else ())
    full = lambda a: pl.BlockSpec(a.shape, lambda bi, c: (0,) * a.ndim)
    tile = pl.BlockSpec((rows, TOK_W), lambda bi, c: (bi * steps + c, 0))
    state = pl.BlockSpec((None, N_HEADS, HEAD_D, HEAD_D), lambda bi, c: (bi, 0, 0, 0))
    in_specs = [pl.BlockSpec((rows, SHIFT_PAD), lambda bi, c: (bi * steps + c, 0)),
                pl.BlockSpec((None, 1, SHIFT_PAD), lambda bi, c: (bi, 0, 0))]
    args = [proj, shift0]
    if has_vres:
        in_specs.append(tile)
        args.append(v_first)
    in_specs += [full(pw[n]) for n in names] + [state, full(pw["lnx_g"]), full(pw["lnx_b"])]
    args += [pw[n] for n in names] + [s0, pw["lnx_g"], pw["lnx_b"]]
    out_specs = [tile] + ([] if has_vres else [tile]) + [state]
    out_shape = ([jax.ShapeDtypeStruct((batch * seq, TOK_W), BF16)]
                 + ([] if has_vres else [jax.ShapeDtypeStruct((batch * seq, TOK_W), F32)])
                 + [jax.ShapeDtypeStruct((batch, N_HEADS, HEAD_D, HEAD_D), F32)])
    outs = pl.pallas_call(
        functools.partial(_rwkv_body, chunk=chunk, has_vres=has_vres),
        grid=(batch, steps),
        in_specs=in_specs,
        out_specs=out_specs,
        out_shape=out_shape,
        scratch_shapes=[pltpu.VMEM((N_PAIRS, LANES, LANES), F32), pltpu.VMEM((1, SHIFT_PAD), F32)]
                       + [pltpu.VMEM((rows, TOK_W), F32)] * _MID,
        compiler_params=_cparams("parallel", "arbitrary"),
        name="rwkv_mix",
    )(*args)
    if has_vres:
        return outs[0], None, outs[1]
    return outs


def _dup_heads(w):
    lead = w.shape[:-1]
    w = w.reshape(lead + (SWA_KV, 1, HEAD_D))
    return jnp.broadcast_to(w, lead + (SWA_KV, 2, HEAD_D)).reshape(lead + (SWA_KV * LANES,))


def _undup_heads(x):
    lead = x.shape[:-1]
    return x.reshape(lead + (SWA_KV, 2, HEAD_D))[..., 0, :]


def _pad_cols(a, width):
    return jnp.pad(a, [(0, 0)] * (a.ndim - 1) + [(0, width - a.shape[-1])])


def _sink_col(sinks, lq):
    return jnp.repeat(sinks.astype(F32).reshape(SWA_KV, SWA_G), lq, axis=1)[..., None]


def _trunk(x, mem_k, mem_v, wkv0, shift0, past_k, past_v, P, batch, seq, mlp_bf16):
    chunk = min(CHUNK, seq)
    rows_mix = min(2 * chunk, seq)
    new_wkv, new_shift = [], []
    v_first = None
    kvd = None
    for l in range(DEPTH):
        if l < N_A:
            pw = P["a"][l]
            proj = norm_matmul(x, P["ln_mix_pre"], l, pw["w_in"], 0)
            new_shift.append(proj.reshape(batch, seq, A_IN_PAD)[:, -1:, :A_SHIFT])
            tok, v, s_new = rwkv_mix(proj, _pad_cols(shift0[l].astype(F32), SHIFT_PAD), pw, v_first,
                                     wkv0[l].astype(F32), batch, seq, chunk, rows_mix)
            if l == 0:
                v_first = v
            new_wkv.append(s_new)
            q_col = A_QMEM_OFF // MEM_W
            w_out, wi = P["a_w_out"], l
        else:
            i = l - N_A
            if kvd is None:
                kvd = norm_matmul(x, P["kv_norm_g"], 0, P["w_kvd"], 0)
            proj = norm_matmul(x, P["ln_mix_pre"], l, P["b_w_in"], i)
            if past_k is None:
                tok = swa_prompt(proj, kvd, _sink_col(P["b_sinks"][i], CHUNK), batch, seq)
            else:
                kd_w = SWA_KV * LANES
                k_all = jnp.concatenate([_dup_heads(past_k.reshape(batch, -1, SWA_KV * HEAD_D)),
                                         kvd[:, :kd_w].reshape(batch, seq, kd_w)], axis=1)
                v_all = jnp.concatenate([_dup_heads(past_v.reshape(batch, -1, SWA_KV * HEAD_D)),
                                         kvd[:, kd_w:].reshape(batch, seq, kd_w)], axis=1)
                tok = swa_sample(proj, k_all, v_all, _sink_col(P["b_sinks"][i], seq), batch, seq)
            q_col = TOK_W // MEM_W
            w_out, wi = P["b_w_out"], i
        mem_o = mem_attn(proj, q_col, mem_k, mem_v, l, batch, seq)
        x = mix_out(tok, mem_o, w_out, wi, P["ln_mix_post"], l, x)
        if l in mlp_bf16:
            wu, wd = mlp_bf16[l]
            x = mlp(x, P["ln_mlp_pre"], wu, wd, P["ln_mlp_post"], l, 0)
        else:
            x, wu, wd = mlp(x, P["ln_mlp_pre"], P["w_up"], P["w_down"], P["ln_mlp_post"], l, l)
            mlp_bf16[l] = (wu[None], wd[None])
    kd_w = SWA_KV * LANES
    k_new = _undup_heads(kvd[:, :kd_w]).reshape(batch, seq, SWA_KV, HEAD_D)
    v_new = _undup_heads(kvd[:, kd_w:]).reshape(batch, seq, SWA_KV, HEAD_D)
    return x, jnp.stack(new_wkv), jnp.stack(new_shift), k_new, v_new


def _prep_params(ln_mix_pre, ln_mix_post, ln_mlp_pre, ln_mlp_post, mem_norm_g, w_mem_kv, w_up, w_down,
                 a_w_in, a_mu, a_w0, a_w2, a_a0, a_a2, a_g2, a_k_k, a_k_a, a_r_k, a_lnx_g, a_lnx_b,
                 a_v0, a_v1, a_v2, a_w_out, kv_norm_g, w_kv, b_w_in, b_sinks, b_w_out):
    d = D_MODEL
    zrow = lambda n: jnp.zeros((n, TOK_W), F32)
    a_params = []
    for i in range(N_A):
        w_in = a_w_in[i]
        w_in = jnp.concatenate([w_in[:, :A_SHIFT], jnp.zeros((d, A_QMEM_OFF - A_SHIFT), F32),
                                w_in[:, A_SHIFT:]], axis=1).astype(BF16)
        pw = {
            "w_in": w_in[None],
            "mu": _pad_cols(a_mu[i].reshape(1, A_SHIFT), SHIFT_PAD),
            "w0": a_w0[i].reshape(1, TOK_W), "a0": a_a0[i].reshape(1, TOK_W),
            "w2": jnp.concatenate([a_w2[i], zrow(A_LORA)], axis=0).astype(BF16),
            "a2": jnp.concatenate([zrow(W_LORA), a_a2[i]], axis=0).astype(BF16),
            "g2": jnp.concatenate([a_g2[i], zrow(SHIFT_PAD - A_SHIFT)], axis=0).astype(BF16),
            "k_k": a_k_k[i].reshape(1, TOK_W), "k_a": a_k_a[i].reshape(1, TOK_W),
            "r_k": a_r_k[i].reshape(1, TOK_W),
            "lnx_g": a_lnx_g[i].reshape(1, TOK_W), "lnx_b": a_lnx_b[i].reshape(1, TOK_W),
        }
        if i > 0:
            pw["v0"] = a_v0[i - 1].reshape(1, TOK_W)
            pw["v1"] = a_v1[i - 1].astype(BF16)
            pw["v2"] = a_v2[i - 1].astype(BF16)
        a_params.append(pw)
    kw = SWA_KV * HEAD_D
    gain = lambda g: g.reshape(-1, 1, d)
    P = {
        "a": a_params,
        "ln_mix_pre": gain(ln_mix_pre), "ln_mix_post": gain(ln_mix_post), "ln_mlp_pre": gain(ln_mlp_pre),
        "ln_mlp_post": gain(ln_mlp_post), "mem_norm_g": gain(mem_norm_g), "kv_norm_g": gain(kv_norm_g),
        "w_up": w_up, "w_down": w_down, "w_mem_kv": w_mem_kv.astype(BF16),
        "w_kvd": jnp.concatenate([_dup_heads(w_kv[:, :kw]), _dup_heads(w_kv[:, kw:])],
                                 axis=1).astype(BF16)[None],
        "a_w_out": a_w_out.astype(BF16),
        "b_w_in": b_w_in.astype(BF16), "b_sinks": b_sinks, "b_w_out": b_w_out.astype(BF16),
    }
    return P


def kernel(x_prompt, x_sample, mem_prompt, state_wkv, state_shift, cache_win_k, cache_win_v, cache_mem_k, cache_mem_v, ln_mix_pre, ln_mix_post, ln_mlp_pre, ln_mlp_post, mem_norm_g, w_mem_kv, w_up, w_down, a_w_in, a_mu, a_w0, a_w2, a_a0, a_a2, a_g2, a_k_k, a_k_a, a_r_k, a_lnx_g, a_lnx_b, a_v0, a_v1, a_v2, a_w_out, kv_norm_g, w_kv, b_w_in, b_sinks, b_w_out):
    bp, sp, d = x_prompt.shape
    bs, ss, _ = x_sample.shape
    P = _prep_params(ln_mix_pre, ln_mix_post, ln_mlp_pre, ln_mlp_post, mem_norm_g, w_mem_kv, w_up, w_down,
                     a_w_in, a_mu, a_w0, a_w2, a_a0, a_a2, a_g2, a_k_k, a_k_a, a_r_k, a_lnx_g, a_lnx_b,
                     a_v0, a_v1, a_v2, a_w_out, kv_norm_g, w_kv, b_w_in, b_sinks, b_w_out)

    mlp_bf16 = {}
    y_s, wkv_s, shift_s, k_s, v_s = _trunk(
        x_sample.reshape(bs * ss, d), cache_mem_k, cache_mem_v, state_wkv, state_shift, cache_win_k,
        cache_win_v, P, bs, ss, mlp_bf16)
    past = cache_win_k.shape[1]
    win_k_s = jnp.concatenate([cache_win_k, k_s], axis=1)[:, -past:]
    win_v_s = jnp.concatenate([cache_win_v, v_s], axis=1)[:, -past:]

    mkv = norm_matmul_layers(mem_prompt.reshape(bp * N_MEM, d), P["mem_norm_g"], P["w_mem_kv"])
    mkv4 = mkv.reshape(DEPTH, bp, N_MEM, 2 * MEM_W)
    wkv0 = jnp.zeros((N_A, bp, N_HEADS, HEAD_D, HEAD_D), F32)
    shift0 = jnp.zeros((N_A, bp, 1, A_SHIFT), F32)
    y_p, wkv_p, shift_p, k_p, v_p = _trunk(
        x_prompt.reshape(bp * sp, d), mkv4, mkv4, wkv0, shift0, None, None, P, bp, sp, mlp_bf16)

    return (y_p.reshape(bp, sp, d), y_s.reshape(bs, ss, d), wkv_p, shift_p,
            k_p[:, -WINDOW:], v_p[:, -WINDOW:],
            mkv[:, :, :MEM_W].reshape(DEPTH, bp, N_MEM, MEM_HEADS, MEM_HD),
            mkv[:, :, MEM_W:].reshape(DEPTH, bp, N_MEM, MEM_HEADS, MEM_HD),
            wkv_s, shift_s, win_k_s, win_v_s)
```

```python
import functools

import jax
import jax.numpy as jnp
from jax import lax
from jax.experimental import pallas as pl
from jax.experimental.pallas import tpu as pltpu

F32 = jnp.float32
BF16 = jnp.bfloat16

D_MODEL = 2048
DEPTH = 4
N_A = 2
CHUNK = 64
N_MEM = 256
MEM_HEADS = 4
MEM_HD = 128
MEM_W = MEM_HEADS * MEM_HD
TOK_W = D_MODEL - MEM_W
HEAD_D = 64
N_HEADS = TOK_W // HEAD_D
N_PAIRS = N_HEADS // 2
W_LORA = 64
A_LORA = 64
G_LORA = 224
A_SHIFT = 3 * TOK_W + W_LORA + A_LORA + G_LORA
LORA_OFF = 3 * TOK_W
GATE_OFF = LORA_OFF + 128
SHIFT_PAD = 4992
A_QMEM_OFF = 5120
A_IN_PAD = A_QMEM_OFF + MEM_W
SWA_KV = 4
SWA_G = N_HEADS // SWA_KV
WINDOW = 128
D_FF = 4 * D_MODEL
RMS_EPS = 1e-6
GN_EPS = 64e-5
NEG = -1e30
LANES = 128
VMEM_LIMIT = 56 * 1024 * 1024

_NT = (((1,), (1,)), ((), ()))
_TN = (((0,), (0,)), ((), ()))


def _cparams(*sem):
    return pltpu.CompilerParams(dimension_semantics=sem, vmem_limit_bytes=VMEM_LIMIT)


def _bdot(a, b):
    return jnp.dot(a.astype(BF16), b.astype(BF16), preferred_element_type=F32)


def _bdot_nt(a, b):
    return lax.dot_general(a.astype(BF16), b.astype(BF16), _NT, preferred_element_type=F32)


def _bdot_tn(a, b):
    return lax.dot_general(a.astype(BF16), b.astype(BF16), _TN, preferred_element_type=F32)


def _rms(x, g):
    return x * lax.rsqrt(jnp.mean(x * x, axis=-1, keepdims=True) + RMS_EPS) * g


def _norm_matmul_body(x_ref, g_ref, w_ref, o_ref, xn_ref):
    @pl.when(pl.program_id(1) == 0)
    def _():
        xn_ref[...] = _rms(x_ref[...], g_ref[...]).astype(BF16)

    o_ref[...] = jnp.dot(xn_ref[...], w_ref[...], preferred_element_type=F32).astype(o_ref.dtype)


def _proj_tiles(m, n):
    tm = min(m, 1024)
    for tn in (1408, 1024, 512):
        if n % tn == 0:
            return tm, tn
    raise ValueError(f"unsupported projection width {n}")


def norm_matmul(x, g, gl, w, wl):
    m, d = x.shape
    n = w.shape[2]
    tm, tn = _proj_tiles(m, n)
    return pl.pallas_call(
        _norm_matmul_body,
        grid=(m // tm, n // tn),
        in_specs=[pl.BlockSpec((tm, d), lambda i, j: (i, 0)),
                  pl.BlockSpec((None, 1, d), lambda i, j: (gl, 0, 0)),
                  pl.BlockSpec((None, d, tn), lambda i, j: (wl, 0, j))],
        out_specs=pl.BlockSpec((tm, tn), lambda i, j: (i, j)),
        out_shape=jax.ShapeDtypeStruct((m, n), F32),
        scratch_shapes=[pltpu.VMEM((tm, d), BF16)],
        compiler_params=_cparams("parallel", "arbitrary"),
        name="norm_matmul",
    )(x, g, w)


def norm_matmul_layers(x, g, w):
    m, d = x.shape
    nl, _, n = w.shape
    _, tn = _proj_tiles(m, n)
    return pl.pallas_call(
        _norm_matmul_body,
        grid=(nl, n // tn),
        in_specs=[pl.BlockSpec((m, d), lambda l, j: (0, 0)),
                  pl.BlockSpec((None, 1, d), lambda l, j: (l, 0, 0)),
                  pl.BlockSpec((None, d, tn), lambda l, j: (l, 0, j))],
        out_specs=pl.BlockSpec((None, m, tn), lambda l, j: (l, 0, j)),
        out_shape=jax.ShapeDtypeStruct((nl, m, n), F32),
        scratch_shapes=[pltpu.VMEM((m, d), BF16)],
        compiler_params=_cparams("parallel", "arbitrary"),
        name="norm_matmul_layers",
    )(x, g, w)


def _mix_out_body(tok_ref, mem_ref, wt_ref, wm_ref, g_ref, x_ref, o_ref):
    y = jnp.dot(tok_ref[...], wt_ref[...], preferred_element_type=F32)
    y = y + jnp.dot(mem_ref[...], wm_ref[...], preferred_element_type=F32)
    o_ref[...] = x_ref[...] + _rms(y, g_ref[...])


def mix_out(tok, mem, w_out, wl, g, gl, x):
    m, d = x.shape
    tm = min(m, 512)
    return pl.pallas_call(
        _mix_out_body,
        grid=(m // tm,),
        in_specs=[pl.BlockSpec((tm, TOK_W), lambda i: (i, 0)),
                  pl.BlockSpec((tm, MEM_W), lambda i: (i, 0)),
                  pl.BlockSpec((None, TOK_W, d), lambda i: (wl, 0, 0)),
                  pl.BlockSpec((None, MEM_W, d), lambda i: (wl, TOK_W // MEM_W, 0)),
                  pl.BlockSpec((None, 1, d), lambda i: (gl, 0, 0)),
                  pl.BlockSpec((tm, d), lambda i: (i, 0))],
        out_specs=pl.BlockSpec((tm, d), lambda i: (i, 0)),
        out_shape=jax.ShapeDtypeStruct((m, d), F32),
        compiler_params=_cparams("parallel"),
        name="mix_out",
    )(tok, mem, w_out, w_out, g, x)


def _mlp_body(x_ref, g1_ref, wu_ref, wd_ref, g2_ref, o_ref, *rest):
    xn_ref = rest[-1]
    f = pl.program_id(1)

    @pl.when(f == 0)
    def _():
        xn_ref[...] = _rms(x_ref[...], g1_ref[...]).astype(BF16)
        o_ref[...] = jnp.zeros_like(o_ref)

    wu = wu_ref[...].astype(BF16)
    wd = wd_ref[...].astype(BF16)
    if len(rest) == 3:
        rest[0][...] = wu
        rest[1][...] = wd
    h = jnp.maximum(jnp.dot(xn_ref[...], wu, preferred_element_type=F32), 0.0)
    o_ref[...] += jnp.dot((h * h).astype(BF16), wd, preferred_element_type=F32)

    @pl.when(f == pl.num_programs(1) - 1)
    def _():
        o_ref[...] = x_ref[...] + _rms(o_ref[...], g2_ref[...])


def mlp(x, g1, w_up, w_down, g2, l, wl):
    m, d = x.shape
    ff = w_up.shape[2]
    emit = w_up.dtype != BF16
    tm, tf = min(m, 1024), (512 if emit else 1024)
    once = pl.Buffered(1)
    out_specs = [pl.BlockSpec((tm, d), lambda i, f: (i, 0), pipeline_mode=once)]
    out_shape = [jax.ShapeDtypeStruct((m, d), F32)]
    if emit:
        out_specs += [pl.BlockSpec((d, tf), lambda i, f: (0, f)), pl.BlockSpec((tf, d), lambda i, f: (f, 0))]
        out_shape += [jax.ShapeDtypeStruct((d, ff), BF16), jax.ShapeDtypeStruct((ff, d), BF16)]
    outs = pl.pallas_call(
        _mlp_body,
        grid=(m // tm, ff // tf),
        in_specs=[pl.BlockSpec((tm, d), lambda i, f: (i, 0), pipeline_mode=once),
                  pl.BlockSpec((None, 1, d), lambda i, f: (l, 0, 0)),
                  pl.BlockSpec((None, d, tf), lambda i, f: (wl, 0, f)),
                  pl.BlockSpec((None, tf, d), lambda i, f: (wl, f, 0)),
                  pl.BlockSpec((None, 1, d), lambda i, f: (l, 0, 0))],
        out_specs=out_specs,
        out_shape=out_shape,
        scratch_shapes=[pltpu.VMEM((tm, d), BF16)],
        compiler_params=_cparams("parallel", "arbitrary"),
        name="mlp",
    )(x, g1, w_up, w_down, g2)
    return outs if emit else outs[0]


def _mem_attn_body(q_ref, k_ref, v_ref, o_ref):
    scale = MEM_HD ** -0.5
    nb = k_ref.shape[0]
    rq = q_ref.shape[0] // nb
    sls = [slice(h * MEM_HD, (h + 1) * MEM_HD) for h in range(MEM_HEADS)]
    rows = [slice(b * rq, (b + 1) * rq) for b in range(nb)]
    head = ((lambda ref, b, h: ref[b, :, h, :]) if len(k_ref.shape) == 4
            else (lambda ref, b, h: ref[b, :, sls[h]]))
    probs = [(b, h) for b in range(nb) for h in range(MEM_HEADS)]
    s = [_bdot_nt(q_ref[rows[b], sls[h]], head(k_ref, b, h)) * scale for b, h in probs]
    e = [jnp.exp(t - jnp.max(t, axis=-1, keepdims=True)) for t in s]
    p = [t / jnp.sum(t, axis=-1, keepdims=True) for t in e]
    o = [_bdot(p[n], head(v_ref, b, h)) for n, (b, h) in enumerate(probs)]
    for n, (b, h) in enumerate(probs):
        o_ref[rows[b], sls[h]] = o[n].astype(o_ref.dtype)


def mem_attn(proj, q_col_block, mem_k, mem_v, l, batch, rows_per_seq):
    m = proj.shape[0]
    if mem_k.ndim == 5:
        nb = _batch_block(batch, 4)
        tr = nb * rows_per_seq
        blk = (None, nb, N_MEM, MEM_HEADS, MEM_HD)
        k_spec = pl.BlockSpec(blk, lambda i: (l, i, 0, 0, 0))
        v_spec = k_spec
    else:
        tr = min(rows_per_seq, 512)
        tiles_per_seq = rows_per_seq // tr
        blk = (None, 1, N_MEM, MEM_W)
        k_spec = pl.BlockSpec(blk, lambda i: (l, i // tiles_per_seq, 0, 0))
        v_spec = pl.BlockSpec(blk, lambda i: (l, i // tiles_per_seq, 0, 1))
    return pl.pallas_call(
        _mem_attn_body,
        grid=(m // tr,),
        in_specs=[pl.BlockSpec((tr, MEM_W), lambda i: (i, q_col_block)), k_spec, v_spec],
        out_specs=pl.BlockSpec((tr, MEM_W), lambda i: (i, 0)),
        out_shape=jax.ShapeDtypeStruct((m, MEM_W), BF16),
        compiler_params=_cparams("parallel"),
        name="mem_attn",
    )(proj, mem_k, mem_v)


def _swa_core(units, q_tile, k_tiles, v_tiles, valid_of, sink_ref, o_store, lq):
    lane = lax.broadcasted_iota(jnp.int32, (lq, LANES), 1)
    low = lane < HEAD_D
    probs = [(u, j) for u in units for j in range(SWA_KV)]
    s = []
    for u, j in probs:
        xs = []
        for i in range(SWA_G // 2):
            pair = q_tile(u, (j * SWA_G + 2 * i) * HEAD_D) * (HEAD_D ** -0.5)
            xs.append(jnp.where(low, pair, 0.0).astype(BF16))
            xs.append(jnp.where(low, 0.0, pair).astype(BF16))
        x = jnp.concatenate(xs, axis=0)
        kj = jnp.concatenate([t.astype(BF16) for t in k_tiles(u, j)], axis=0)
        sj = lax.dot_general(x, kj, _NT, preferred_element_type=F32)
        valid = valid_of(u)
        s.append(sj if valid is None else jnp.where(valid, sj, NEG))
    p = []
    for n, (u, j) in enumerate(probs):
        sink = sink_ref[j]
        mx = jnp.maximum(jnp.max(s[n], axis=-1, keepdims=True), sink)
        e = jnp.exp(s[n] - mx)
        p.append((e / (jnp.sum(e, axis=-1, keepdims=True) + jnp.exp(sink - mx))).astype(BF16))
    o = [jnp.dot(p[n], jnp.concatenate([t.astype(BF16) for t in v_tiles(u, j)], axis=0),
                 preferred_element_type=F32) for n, (u, j) in enumerate(probs)]
    for n, (u, j) in enumerate(probs):
        for i in range(SWA_G // 2):
            top = o[n][(2 * i) * lq:(2 * i + 1) * lq]
            bot = o[n][(2 * i + 1) * lq:(2 * i + 2) * lq]
            o_store(u, (j * SWA_G + 2 * i) * HEAD_D, jnp.where(low, top, bot))


def _swa_prompt_body(q_ref, ka, kb, va, vb, sink_ref, o_ref):
    c2 = pl.program_id(1)
    col = lax.broadcasted_iota(jnp.int32, (SWA_G * CHUNK, 3 * CHUNK), 1)
    units = [(b, h) for b in range(q_ref.shape[0]) for h in range(2)]
    rows = lambda h: slice(h * CHUNK, (h + 1) * CHUNK)

    def tiles(ra, rb):
        def get(u, j):
            b, h = u
            ls = slice(j * LANES, (j + 1) * LANES)
            blocks = [ra[b, rows(0), ls], ra[b, rows(1), ls], rb[b, rows(0), ls], rb[b, rows(1), ls]]
            return blocks[h:h + 3]
        return get

    def store(u, c0, tile):
        o_ref[u[0], rows(u[1]), c0:c0 + LANES] = tile.astype(o_ref.dtype)

    _swa_core(units, lambda u, c0: q_ref[u[0], rows(u[1]), c0:c0 + LANES], tiles(ka, kb), tiles(va, vb),
              lambda u: col >= jnp.maximum(2 - (2 * c2 + u[1]), 0) * CHUNK,
              sink_ref, store, CHUNK)


def _batch_block(batch, want):
    return want if batch % want == 0 else 1


def swa_prompt(proj, kvd, sink_col, batch, seq):
    rows = 2 * CHUNK
    kd_w = SWA_KV * LANES
    nb = _batch_block(batch, 2)
    kv_prev = lambda col: pl.BlockSpec((nb, rows, kd_w), lambda b, c: (b, jnp.maximum(c - 1, 0), col))
    kv_same = lambda col: pl.BlockSpec((nb, rows, kd_w), lambda b, c: (b, c, col))
    kv3 = kvd.reshape(batch, seq, 2 * kd_w)
    out = pl.pallas_call(
        _swa_prompt_body,
        grid=(batch // nb, seq // rows),
        in_specs=[pl.BlockSpec((nb, rows, TOK_W), lambda b, c: (b, c, 0)),
                  kv_prev(0), kv_same(0), kv_prev(1), kv_same(1),
                  pl.BlockSpec((SWA_KV, SWA_G * CHUNK, 1), lambda b, c: (0, 0, 0))],
        out_specs=pl.BlockSpec((nb, rows, TOK_W), lambda b, c: (b, c, 0)),
        out_shape=jax.ShapeDtypeStruct((batch, seq, TOK_W), BF16),
        compiler_params=_cparams("parallel", "arbitrary"),
        name="swa_prompt",
    )(proj.reshape(batch, seq, -1), kv3, kv3, kv3, kv3, sink_col)
    return out.reshape(batch * seq, TOK_W)


def _swa_sample_body(q_ref, k_ref, v_ref, sink_ref, o_ref):
    nb, lq, _ = q_ref.shape

    def store(b, c0, tile):
        o_ref[b, :, c0:c0 + LANES] = tile.astype(o_ref.dtype)

    _swa_core(range(nb), lambda b, c0: q_ref[b, :, c0:c0 + LANES],
              lambda b, j: [k_ref[b, :, j * LANES:(j + 1) * LANES]],
              lambda b, j: [v_ref[b, :, j * LANES:(j + 1) * LANES]],
              lambda b: None, sink_ref, store, lq)


def swa_sample(proj, k_all, v_all, sink_col, batch, lq):
    keys = k_all.shape[1]
    kd_w = SWA_KV * LANES
    nb = _batch_block(batch, 4)
    out = pl.pallas_call(
        _swa_sample_body,
        grid=(batch // nb,),
        in_specs=[pl.BlockSpec((nb, lq, TOK_W), lambda b: (b, 0, 0)),
                  pl.BlockSpec((nb, keys, kd_w), lambda b: (b, 0, 0)),
                  pl.BlockSpec((nb, keys, kd_w), lambda b: (b, 0, 0)),
                  pl.BlockSpec((SWA_KV, SWA_G * lq, 1), lambda b: (0, 0, 0))],
        out_specs=pl.BlockSpec((nb, lq, TOK_W), lambda b: (b, 0, 0)),
        out_shape=jax.ShapeDtypeStruct((batch, lq, TOK_W), BF16),
        compiler_params=_cparams("parallel"),
        name="swa_sample",
    )(proj.reshape(batch, lq, -1), k_all, v_all, sink_col)
    return out.reshape(batch * lq, TOK_W)


def _pair_ones():
    r = lax.broadcasted_iota(jnp.int32, (LANES, LANES), 0)
    c = lax.broadcasted_iota(jnp.int32, (LANES, LANES), 1)
    return jnp.where((r < HEAD_D) == (c < HEAD_D), 1.0, 0.0).astype(BF16)


def _head_sum(x, ones_bd):
    return jnp.dot(x.astype(BF16), ones_bd, preferred_element_type=F32)


def _pre_stage(p_ref, prev_ref, wr, vf_ref, outs):
    has_vres = vf_ref is not None
    mu_ref, w0_ref, w2_ref, a0_ref, a2_ref, g2_ref = (wr[n] for n in ("mu", "w0", "w2", "a0", "a2", "g2"))
    kk_ref, ka_ref, rk_ref = wr["k_k"], wr["k_a"], wr["r_k"]
    if has_vres:
        v0_ref, v1_ref, v2_ref = wr["v0"], wr["v1"], wr["v2"]
    r_o, lw_o, k_o, v_o, kn_o, b_o, g_o, bonus_o = outs
    tr = p_ref.shape[0]
    row0 = lax.broadcasted_iota(jnp.int32, (tr, LANES), 0) == 0
    ones_bd = _pair_ones()

    def mixed(c0, width):
        outs_ = []
        for t in range(width // LANES):
            sl = slice(c0 + t * LANES, c0 + (t + 1) * LANES)
            p = p_ref[:, sl]
            pp = jnp.where(row0, prev_ref[:, sl], pltpu.roll(p, 1, 0))
            outs_.append(p + (pp - p) * mu_ref[:, sl])
        return outs_

    lora = mixed(LORA_OFF, LANES)[0]
    gate_in = jnp.concatenate(mixed(GATE_OFF, SHIFT_PAD - GATE_OFF), axis=1)
    w_pre = w0_ref[...] + _bdot(jnp.tanh(lora), w2_ref[...])
    lw_o[...] = -jnp.exp(-jax.nn.softplus(-w_pre) - 0.5)
    a = jax.nn.sigmoid(a0_ref[...] + _bdot(lora, a2_ref[...]))
    g_o[...] = _bdot(jax.nn.sigmoid(gate_in), g2_ref[...])

    r_t = mixed(0, TOK_W)
    k_t = mixed(TOK_W, TOK_W)
    v_t = mixed(2 * TOK_W, TOK_W)
    if has_vres:
        v_all = jnp.concatenate(v_t, axis=1)
        gate = jax.nn.sigmoid(v0_ref[...] + _bdot(_bdot(v_all, v1_ref[...]), v2_ref[...]))
        v_all = v_all + (vf_ref[...] - v_all) * gate
        v_t = [v_all[:, t * LANES:(t + 1) * LANES] for t in range(N_PAIRS)]
    for t in range(N_PAIRS):
        sl = slice(t * LANES, (t + 1) * LANES)
        kraw = k_t[t]
        kk = kraw * kk_ref[:, sl]
        kk = kk / jnp.maximum(jnp.sqrt(_head_sum(kk * kk, ones_bd)), 1e-12)
        a_t = a[:, sl]
        k = kraw * (1.0 + (a_t - 1.0) * ka_ref[:, sl])
        v = v_t[t]
        r_o[:, sl] = r_t[t]
        k_o[:, sl] = k
        v_o[:, sl] = v
        kn_o[:, sl] = kk
        b_o[:, sl] = kk * a_t
        bonus_o[:, sl] = _head_sum(r_t[t] * k * rk_ref[:, sl], ones_bd) * v


def _wkv_stage(mid, lg_ref, lb_ref, o_ref, s_ref, L):
    r_ref, lw_ref, k_ref, v_ref, kn_ref, b_ref, g_ref, bonus_ref = mid
    nch = r_ref.shape[0] // L
    W2 = 2 * L

    t_i = lax.broadcasted_iota(jnp.int32, (L, W2), 0)
    l_i = lax.broadcasted_iota(jnp.int32, (L, W2), 1)
    j_i = jnp.where(l_i >= L, l_i - L, l_i)
    strict = t_i > j_i
    incl = t_i >= j_i
    eye2 = jnp.where(t_i == j_i, 1.0, 0.0)
    levels = []
    s = 1
    while s < L:
        sh = s.bit_length() - 1
        levels.append(((t_i >> (sh + 1)) == (j_i >> (sh + 1)))
                      & (((t_i >> sh) & 1) == 1) & (((j_i >> sh) & 1) == 0))
        s *= 2
    first_t = l_i < L
    lane = lax.broadcasted_iota(jnp.int32, (L, LANES), 1)
    low = lane < HEAD_D
    rr = lax.broadcasted_iota(jnp.int32, (LANES, LANES), 0)
    cc = lax.broadcasted_iota(jnp.int32, (LANES, LANES), 1)
    bd = (rr < HEAD_D) == (cc < HEAD_D)
    ones_bd = jnp.where(bd, 1.0, 0.0).astype(BF16)
    tri = jnp.where(lax.broadcasted_iota(jnp.int32, (L, L), 0)
                    >= lax.broadcasted_iota(jnp.int32, (L, L), 1), 1.0, 0.0).astype(BF16)

    def cumsum_rows(x):
        hi = x.astype(BF16)
        r1 = x - hi.astype(F32)
        mid = r1.astype(BF16)
        lo = (r1 - mid.astype(F32)).astype(BF16)
        return (jnp.dot(tri, hi, preferred_element_type=F32) + jnp.dot(tri, mid, preferred_element_type=F32)
                + jnp.dot(tri, lo, preferred_element_type=F32))

    def stack_t(x):
        return jnp.concatenate([jnp.where(first_t, x, 0.0), jnp.where(first_t, 0.0, x)],
                               axis=0).astype(BF16)

    def stack_d(x):
        return jnp.concatenate([jnp.where(low, x, 0.0), jnp.where(low, 0.0, x)], axis=0).astype(BF16)

    def tdot(m, stacked):
        return jnp.dot(m.astype(BF16), stacked, preferred_element_type=F32)

    cum_all = [cumsum_rows(lw_ref[ch * L:(ch + 1) * L, :]) for ch in range(nch)]

    probs = [(ch, hp) for ch in range(nch) for hp in range(N_PAIRS)]
    pairs = range(len(probs))
    rows = [slice(ch * L, (ch + 1) * L) for ch, _ in probs]
    sls = [slice(hp * LANES, (hp + 1) * LANES) for _, hp in probs]
    a_t, r_t, sv, bh, kh, dec, cb, ck = [], [], [], [], [], [], [], []
    for p in pairs:
        rw, sl = rows[p], sls[p]
        lw = lw_ref[rw, sl]
        cum = cum_all[probs[p][0]][:, sl]
        cum_l = cum[L - 1:L, :]
        p_inv = jnp.exp(-cum)
        p_end = jnp.exp(cum_l - cum)
        b = b_ref[rw, sl]
        k = k_ref[rw, sl]
        a_t.append(-kn_ref[rw, sl] * jnp.exp(cum - lw))
        r_t.append(r_ref[rw, sl] * jnp.exp(cum))
        sv.append(stack_d(v_ref[rw, sl]))
        bh.append((b * p_end).astype(BF16))
        kh.append((k * p_end).astype(BF16))
        dec.append(jnp.exp(cum_l))
        x = jnp.concatenate([a_t[p], r_t[p]], axis=0).astype(BF16)
        cb.append(lax.dot_general(x, stack_d(b * p_inv), _NT, preferred_element_type=F32))
        ck.append(lax.dot_general(x, stack_d(k * p_inv), _NT, preferred_element_type=F32))
    a_ab = [jnp.where(strict, cb[p][:L], 0.0) for p in pairs]
    a_rb = [jnp.where(incl, cb[p][L:], 0.0).astype(BF16) for p in pairs]
    a_k = [jnp.concatenate([jnp.where(strict, ck[p][:L], 0.0), jnp.where(incl, ck[p][L:], 0.0)],
                           axis=0).astype(BF16) for p in pairs]
    kv = [jnp.dot(a_k[p], sv[p], preferred_element_type=F32) for p in pairs]
    t_m = [eye2 + jnp.where(levels[0], a_ab[p], 0.0) for p in pairs]
    for msk in levels[1:]:
        ta = [tdot(t_m[p], stack_t(jnp.where(msk, a_ab[p], 0.0))) for p in pairs]
        t_m = [t_m[p] + tdot(ta[p], stack_t(t_m[p])) for p in pairs]
    t_b = [t_m[p].astype(BF16) for p in pairs]
    w12 = [jnp.dot(t_b[p], jnp.concatenate([stack_d(a_t[p]), stack_d(kv[p][:L])], axis=1),
                   preferred_element_type=F32) for p in pairs]
    qq = [jnp.dot(a_rb[p], jnp.concatenate([stack_d(w12[p][:, :LANES]), stack_d(w12[p][:, LANES:])],
                                           axis=1), preferred_element_type=F32) for p in pairs]
    mm = [_bdot_tn(w12[p], bh[p]) for p in pairs]
    mk = [_bdot_tn(v_ref[rows[p], sls[p]], kh[p]) for p in pairs]
    y = []
    for p in pairs:
        hp = probs[p][1]
        s_p = s_ref[hp]
        q1 = r_t[p] + qq[p][:, :LANES]
        q2 = qq[p][:, LANES:] + kv[p][L:]
        m1 = jnp.where(bd, mm[p][:LANES], 0.0)
        m2 = jnp.where(bd, mm[p][LANES:] + mk[p], 0.0)
        y.append(_bdot_nt(q1, s_p) + q2)
        s_ref[hp] = s_p * dec[p] + _bdot(s_p, m1) + m2
    mean = [_head_sum(y[p], ones_bd) * (1.0 / HEAD_D) for p in pairs]
    dv = [y[p] - mean[p] for p in pairs]
    var = [_head_sum(dv[p] * dv[p], ones_bd) * (1.0 / HEAD_D) for p in pairs]
    for p in pairs:
        rw, sl = rows[p], sls[p]
        yn = dv[p] * lax.rsqrt(var[p] + GN_EPS) * lg_ref[:, sl] + lb_ref[:, sl]
        o_ref[rw, sl] = ((yn + bonus_ref[rw, sl]) * g_ref[rw, sl]).astype(o_ref.dtype)


_PRE_W = ("mu", "w0", "w2", "a0", "a2", "g2", "k_k", "k_a", "r_k")
_VRES_W = ("v0", "v1", "v2")
_MID = 8


def _rwkv_body(*refs, chunk, has_vres):
    names = _PRE_W + (_VRES_W if has_vres else ())
    it = iter(refs)
    p_ref, shift_ref = next(it), next(it)
    vf_ref = next(it) if has_vres else None
    wr = {n: next(it) for n in names}
    s0_ref, lg_ref, lb_ref = next(it), next(it), next(it)
    o_ref = next(it)
    v_out_ref = None if has_vres else next(it)
    s_out_ref, s_ref, prev_ref = next(it), next(it), next(it)
    mid = [next(it) for _ in range(_MID)]
    c = pl.program_id(1)

    @pl.when(c == 0)
    def _():
        zero = jnp.zeros((HEAD_D, HEAD_D), F32)
        for hp in range(N_PAIRS):
            top = jnp.concatenate([s0_ref[2 * hp], zero], axis=1)
            bot = jnp.concatenate([zero, s0_ref[2 * hp + 1]], axis=1)
            s_ref[hp] = jnp.concatenate([top, bot], axis=0)
        prev_ref[...] = shift_ref[...]

    _pre_stage(p_ref, prev_ref, wr, vf_ref, mid)
    prev_ref[...] = p_ref[p_ref.shape[0] - 1:, :]
    if v_out_ref is not None:
        v_out_ref[...] = mid[3][...]
    _wkv_stage(mid, lg_ref, lb_ref, o_ref, s_ref, chunk)

    @pl.when(c == pl.num_programs(1) - 1)
    def _():
        for hp in range(N_PAIRS):
            s_p = s_ref[hp]
            s_out_ref[2 * hp] = s_p[:HEAD_D, :HEAD_D]
            s_out_ref[2 * hp + 1] = s_p[HEAD_D:, HEAD_D:]


def rwkv_mix(proj, shift0, pw, v_first, s0, batch, seq, chunk, rows):
    steps = seq // rows
    has_vres = v_first is not None
    names = _PRE_W + (_VRES_W if has_vres else ())
    full = lambda a: pl.BlockSpec(a.shape, lambda bi, c: (0,) * a.ndim)
    tile = pl.BlockSpec((rows, TOK_W), lambda bi, c: (bi * steps + c, 0))
    state = pl.BlockSpec((None, N_HEADS, HEAD_D, HEAD_D), lambda bi, c: (bi, 0, 0, 0))
    in_specs = [pl.BlockSpec((rows, SHIFT_PAD), lambda bi, c: (bi * steps + c, 0)),
                pl.BlockSpec((None, 1, SHIFT_PAD), lambda bi, c: (bi, 0, 0))]
    args = [proj, shift0]
    if has_vres:
        in_specs.append(tile)
        args.append(v_first)
    in_specs += [full(pw[n]) for n in names] + [state, full(pw["lnx_g"]), full(pw["lnx_b"])]
    args += [pw[n] for n in names] + [s0, pw["lnx_g"], pw["lnx_b"]]
    out_specs = [tile] + ([] if has_vres else [tile]) + [state]
    out_shape = ([jax.ShapeDtypeStruct((batch * seq, TOK_W), BF16)]
                 + ([] if has_vres else [jax.ShapeDtypeStruct((batch * seq, TOK_W), F32)])
                 + [jax.ShapeDtypeStruct((batch, N_HEADS, HEAD_D, HEAD_D), F32)])
    outs = pl.pallas_call(
        functools.partial(_rwkv_body, chunk=chunk, has_vres=has_vres),
        grid=(batch, steps),
        in_specs=in_specs,
        out_specs=out_specs,
        out_shape=out_shape,
        scratch_shapes=[pltpu.VMEM((N_PAIRS, LANES, LANES), F32), pltpu.VMEM((1, SHIFT_PAD), F32)]
                       + [pltpu.VMEM((rows, TOK_W), F32)] * _MID,
        compiler_params=_cparams("parallel", "arbitrary"),
        name="rwkv_mix",
    )(*args)
    if has_vres:
        return outs[0], None, outs[1]
    return outs


def _dup_heads(w):
    lead = w.shape[:-1]
    w = w.reshape(lead + (SWA_KV, 1, HEAD_D))
    return jnp.broadcast_to(w, lead + (SWA_KV, 2, HEAD_D)).reshape(lead + (SWA_KV * LANES,))


def _undup_heads(x):
    lead = x.shape[:-1]
    return x.reshape(lead + (SWA_KV, 2, HEAD_D))[..., 0, :]


def _pad_cols(a, width):
    return jnp.pad(a, [(0, 0)] * (a.ndim - 1) + [(0, width - a.shape[-1])])


def _sink_col(sinks, lq):
    return jnp.repeat(sinks.astype(F32).reshape(SWA_KV, SWA_G), lq, axis=1)[..., None]


def _trunk(x, mem_k, mem_v, wkv0, shift0, past_k, past_v, P, batch, seq, mlp_bf16):
    chunk = min(CHUNK, seq)
    rows_mix = min(2 * chunk, seq)
    new_wkv, new_shift = [], []
    v_first = None
    kvd = None
    for l in range(DEPTH):
        if l < N_A:
            pw = P["a"][l]
            proj = norm_matmul(x, P["ln_mix_pre"], l, pw["w_in"], 0)
            new_shift.append(proj.reshape(batch, seq, A_IN_PAD)[:, -1:, :A_SHIFT])
            tok, v, s_new = rwkv_mix(proj, _pad_cols(shift0[l].astype(F32), SHIFT_PAD), pw, v_first,
                                     wkv0[l].astype(F32), batch, seq, chunk, rows_mix)
            if l == 0:
                v_first = v
            new_wkv.append(s_new)
            q_col = A_QMEM_OFF // MEM_W
            w_out, wi = P["a_w_out"], l
        else:
            i = l - N_A
            if kvd is None:
                kvd = norm_matmul(x, P["kv_norm_g"], 0, P["w_kvd"], 0)
            proj = norm_matmul(x, P["ln_mix_pre"], l, P["b_w_in"], i)
            if past_k is None:
                tok = swa_prompt(proj, kvd, _sink_col(P["b_sinks"][i], CHUNK), batch, seq)
            else:
                kd_w = SWA_KV * LANES
                k_all = jnp.concatenate([_dup_heads(past_k.reshape(batch, -1, SWA_KV * HEAD_D)),
                                         kvd[:, :kd_w].reshape(batch, seq, kd_w)], axis=1)
                v_all = jnp.concatenate([_dup_heads(past_v.reshape(batch, -1, SWA_KV * HEAD_D)),
                                         kvd[:, kd_w:].reshape(batch, seq, kd_w)], axis=1)
                tok = swa_sample(proj, k_all, v_all, _sink_col(P["b_sinks"][i], seq), batch, seq)
            q_col = TOK_W // MEM_W
            w_out, wi = P["b_w_out"], i
        mem_o = mem_attn(proj, q_col, mem_k, mem_v, l, batch, seq)
        x = mix_out(tok, mem_o, w_out, wi, P["ln_mix_post"], l, x)
        if l in mlp_bf16:
            wu, wd = mlp_bf16[l]
            x = mlp(x, P["ln_mlp_pre"], wu, wd, P["ln_mlp_post"], l, 0)
        else:
            x, wu, wd = mlp(x, P["ln_mlp_pre"], P["w_up"], P["w_down"], P["ln_mlp_post"], l, l)
            mlp_bf16[l] = (wu[None], wd[None])
    kd_w = SWA_KV * LANES
    k_new = _undup_heads(kvd[:, :kd_w]).reshape(batch, seq, SWA_KV, HEAD_D)
    v_new = _undup_heads(kvd[:, kd_w:]).reshape(batch, seq, SWA_KV, HEAD_D)
    return x, jnp.stack(new_wkv), jnp.stack(new_shift), k_new, v_new


def _prep_params(ln_mix_pre, ln_mix_post, ln_mlp_pre, ln_mlp_post, mem_norm_g, w_mem_kv, w_up, w_down,
                 a_w_in, a_mu, a_w0, a_w2, a_a0, a_a2, a_g2, a_k_k, a_k_a, a_r_k, a_lnx_g, a_lnx_b,
                 a_v0, a_v1, a_v2, a_w_out, kv_norm_g, w_kv, b_w_in, b_sinks, b_w_out):
    d = D_MODEL
    zrow = lambda n: jnp.zeros((n, TOK_W), F32)
    a_params = []
    for i in range(N_A):
        w_in = a_w_in[i]
        w_in = jnp.concatenate([w_in[:, :A_SHIFT], jnp.zeros((d, A_QMEM_OFF - A_SHIFT), F32),
                                w_in[:, A_SHIFT:]], axis=1).astype(BF16)
        pw = {
            "w_in": w_in[None],
            "mu": _pad_cols(a_mu[i].reshape(1, A_SHIFT), SHIFT_PAD),
            "w0": a_w0[i].reshape(1, TOK_W), "a0": a_a0[i].reshape(1, TOK_W),
            "w2": jnp.concatenate([a_w2[i], zrow(A_LORA)], axis=0).astype(BF16),
            "a2": jnp.concatenate([zrow(W_LORA), a_a2[i]], axis=0).astype(BF16),
            "g2": jnp.concatenate([a_g2[i], zrow(SHIFT_PAD - A_SHIFT)], axis=0).astype(BF16),
            "k_k": a_k_k[i].reshape(1, TOK_W), "k_a": a_k_a[i].reshape(1, TOK_W),
            "r_k": a_r_k[i].reshape(1, TOK_W),
            "lnx_g": a_lnx_g[i].reshape(1, TOK_W), "lnx_b": a_lnx_b[i].reshape(1, TOK_W),
        }
        if i > 0:
            pw["v0"] = a_v0[i - 1].reshape(1, TOK_W)
            pw["v1"] = a_v1[i - 1].astype(BF16)
            pw["v2"] = a_v2[i - 1].astype(BF16)
        a_params.append(pw)
    kw = SWA_KV * HEAD_D
    gain = lambda g: g.reshape(-1, 1, d)
    P = {
        "a": a_params,
        "ln_mix_pre": gain(ln_mix_pre), "ln_mix_post": gain(ln_mix_post), "ln_mlp_pre": gain(ln_mlp_pre),
        "ln_mlp_post": gain(ln_mlp_post), "mem_norm_g": gain(mem_norm_g), "kv_norm_g": gain(kv_norm_g),
        "w_up": w_up, "w_down": w_down, "w_mem_kv": w_mem_kv.astype(BF16),
        "w_kvd": jnp.concatenate([_dup_heads(w_kv[:, :kw]), _dup_heads(w_kv[:, kw:])],
                                 axis=1).astype(BF16)[None],
        "a_w_out": a_w_out.astype(BF16),
        "b_w_in": b_w_in.astype(BF16), "b_sinks": b_sinks, "b_w_out": b_w_out.astype(BF16),
    }
    return P


def kernel(x_prompt, x_sample, mem_prompt, state_wkv, state_shift, cache_win_k, cache_win_v, cache_mem_k, cache_mem_v, ln_mix_pre, ln_mix_post, ln_mlp_pre, ln_mlp_post, mem_norm_g, w_mem_kv, w_up, w_down, a_w_in, a_mu, a_w0, a_w2, a_a0, a_a2, a_g2, a_k_k, a_k_a, a_r_k, a_lnx_g, a_lnx_b, a_v0, a_v1, a_v2, a_w_out, kv_norm_g, w_kv, b_w_in, b_sinks, b_w_out):
    bp, sp, d = x_prompt.shape
    bs, ss, _ = x_sample.shape
    P = _prep_params(ln_mix_pre, ln_mix_post, ln_mlp_pre, ln_mlp_post, mem_norm_g, w_mem_kv, w_up, w_down,
                     a_w_in, a_mu, a_w0, a_w2, a_a0, a_a2, a_g2, a_k_k, a_k_a, a_r_k, a_lnx_g, a_lnx_b,
                     a_v0, a_v1, a_v2, a_w_out, kv_norm_g, w_kv, b_w_in, b_sinks, b_w_out)

    mlp_bf16 = {}
    y_s, wkv_s, shift_s, k_s, v_s = _trunk(
        x_sample.reshape(bs * ss, d), cache_mem_k, cache_mem_v, state_wkv, state_shift, cache_win_k,
        cache_win_v, P, bs, ss, mlp_bf16)
    past = cache_win_k.shape[1]
    win_k_s = jnp.concatenate([cache_win_k, k_s], axis=1)[:, -past:]
    win_v_s = jnp.concatenate([cache_win_v, v_s], axis=1)[:, -past:]

    mkv = norm_matmul_layers(mem_prompt.reshape(bp * N_MEM, d), P["mem_norm_g"], P["w_mem_kv"])
    mkv4 = mkv.reshape(DEPTH, bp, N_MEM, 2 * MEM_W)
    wkv0 = jnp.zeros((N_A, bp, N_HEADS, HEAD_D, HEAD_D), F32)
    shift0 = jnp.zeros((N_A, bp, 1, A_SHIFT), F32)
    y_p, wkv_p, shift_p, k_p, v_p = _trunk(
        x_prompt.reshape(bp * sp, d), mkv4, mkv4, wkv0, shift0, None, None, P, bp, sp, mlp_bf16)

    return (y_p.reshape(bp, sp, d), y_s.reshape(bs, ss, d), wkv_p, shift_p,
            k_p[:, -WINDOW:], v_p[:, -WINDOW:],
            mkv[:, :, :MEM_W].reshape(DEPTH, bp, N_MEM, MEM_HEADS, MEM_HD),
            mkv[:, :, MEM_W:].reshape(DEPTH, bp, N_MEM, MEM_HEADS, MEM_HD),
            wkv_s, shift_s, win_k_s, win_v_s)
```

```python
import functools

import jax
import jax.numpy as jnp
from jax import lax
from jax.experimental import pallas as pl
from jax.experimental.pallas import tpu as pltpu

F32 = jnp.float32
BF16 = jnp.bfloat16

D_MODEL = 2048
DEPTH = 4
N_A = 2
CHUNK = 64
N_MEM = 256
MEM_HEADS = 4
MEM_HD = 128
MEM_W = MEM_HEADS * MEM_HD
TOK_W = D_MODEL - MEM_W
HEAD_D = 64
N_HEADS = TOK_W // HEAD_D
N_PAIRS = N_HEADS // 2
W_LORA = 64
A_LORA = 64
G_LORA = 224
A_SHIFT = 3 * TOK_W + W_LORA + A_LORA + G_LORA
LORA_OFF = 3 * TOK_W
GATE_OFF = LORA_OFF + 128
SHIFT_PAD = 4992
A_QMEM_OFF = 5120
A_IN_PAD = A_QMEM_OFF + MEM_W
SWA_KV = 4
SWA_G = N_HEADS // SWA_KV
WINDOW = 128
D_FF = 4 * D_MODEL
RMS_EPS = 1e-6
GN_EPS = 64e-5
NEG = -1e30
LANES = 128
VMEM_LIMIT = 56 * 1024 * 1024

_NT = (((1,), (1,)), ((), ()))
_TN = (((0,), (0,)), ((), ()))


def _cparams(*sem):
    return pltpu.CompilerParams(dimension_semantics=sem, vmem_limit_bytes=VMEM_LIMIT)


def _bdot(a, b):
    return jnp.dot(a.astype(BF16), b.astype(BF16), preferred_element_type=F32)


def _bdot_nt(a, b):
    return lax.dot_general(a.astype(BF16), b.astype(BF16), _NT, preferred_element_type=F32)


def _bdot_tn(a, b):
    return lax.dot_general(a.astype(BF16), b.astype(BF16), _TN, preferred_element_type=F32)


def _rms(x, g):
    return x * lax.rsqrt(jnp.mean(x * x, axis=-1, keepdims=True) + RMS_EPS) * g


def _norm_matmul_body(x_ref, g_ref, w_ref, o_ref, xn_ref):
    @pl.when(pl.program_id(1) == 0)
    def _():
        xn_ref[...] = _rms(x_ref[...], g_ref[...]).astype(BF16)

    o_ref[...] = jnp.dot(xn_ref[...], w_ref[...], preferred_element_type=F32).astype(o_ref.dtype)


def _proj_tiles(m, n):
    tm = min(m, 1024)
    for tn in (1408, 1024, 512):
        if n % tn == 0:
            return tm, tn
    raise ValueError(f"unsupported projection width {n}")


def norm_matmul(x, g, gl, w, wl):
    m, d = x.shape
    n = w.shape[2]
    tm, tn = _proj_tiles(m, n)
    return pl.pallas_call(
        _norm_matmul_body,
        grid=(m // tm, n // tn),
        in_specs=[pl.BlockSpec((tm, d), lambda i, j: (i, 0)),
                  pl.BlockSpec((None, 1, d), lambda i, j: (gl, 0, 0)),
                  pl.BlockSpec((None, d, tn), lambda i, j: (wl, 0, j))],
        out_specs=pl.BlockSpec((tm, tn), lambda i, j: (i, j)),
        out_shape=jax.ShapeDtypeStruct((m, n), F32),
        scratch_shapes=[pltpu.VMEM((tm, d), BF16)],
        compiler_params=_cparams("parallel", "arbitrary"),
        name="norm_matmul",
    )(x, g, w)


def norm_matmul_layers(x, g, w):
    m, d = x.shape
    nl, _, n = w.shape
    _, tn = _proj_tiles(m, n)
    return pl.pallas_call(
        _norm_matmul_body,
        grid=(nl, n // tn),
        in_specs=[pl.BlockSpec((m, d), lambda l, j: (0, 0)),
                  pl.BlockSpec((None, 1, d), lambda l, j: (l, 0, 0)),
                  pl.BlockSpec((None, d, tn), lambda l, j: (l, 0, j))],
        out_specs=pl.BlockSpec((None, m, tn), lambda l, j: (l, 0, j)),
        out_shape=jax.ShapeDtypeStruct((nl, m, n), F32),
        scratch_shapes=[pltpu.VMEM((m, d), BF16)],
        compiler_params=_cparams("parallel", "arbitrary"),
        name="norm_matmul_layers",
    )(x, g, w)


def _mix_out_body(tok_ref, mem_ref, wt_ref, wm_ref, g_ref, x_ref, o_ref):
    y = jnp.dot(tok_ref[...], wt_ref[...], preferred_element_type=F32)
    y = y + jnp.dot(mem_ref[...], wm_ref[...], preferred_element_type=F32)
    o_ref[...] = x_ref[...] + _rms(y, g_ref[...])


def mix_out(tok, mem, w_out, wl, g, gl, x):
    m, d = x.shape
    tm = min(m, 512)
    return pl.pallas_call(
        _mix_out_body,
        grid=(m // tm,),
        in_specs=[pl.BlockSpec((tm, TOK_W), lambda i: (i, 0)),
                  pl.BlockSpec((tm, MEM_W), lambda i: (i, 0)),
                  pl.BlockSpec((None, TOK_W, d), lambda i: (wl, 0, 0)),
                  pl.BlockSpec((None, MEM_W, d), lambda i: (wl, TOK_W // MEM_W, 0)),
                  pl.BlockSpec((None, 1, d), lambda i: (gl, 0, 0)),
                  pl.BlockSpec((tm, d), lambda i: (i, 0))],
        out_specs=pl.BlockSpec((tm, d), lambda i: (i, 0)),
        out_shape=jax.ShapeDtypeStruct((m, d), F32),
        compiler_params=_cparams("parallel"),
        name="mix_out",
    )(tok, mem, w_out, w_out, g, x)


def _mlp_body(x_ref, g1_ref, wu_ref, wd_ref, g2_ref, o_ref, *rest):
    xn_ref = rest[-1]
    f = pl.program_id(1)

    @pl.when(f == 0)
    def _():
        xn_ref[...] = _rms(x_ref[...], g1_ref[...]).astype(BF16)
        o_ref[...] = jnp.zeros_like(o_ref)

    wu = wu_ref[...].astype(BF16)
    wd = wd_ref[...].astype(BF16)
    if len(rest) == 3:
        rest[0][...] = wu
        rest[1][...] = wd
    h = jnp.maximum(jnp.dot(xn_ref[...], wu, preferred_element_type=F32), 0.0)
    o_ref[...] += jnp.dot((h * h).astype(BF16), wd, preferred_element_type=F32)

    @pl.when(f == pl.num_programs(1) - 1)
    def _():
        o_ref[...] = x_ref[...] + _rms(o_ref[...], g2_ref[...])


def mlp(x, g1, w_up, w_down, g2, l, wl):
    m, d = x.shape
    ff = w_up.shape[2]
    emit = w_up.dtype != BF16
    tm, tf = (min(m, 1024), 512) if emit else (min(m, 512), 1024)
    rows_mode = pl.Buffered(1) if emit else None
    out_specs = [pl.BlockSpec((tm, d), lambda i, f: (i, 0), pipeline_mode=rows_mode)]
    out_shape = [jax.ShapeDtypeStruct((m, d), F32)]
    if emit:
        out_specs += [pl.BlockSpec((d, tf), lambda i, f: (0, f)), pl.BlockSpec((tf, d), lambda i, f: (f, 0))]
        out_shape += [jax.ShapeDtypeStruct((d, ff), BF16), jax.ShapeDtypeStruct((ff, d), BF16)]
    outs = pl.pallas_call(
        _mlp_body,
        grid=(m // tm, ff // tf),
        in_specs=[pl.BlockSpec((tm, d), lambda i, f: (i, 0), pipeline_mode=rows_mode),
                  pl.BlockSpec((None, 1, d), lambda i, f: (l, 0, 0)),
                  pl.BlockSpec((None, d, tf), lambda i, f: (wl, 0, f)),
                  pl.BlockSpec((None, tf, d), lambda i, f: (wl, f, 0)),
                  pl.BlockSpec((None, 1, d), lambda i, f: (l, 0, 0))],
        out_specs=out_specs,
        out_shape=out_shape,
        scratch_shapes=[pltpu.VMEM((tm, d), BF16)],
        compiler_params=_cparams("parallel", "arbitrary"),
        name="mlp",
    )(x, g1, w_up, w_down, g2)
    return outs if emit else outs[0]


def _mem_attn_body(q_ref, k_ref, v_ref, o_ref):
    scale = MEM_HD ** -0.5
    nb = k_ref.shape[0]
    rq = q_ref.shape[0] // nb
    sls = [slice(h * MEM_HD, (h + 1) * MEM_HD) for h in range(MEM_HEADS)]
    rows = [slice(b * rq, (b + 1) * rq) for b in range(nb)]
    head = ((lambda ref, b, h: ref[b, :, h, :]) if len(k_ref.shape) == 4
            else (lambda ref, b, h: ref[b, :, sls[h]]))
    probs = [(b, h) for b in range(nb) for h in range(MEM_HEADS)]
    s = [_bdot_nt(q_ref[rows[b], sls[h]], head(k_ref, b, h)) * scale for b, h in probs]
    e = [jnp.exp(t - jnp.max(t, axis=-1, keepdims=True)) for t in s]
    p = [t / jnp.sum(t, axis=-1, keepdims=True) for t in e]
    o = [_bdot(p[n], head(v_ref, b, h)) for n, (b, h) in enumerate(probs)]
    for n, (b, h) in enumerate(probs):
        o_ref[rows[b], sls[h]] = o[n].astype(o_ref.dtype)


def mem_attn(proj, q_col_block, mem_k, mem_v, l, batch, rows_per_seq):
    m = proj.shape[0]
    if mem_k.ndim == 5:
        nb = _batch_block(batch, 4)
        tr = nb * rows_per_seq
        blk = (None, nb, N_MEM, MEM_HEADS, MEM_HD)
        k_spec = pl.BlockSpec(blk, lambda i: (l, i, 0, 0, 0))
        v_spec = k_spec
    else:
        tr = min(rows_per_seq, 512)
        tiles_per_seq = rows_per_seq // tr
        blk = (None, 1, N_MEM, MEM_W)
        k_spec = pl.BlockSpec(blk, lambda i: (l, i // tiles_per_seq, 0, 0))
        v_spec = pl.BlockSpec(blk, lambda i: (l, i // tiles_per_seq, 0, 1))
    return pl.pallas_call(
        _mem_attn_body,
        grid=(m // tr,),
        in_specs=[pl.BlockSpec((tr, MEM_W), lambda i: (i, q_col_block)), k_spec, v_spec],
        out_specs=pl.BlockSpec((tr, MEM_W), lambda i: (i, 0)),
        out_shape=jax.ShapeDtypeStruct((m, MEM_W), BF16),
        compiler_params=_cparams("parallel"),
        name="mem_attn",
    )(proj, mem_k, mem_v)


def _swa_core(units, q_tile, k_tiles, v_tiles, valid_of, sink_ref, o_store, lq):
    lane = lax.broadcasted_iota(jnp.int32, (lq, LANES), 1)
    low = lane < HEAD_D
    probs = [(u, j) for u in units for j in range(SWA_KV)]
    s = []
    for u, j in probs:
        xs = []
        for i in range(SWA_G // 2):
            pair = q_tile(u, (j * SWA_G + 2 * i) * HEAD_D) * (HEAD_D ** -0.5)
            xs.append(jnp.where(low, pair, 0.0).astype(BF16))
            xs.append(jnp.where(low, 0.0, pair).astype(BF16))
        x = jnp.concatenate(xs, axis=0)
        kj = jnp.concatenate([t.astype(BF16) for t in k_tiles(u, j)], axis=0)
        sj = lax.dot_general(x, kj, _NT, preferred_element_type=F32)
        valid = valid_of(u)
        s.append(sj if valid is None else jnp.where(valid, sj, NEG))
    p = []
    for n, (u, j) in enumerate(probs):
        sink = sink_ref[j]
        mx = jnp.maximum(jnp.max(s[n], axis=-1, keepdims=True), sink)
        e = jnp.exp(s[n] - mx)
        p.append((e / (jnp.sum(e, axis=-1, keepdims=True) + jnp.exp(sink - mx))).astype(BF16))
    o = [jnp.dot(p[n], jnp.concatenate([t.astype(BF16) for t in v_tiles(u, j)], axis=0),
                 preferred_element_type=F32) for n, (u, j) in enumerate(probs)]
    for n, (u, j) in enumerate(probs):
        for i in range(SWA_G // 2):
            top = o[n][(2 * i) * lq:(2 * i + 1) * lq]
            bot = o[n][(2 * i + 1) * lq:(2 * i + 2) * lq]
            o_store(u, (j * SWA_G + 2 * i) * HEAD_D, jnp.where(low, top, bot))


def _swa_prompt_body(q_ref, ka, kb, va, vb, sink_ref, o_ref):
    c2 = pl.program_id(1)
    col = lax.broadcasted_iota(jnp.int32, (SWA_G * CHUNK, 3 * CHUNK), 1)
    units = [(b, h) for b in range(q_ref.shape[0]) for h in range(2)]
    rows = lambda h: slice(h * CHUNK, (h + 1) * CHUNK)

    def tiles(ra, rb):
        def get(u, j):
            b, h = u
            ls = slice(j * LANES, (j + 1) * LANES)
            blocks = [ra[b, rows(0), ls], ra[b, rows(1), ls], rb[b, rows(0), ls], rb[b, rows(1), ls]]
            return blocks[h:h + 3]
        return get

    def store(u, c0, tile):
        o_ref[u[0], rows(u[1]), c0:c0 + LANES] = tile.astype(o_ref.dtype)

    _swa_core(units, lambda u, c0: q_ref[u[0], rows(u[1]), c0:c0 + LANES], tiles(ka, kb), tiles(va, vb),
              lambda u: col >= jnp.maximum(2 - (2 * c2 + u[1]), 0) * CHUNK,
              sink_ref, store, CHUNK)


def _batch_block(batch, want):
    return want if batch % want == 0 else 1


def swa_prompt(proj, kvd, sink_col, batch, seq):
    rows = 2 * CHUNK
    kd_w = SWA_KV * LANES
    nb = _batch_block(batch, 2)
    kv_prev = lambda col: pl.BlockSpec((nb, rows, kd_w), lambda b, c: (b, jnp.maximum(c - 1, 0), col))
    kv_same = lambda col: pl.BlockSpec((nb, rows, kd_w), lambda b, c: (b, c, col))
    kv3 = kvd.reshape(batch, seq, 2 * kd_w)
    out = pl.pallas_call(
        _swa_prompt_body,
        grid=(batch // nb, seq // rows),
        in_specs=[pl.BlockSpec((nb, rows, TOK_W), lambda b, c: (b, c, 0)),
                  kv_prev(0), kv_same(0), kv_prev(1), kv_same(1),
                  pl.BlockSpec((SWA_KV, SWA_G * CHUNK, 1), lambda b, c: (0, 0, 0))],
        out_specs=pl.BlockSpec((nb, rows, TOK_W), lambda b, c: (b, c, 0)),
        out_shape=jax.ShapeDtypeStruct((batch, seq, TOK_W), BF16),
        compiler_params=_cparams("parallel", "arbitrary"),
        name="swa_prompt",
    )(proj.reshape(batch, seq, -1), kv3, kv3, kv3, kv3, sink_col)
    return out.reshape(batch * seq, TOK_W)


def _swa_sample_body(q_ref, k_ref, v_ref, sink_ref, o_ref):
    nb, lq, _ = q_ref.shape

    def store(b, c0, tile):
        o_ref[b, :, c0:c0 + LANES] = tile.astype(o_ref.dtype)

    _swa_core(range(nb), lambda b, c0: q_ref[b, :, c0:c0 + LANES],
              lambda b, j: [k_ref[b, :, j * LANES:(j + 1) * LANES]],
              lambda b, j: [v_ref[b, :, j * LANES:(j + 1) * LANES]],
              lambda b: None, sink_ref, store, lq)


def swa_sample(proj, k_all, v_all, sink_col, batch, lq):
    keys = k_all.shape[1]
    kd_w = SWA_KV * LANES
    nb = _batch_block(batch, 4)
    out = pl.pallas_call(
        _swa_sample_body,
        grid=(batch // nb,),
        in_specs=[pl.BlockSpec((nb, lq, TOK_W), lambda b: (b, 0, 0)),
                  pl.BlockSpec((nb, keys, kd_w), lambda b: (b, 0, 0)),
                  pl.BlockSpec((nb, keys, kd_w), lambda b: (b, 0, 0)),
                  pl.BlockSpec((SWA_KV, SWA_G * lq, 1), lambda b: (0, 0, 0))],
        out_specs=pl.BlockSpec((nb, lq, TOK_W), lambda b: (b, 0, 0)),
        out_shape=jax.ShapeDtypeStruct((batch, lq, TOK_W), BF16),
        compiler_params=_cparams("parallel"),
        name="swa_sample",
    )(proj.reshape(batch, lq, -1), k_all, v_all, sink_col)
    return out.reshape(batch * lq, TOK_W)


def _pair_ones():
    r = lax.broadcasted_iota(jnp.int32, (LANES, LANES), 0)
    c = lax.broadcasted_iota(jnp.int32, (LANES, LANES), 1)
    return jnp.where((r < HEAD_D) == (c < HEAD_D), 1.0, 0.0).astype(BF16)


def _head_sum(x, ones_bd):
    return jnp.dot(x.astype(BF16), ones_bd, preferred_element_type=F32)


def _pre_stage(p_ref, prev_ref, wr, vf_ref, outs):
    has_vres = vf_ref is not None
    mu_ref, w0_ref, w2_ref, a0_ref, a2_ref, g2_ref = (wr[n] for n in ("mu", "w0", "w2", "a0", "a2", "g2"))
    kk_ref, ka_ref, rk_ref = wr["k_k"], wr["k_a"], wr["r_k"]
    if has_vres:
        v0_ref, v1_ref, v2_ref = wr["v0"], wr["v1"], wr["v2"]
    r_o, lw_o, k_o, v_o, kn_o, b_o, g_o, bonus_o = outs
    tr = p_ref.shape[0]
    row0 = lax.broadcasted_iota(jnp.int32, (tr, LANES), 0) == 0
    ones_bd = _pair_ones()

    def mixed(c0, width):
        outs_ = []
        for t in range(width // LANES):
            sl = slice(c0 + t * LANES, c0 + (t + 1) * LANES)
            p = p_ref[:, sl]
            pp = jnp.where(row0, prev_ref[:, sl], pltpu.roll(p, 1, 0))
            outs_.append(p + (pp - p) * mu_ref[:, sl])
        return outs_

    lora = mixed(LORA_OFF, LANES)[0]
    gate_in = jnp.concatenate(mixed(GATE_OFF, SHIFT_PAD - GATE_OFF), axis=1)
    w_pre = w0_ref[...] + _bdot(jnp.tanh(lora), w2_ref[...])
    lw_o[...] = -jnp.exp(-jax.nn.softplus(-w_pre) - 0.5)
    a = jax.nn.sigmoid(a0_ref[...] + _bdot(lora, a2_ref[...]))
    g_o[...] = _bdot(jax.nn.sigmoid(gate_in), g2_ref[...])

    r_t = mixed(0, TOK_W)
    k_t = mixed(TOK_W, TOK_W)
    v_t = mixed(2 * TOK_W, TOK_W)
    if has_vres:
        v_all = jnp.concatenate(v_t, axis=1)
        gate = jax.nn.sigmoid(v0_ref[...] + _bdot(_bdot(v_all, v1_ref[...]), v2_ref[...]))
        v_all = v_all + (vf_ref[...] - v_all) * gate
        v_t = [v_all[:, t * LANES:(t + 1) * LANES] for t in range(N_PAIRS)]
    for t in range(N_PAIRS):
        sl = slice(t * LANES, (t + 1) * LANES)
        kraw = k_t[t]
        kk = kraw * kk_ref[:, sl]
        kk = kk / jnp.maximum(jnp.sqrt(_head_sum(kk * kk, ones_bd)), 1e-12)
        a_t = a[:, sl]
        k = kraw * (1.0 + (a_t - 1.0) * ka_ref[:, sl])
        v = v_t[t]
        r_o[:, sl] = r_t[t]
        k_o[:, sl] = k
        v_o[:, sl] = v
        kn_o[:, sl] = kk
        b_o[:, sl] = kk * a_t
        bonus_o[:, sl] = _head_sum(r_t[t] * k * rk_ref[:, sl], ones_bd) * v


def _wkv_stage(mid, lg_ref, lb_ref, o_ref, s_ref, L):
    r_ref, lw_ref, k_ref, v_ref, kn_ref, b_ref, g_ref, bonus_ref = mid
    nch = r_ref.shape[0] // L
    W2 = 2 * L

    t_i = lax.broadcasted_iota(jnp.int32, (L, W2), 0)
    l_i = lax.broadcasted_iota(jnp.int32, (L, W2), 1)
    j_i = jnp.where(l_i >= L, l_i - L, l_i)
    strict = t_i > j_i
    incl = t_i >= j_i
    eye2 = jnp.where(t_i == j_i, 1.0, 0.0)
    levels = []
    s = 1
    while s < L:
        sh = s.bit_length() - 1
        levels.append(((t_i >> (sh + 1)) == (j_i >> (sh + 1)))
                      & (((t_i >> sh) & 1) == 1) & (((j_i >> sh) & 1) == 0))
        s *= 2
    first_t = l_i < L
    lane = lax.broadcasted_iota(jnp.int32, (L, LANES), 1)
    low = lane < HEAD_D
    rr = lax.broadcasted_iota(jnp.int32, (LANES, LANES), 0)
    cc = lax.broadcasted_iota(jnp.int32, (LANES, LANES), 1)
    bd = (rr < HEAD_D) == (cc < HEAD_D)
    ones_bd = jnp.where(bd, 1.0, 0.0).astype(BF16)
    tri = jnp.where(lax.broadcasted_iota(jnp.int32, (L, L), 0)
                    >= lax.broadcasted_iota(jnp.int32, (L, L), 1), 1.0, 0.0).astype(BF16)

    def cumsum_rows(x):
        hi = x.astype(BF16)
        r1 = x - hi.astype(F32)
        mid = r1.astype(BF16)
        lo = (r1 - mid.astype(F32)).astype(BF16)
        return (jnp.dot(tri, hi, preferred_element_type=F32) + jnp.dot(tri, mid, preferred_element_type=F32)
                + jnp.dot(tri, lo, preferred_element_type=F32))

    def stack_t(x):
        return jnp.concatenate([jnp.where(first_t, x, 0.0), jnp.where(first_t, 0.0, x)],
                               axis=0).astype(BF16)

    def stack_d(x):
        return jnp.concatenate([jnp.where(low, x, 0.0), jnp.where(low, 0.0, x)], axis=0).astype(BF16)

    def tdot(m, stacked):
        return jnp.dot(m.astype(BF16), stacked, preferred_element_type=F32)

    cum_all = [cumsum_rows(lw_ref[ch * L:(ch + 1) * L, :]) for ch in range(nch)]

    probs = [(ch, hp) for ch in range(nch) for hp in range(N_PAIRS)]
    pairs = range(len(probs))
    rows = [slice(ch * L, (ch + 1) * L) for ch, _ in probs]
    sls = [slice(hp * LANES, (hp + 1) * LANES) for _, hp in probs]
    a_t, r_t, sv, bh, kh, dec, cb, ck = [], [], [], [], [], [], [], []
    for p in pairs:
        rw, sl = rows[p], sls[p]
        lw = lw_ref[rw, sl]
        cum = cum_all[probs[p][0]][:, sl]
        cum_l = cum[L - 1:L, :]
        p_inv = jnp.exp(-cum)
        p_end = jnp.exp(cum_l - cum)
        b = b_ref[rw, sl]
        k = k_ref[rw, sl]
        a_t.append(-kn_ref[rw, sl] * jnp.exp(cum - lw))
        r_t.append(r_ref[rw, sl] * jnp.exp(cum))
        sv.append(stack_d(v_ref[rw, sl]))
        bh.append((b * p_end).astype(BF16))
        kh.append((k * p_end).astype(BF16))
        dec.append(jnp.exp(cum_l))
        x = jnp.concatenate([a_t[p], r_t[p]], axis=0).astype(BF16)
        cb.append(lax.dot_general(x, stack_d(b * p_inv), _NT, preferred_element_type=F32))
        ck.append(lax.dot_general(x, stack_d(k * p_inv), _NT, preferred_element_type=F32))
    a_ab = [jnp.where(strict, cb[p][:L], 0.0) for p in pairs]
    a_rb = [jnp.where(incl, cb[p][L:], 0.0).astype(BF16) for p in pairs]
    a_k = [jnp.concatenate([jnp.where(strict, ck[p][:L], 0.0), jnp.where(incl, ck[p][L:], 0.0)],
                           axis=0).astype(BF16) for p in pairs]
    kv = [jnp.dot(a_k[p], sv[p], preferred_element_type=F32) for p in pairs]
    t_m = [eye2 + jnp.where(levels[0], a_ab[p], 0.0) for p in pairs]
    for msk in levels[1:]:
        ta = [tdot(t_m[p], stack_t(jnp.where(msk, a_ab[p], 0.0))) for p in pairs]
        t_m = [t_m[p] + tdot(ta[p], stack_t(t_m[p])) for p in pairs]
    t_b = [t_m[p].astype(BF16) for p in pairs]
    w12 = [jnp.dot(t_b[p], jnp.concatenate([stack_d(a_t[p]), stack_d(kv[p][:L])], axis=1),
                   preferred_element_type=F32) for p in pairs]
    qq = [jnp.dot(a_rb[p], jnp.concatenate([stack_d(w12[p][:, :LANES]), stack_d(w12[p][:, LANES:])],
                                           axis=1), preferred_element_type=F32) for p in pairs]
    mm = [_bdot_tn(w12[p], bh[p]) for p in pairs]
    mk = [_bdot_tn(v_ref[rows[p], sls[p]], kh[p]) for p in pairs]
    y = []
    for p in pairs:
        hp = probs[p][1]
        s_p = s_ref[hp]
        q1 = r_t[p] + qq[p][:, :LANES]
        q2 = qq[p][:, LANES:] + kv[p][L:]
        m1 = jnp.where(bd, mm[p][:LANES], 0.0)
        m2 = jnp.where(bd, mm[p][LANES:] + mk[p], 0.0)
        y.append(_bdot_nt(q1, s_p) + q2)
        s_ref[hp] = s_p * dec[p] + _bdot(s_p, m1) + m2
    mean = [_head_sum(y[p], ones_bd) * (1.0 / HEAD_D) for p in pairs]
    dv = [y[p] - mean[p] for p in pairs]
    var = [_head_sum(dv[p] * dv[p], ones_bd) * (1.0 / HEAD_D) for p in pairs]
    for p in pairs:
        rw, sl = rows[p], sls[p]
        yn = dv[p] * lax.rsqrt(var[p] + GN_EPS) * lg_ref[:, sl] + lb_ref[:, sl]
        o_ref[rw, sl] = ((yn + bonus_ref[rw, sl]) * g_ref[rw, sl]).astype(o_ref.dtype)


_PRE_W = ("mu", "w0", "w2", "a0", "a2", "g2", "k_k", "k_a", "r_k")
_VRES_W = ("v0", "v1", "v2")
_MID = 8


def _rwkv_body(*refs, chunk, has_vres):
    names = _PRE_W + (_VRES_W if has_vres else ())
    it = iter(refs)
    p_ref, shift_ref = next(it), next(it)
    vf_ref = next(it) if has_vres else None
    wr = {n: next(it) for n in names}
    s0_ref, lg_ref, lb_ref = next(it), next(it), next(it)
    o_ref = next(it)
    v_out_ref = None if has_vres else next(it)
    s_out_ref, s_ref, prev_ref = next(it), next(it), next(it)
    mid = [next(it) for _ in range(_MID)]
    c = pl.program_id(1)

    @pl.when(c == 0)
    def _():
        zero = jnp.zeros((HEAD_D, HEAD_D), F32)
        for hp in range(N_PAIRS):
            top = jnp.concatenate([s0_ref[2 * hp], zero], axis=1)
            bot = jnp.concatenate([zero, s0_ref[2 * hp + 1]], axis=1)
            s_ref[hp] = jnp.concatenate([top, bot], axis=0)
        prev_ref[...] = shift_ref[...]

    _pre_stage(p_ref, prev_ref, wr, vf_ref, mid)
    prev_ref[...] = p_ref[p_ref.shape[0] - 1:, :]
    if v_out_ref is not None:
        v_out_ref[...] = mid[3][...]
    _wkv_stage(mid, lg_ref, lb_ref, o_ref, s_ref, chunk)

    @pl.when(c == pl.num_programs(1) - 1)
    def _():
        for hp in range(N_PAIRS):
            s_p = s_ref[hp]
            s_out_ref[2 * hp] = s_p[:HEAD_D, :HEAD_D]
            s_out_ref[2 * hp + 1] = s_p[HEAD_D:, HEAD_D:]


def rwkv_mix(proj, shift0, pw, v_first, s0, sl, batch, seq, chunk, rows):
    steps = seq // rows
    has_vres = v_first is not None
    names = _PRE_W + (_VRES_W if has_vres else ())
    full = lambda a: pl.BlockSpec(a.shape, lambda bi, c: (0,) * a.ndim)
    tile = pl.BlockSpec((rows, TOK_W), lambda bi, c: (bi * steps + c, 0))
    state = pl.BlockSpec((None, N_HEADS, HEAD_D, HEAD_D), lambda bi, c: (bi, 0, 0, 0))
    in_specs = [pl.BlockSpec((rows, SHIFT_PAD), lambda bi, c: (bi * steps + c, 0)),
                pl.BlockSpec((None, 1, SHIFT_PAD), lambda bi, c: (bi, 0, 0))]
    args = [proj, shift0]
    if has_vres:
        in_specs.append(tile)
        args.append(v_first)
    state_in = pl.BlockSpec((None, None, N_HEADS, HEAD_D, HEAD_D), lambda bi, c: (sl, bi, 0, 0, 0))
    in_specs += [full(pw[n]) for n in names] + [state_in, full(pw["lnx_g"]), full(pw["lnx_b"])]
    args += [pw[n] for n in names] + [s0, pw["lnx_g"], pw["lnx_b"]]
    out_specs = [tile] + ([] if has_vres else [tile]) + [state]
    out_shape = ([jax.ShapeDtypeStruct((batch * seq, TOK_W), BF16)]
                 + ([] if has_vres else [jax.ShapeDtypeStruct((batch * seq, TOK_W), F32)])
                 + [jax.ShapeDtypeStruct((batch, N_HEADS, HEAD_D, HEAD_D), F32)])
    outs = pl.pallas_call(
        functools.partial(_rwkv_body, chunk=chunk, has_vres=has_vres),
        grid=(batch, steps),
        in_specs=in_specs,
        out_specs=out_specs,
        out_shape=out_shape,
        scratch_shapes=[pltpu.VMEM((N_PAIRS, LANES, LANES), F32), pltpu.VMEM((1, SHIFT_PAD), F32)]
                       + [pltpu.VMEM((rows, TOK_W), F32)] * _MID,
        compiler_params=_cparams("parallel", "arbitrary"),
        name="rwkv_mix",
    )(*args)
    if has_vres:
        return outs[0], None, outs[1]
    return outs


def _dup_heads(w):
    lead = w.shape[:-1]
    w = w.reshape(lead + (SWA_KV, 1, HEAD_D))
    return jnp.broadcast_to(w, lead + (SWA_KV, 2, HEAD_D)).reshape(lead + (SWA_KV * LANES,))


def _undup_heads(x):
    lead = x.shape[:-1]
    return x.reshape(lead + (SWA_KV, 2, HEAD_D))[..., 0, :]


def _pad_cols(a, width):
    return jnp.pad(a, [(0, 0)] * (a.ndim - 1) + [(0, width - a.shape[-1])])


def _sink_col(sinks, lq):
    return jnp.repeat(sinks.astype(F32).reshape(SWA_KV, SWA_G), lq, axis=1)[..., None]


def _trunk(x, mem_k, mem_v, wkv0, shift0, past_k, past_v, P, batch, seq, mlp_bf16):
    chunk = min(CHUNK, seq)
    rows_mix = min(2 * chunk, seq)
    new_wkv, new_shift = [], []
    v_first = None
    kvd = None
    for l in range(DEPTH):
        if l < N_A:
            pw = P["a"][l]
            proj = norm_matmul(x, P["ln_mix_pre"], l, P["a_w_in"], l)
            new_shift.append(proj.reshape(batch, seq, A_IN_PAD)[:, -1:, :A_SHIFT])
            tok, v, s_new = rwkv_mix(proj, _pad_cols(shift0[l].astype(F32), SHIFT_PAD), pw, v_first,
                                     wkv0.astype(F32), l, batch, seq, chunk, rows_mix)
            if l == 0:
                v_first = v
            new_wkv.append(s_new)
            q_col = A_QMEM_OFF // MEM_W
            w_out, wi = P["a_w_out"], l
        else:
            i = l - N_A
            if kvd is None:
                kvd = norm_matmul(x, P["kv_norm_g"], 0, P["w_kvd"], 0)
            proj = norm_matmul(x, P["ln_mix_pre"], l, P["b_w_in"], i)
            if past_k is None:
                tok = swa_prompt(proj, kvd, _sink_col(P["b_sinks"][i], CHUNK), batch, seq)
            else:
                kd_w = SWA_KV * LANES
                k_all = jnp.concatenate([_dup_heads(past_k.reshape(batch, -1, SWA_KV * HEAD_D)),
                                         kvd[:, :kd_w].reshape(batch, seq, kd_w)], axis=1)
                v_all = jnp.concatenate([_dup_heads(past_v.reshape(batch, -1, SWA_KV * HEAD_D)),
                                         kvd[:, kd_w:].reshape(batch, seq, kd_w)], axis=1)
                tok = swa_sample(proj, k_all, v_all, _sink_col(P["b_sinks"][i], seq), batch, seq)
            q_col = TOK_W // MEM_W
            w_out, wi = P["b_w_out"], i
        mem_o = mem_attn(proj, q_col, mem_k, mem_v, l, batch, seq)
        x = mix_out(tok, mem_o, w_out, wi, P["ln_mix_post"], l, x)
        if l in mlp_bf16:
            wu, wd = mlp_bf16[l]
            x = mlp(x, P["ln_mlp_pre"], wu, wd, P["ln_mlp_post"], l, 0)
        else:
            x, wu, wd = mlp(x, P["ln_mlp_pre"], P["w_up"], P["w_down"], P["ln_mlp_post"], l, l)
            mlp_bf16[l] = (wu[None], wd[None])
    kd_w = SWA_KV * LANES
    keep = min(WINDOW, seq)
    tail = kvd.reshape(batch, seq, 2 * kd_w)[:, seq - keep:]
    k_new = _undup_heads(tail[..., :kd_w]).reshape(batch, keep, SWA_KV, HEAD_D)
    v_new = _undup_heads(tail[..., kd_w:]).reshape(batch, keep, SWA_KV, HEAD_D)
    return x, jnp.stack(new_wkv), jnp.stack(new_shift), k_new, v_new


def _prep_params(ln_mix_pre, ln_mix_post, ln_mlp_pre, ln_mlp_post, mem_norm_g, w_mem_kv, w_up, w_down,
                 a_w_in, a_mu, a_w0, a_w2, a_a0, a_a2, a_g2, a_k_k, a_k_a, a_r_k, a_lnx_g, a_lnx_b,
                 a_v0, a_v1, a_v2, a_w_out, kv_norm_g, w_kv, b_w_in, b_sinks, b_w_out):
    d = D_MODEL
    zrow = lambda n: jnp.zeros((n, TOK_W), F32)
    a_params = []
    for i in range(N_A):
        pw = {
            "mu": _pad_cols(a_mu[i].reshape(1, A_SHIFT), SHIFT_PAD),
            "w0": a_w0[i].reshape(1, TOK_W), "a0": a_a0[i].reshape(1, TOK_W),
            "w2": jnp.concatenate([a_w2[i], zrow(A_LORA)], axis=0).astype(BF16),
            "a2": jnp.concatenate([zrow(W_LORA), a_a2[i]], axis=0).astype(BF16),
            "g2": jnp.concatenate([a_g2[i], zrow(SHIFT_PAD - A_SHIFT)], axis=0).astype(BF16),
            "k_k": a_k_k[i].reshape(1, TOK_W), "k_a": a_k_a[i].reshape(1, TOK_W),
            "r_k": a_r_k[i].reshape(1, TOK_W),
            "lnx_g": a_lnx_g[i].reshape(1, TOK_W), "lnx_b": a_lnx_b[i].reshape(1, TOK_W),
        }
        if i > 0:
            pw["v0"] = a_v0[i - 1].reshape(1, TOK_W)
            pw["v1"] = a_v1[i - 1].astype(BF16)
            pw["v2"] = a_v2[i - 1].astype(BF16)
        a_params.append(pw)
    kw = SWA_KV * HEAD_D
    gain = lambda g: g.reshape(-1, 1, d)
    P = {
        "a": a_params,
        "ln_mix_pre": gain(ln_mix_pre), "ln_mix_post": gain(ln_mix_post), "ln_mlp_pre": gain(ln_mlp_pre),
        "ln_mlp_post": gain(ln_mlp_post), "mem_norm_g": gain(mem_norm_g), "kv_norm_g": gain(kv_norm_g),
        "w_up": w_up, "w_down": w_down, "w_mem_kv": w_mem_kv.astype(BF16),
        "w_kvd": jnp.concatenate([_dup_heads(w_kv[:, :kw]), _dup_heads(w_kv[:, kw:])],
                                 axis=1).astype(BF16)[None],
        "a_w_in": jnp.concatenate([a_w_in[:, :, :A_SHIFT], jnp.zeros((N_A, d, A_QMEM_OFF - A_SHIFT), F32),
                                   a_w_in[:, :, A_SHIFT:]], axis=2).astype(BF16),
        "a_w_out": a_w_out.astype(BF16),
        "b_w_in": b_w_in.astype(BF16), "b_sinks": b_sinks, "b_w_out": b_w_out.astype(BF16),
    }
    return P


def kernel(x_prompt, x_sample, mem_prompt, state_wkv, state_shift, cache_win_k, cache_win_v, cache_mem_k, cache_mem_v, ln_mix_pre, ln_mix_post, ln_mlp_pre, ln_mlp_post, mem_norm_g, w_mem_kv, w_up, w_down, a_w_in, a_mu, a_w0, a_w2, a_a0, a_a2, a_g2, a_k_k, a_k_a, a_r_k, a_lnx_g, a_lnx_b, a_v0, a_v1, a_v2, a_w_out, kv_norm_g, w_kv, b_w_in, b_sinks, b_w_out):
    bp, sp, d = x_prompt.shape
    bs, ss, _ = x_sample.shape
    P = _prep_params(ln_mix_pre, ln_mix_post, ln_mlp_pre, ln_mlp_post, mem_norm_g, w_mem_kv, w_up, w_down,
                     a_w_in, a_mu, a_w0, a_w2, a_a0, a_a2, a_g2, a_k_k, a_k_a, a_r_k, a_lnx_g, a_lnx_b,
                     a_v0, a_v1, a_v2, a_w_out, kv_norm_g, w_kv, b_w_in, b_sinks, b_w_out)

    mlp_bf16 = {}
    y_s, wkv_s, shift_s, k_s, v_s = _trunk(
        x_sample.reshape(bs * ss, d), cache_mem_k, cache_mem_v, state_wkv, state_shift, cache_win_k,
        cache_win_v, P, bs, ss, mlp_bf16)
    past = cache_win_k.shape[1]
    win_k_s = jnp.concatenate([cache_win_k, k_s], axis=1)[:, -past:]
    win_v_s = jnp.concatenate([cache_win_v, v_s], axis=1)[:, -past:]

    mkv = norm_matmul_layers(mem_prompt.reshape(bp * N_MEM, d), P["mem_norm_g"], P["w_mem_kv"])
    mkv4 = mkv.reshape(DEPTH, bp, N_MEM, 2 * MEM_W)
    wkv0 = jnp.zeros((N_A, bp, N_HEADS, HEAD_D, HEAD_D), F32)
    shift0 = jnp.zeros((N_A, bp, 1, A_SHIFT), F32)
    y_p, wkv_p, shift_p, k_p, v_p = _trunk(
        x_prompt.reshape(bp * sp, d), mkv4, mkv4, wkv0, shift0, None, None, P, bp, sp, mlp_bf16)

    return (y_p.reshape(bp, sp, d), y_s.reshape(bs, ss, d), wkv_p, shift_p,
            k_p[:, -WINDOW:], v_p[:, -WINDOW:],
            mkv[:, :, :MEM_W].reshape(DEPTH, bp, N_MEM, MEM_HEADS, MEM_HD),
            mkv[:, :, MEM_W:].reshape(DEPTH, bp, N_MEM, MEM_HEADS, MEM_HD),
            wkv_s, shift_s, win_k_s, win_v_s)
```

```python
import functools

import jax
import jax.numpy as jnp
from jax import lax
from jax.experimental import pallas as pl
from jax.experimental.pallas import tpu as pltpu

F32 = jnp.float32
BF16 = jnp.bfloat16

D_MODEL = 2048
DEPTH = 4
N_A = 2
CHUNK = 64
N_MEM = 256
MEM_HEADS = 4
MEM_HD = 128
MEM_W = MEM_HEADS * MEM_HD
TOK_W = D_MODEL - MEM_W
HEAD_D = 64
N_HEADS = TOK_W // HEAD_D
N_PAIRS = N_HEADS // 2
W_LORA = 64
A_LORA = 64
G_LORA = 224
A_SHIFT = 3 * TOK_W + W_LORA + A_LORA + G_LORA
LORA_OFF = 3 * TOK_W
GATE_OFF = LORA_OFF + 128
SHIFT_PAD = 4992
A_QMEM_OFF = 5120
A_IN_PAD = A_QMEM_OFF + MEM_W
SWA_KV = 4
SWA_G = N_HEADS // SWA_KV
WINDOW = 128
D_FF = 4 * D_MODEL
RMS_EPS = 1e-6
GN_EPS = 64e-5
NEG = -1e30
LANES = 128
VMEM_LIMIT = 56 * 1024 * 1024

_NT = (((1,), (1,)), ((), ()))
_TN = (((0,), (0,)), ((), ()))


def _cparams(*sem):
    return pltpu.CompilerParams(dimension_semantics=sem, vmem_limit_bytes=VMEM_LIMIT)


def _bdot(a, b):
    return jnp.dot(a.astype(BF16), b.astype(BF16), preferred_element_type=F32)


def _bdot_nt(a, b):
    return lax.dot_general(a.astype(BF16), b.astype(BF16), _NT, preferred_element_type=F32)


def _bdot_tn(a, b):
    return lax.dot_general(a.astype(BF16), b.astype(BF16), _TN, preferred_element_type=F32)


def _rms(x, g):
    return x * lax.rsqrt(jnp.mean(x * x, axis=-1, keepdims=True) + RMS_EPS) * g


def _norm_matmul_body(x_ref, g_ref, w_ref, o_ref, xn_ref):
    @pl.when(pl.program_id(1) == 0)
    def _():
        xn_ref[...] = _rms(x_ref[...], g_ref[...]).astype(BF16)

    o_ref[...] = jnp.dot(xn_ref[...], w_ref[...], preferred_element_type=F32).astype(o_ref.dtype)


def _proj_tiles(m, n):
    tm = min(m, 1024)
    for tn in (1408, 1024, 512):
        if n % tn == 0:
            return tm, tn
    raise ValueError(f"unsupported projection width {n}")


def norm_matmul(x, g, gl, w, wl):
    m, d = x.shape
    n = w.shape[2]
    tm, tn = _proj_tiles(m, n)
    return pl.pallas_call(
        _norm_matmul_body,
        grid=(m // tm, n // tn),
        in_specs=[pl.BlockSpec((tm, d), lambda i, j: (i, 0)),
                  pl.BlockSpec((None, 1, d), lambda i, j: (gl, 0, 0)),
                  pl.BlockSpec((None, d, tn), lambda i, j: (wl, 0, j))],
        out_specs=pl.BlockSpec((tm, tn), lambda i, j: (i, j)),
        out_shape=jax.ShapeDtypeStruct((m, n), F32),
        scratch_shapes=[pltpu.VMEM((tm, d), BF16)],
        compiler_params=_cparams("parallel", "arbitrary"),
        name="norm_matmul",
    )(x, g, w)


def norm_matmul_layers(x, g, w):
    m, d = x.shape
    nl, _, n = w.shape
    _, tn = _proj_tiles(m, n)
    return pl.pallas_call(
        _norm_matmul_body,
        grid=(nl, n // tn),
        in_specs=[pl.BlockSpec((m, d), lambda l, j: (0, 0)),
                  pl.BlockSpec((None, 1, d), lambda l, j: (l, 0, 0)),
                  pl.BlockSpec((None, d, tn), lambda l, j: (l, 0, j))],
        out_specs=pl.BlockSpec((None, m, tn), lambda l, j: (l, 0, j)),
        out_shape=jax.ShapeDtypeStruct((nl, m, n), F32),
        scratch_shapes=[pltpu.VMEM((m, d), BF16)],
        compiler_params=_cparams("parallel", "arbitrary"),
        name="norm_matmul_layers",
    )(x, g, w)


def _mix_out_body(tok_ref, mem_ref, wt_ref, wm_ref, g_ref, x_ref, o_ref):
    y = jnp.dot(tok_ref[...], wt_ref[...], preferred_element_type=F32)
    y = y + jnp.dot(mem_ref[...], wm_ref[...], preferred_element_type=F32)
    o_ref[...] = x_ref[...] + _rms(y, g_ref[...])


def mix_out(tok, mem, w_out, wl, g, gl, x):
    m, d = x.shape
    tm = min(m, 512)
    return pl.pallas_call(
        _mix_out_body,
        grid=(m // tm,),
        in_specs=[pl.BlockSpec((tm, TOK_W), lambda i: (i, 0)),
                  pl.BlockSpec((tm, MEM_W), lambda i: (i, 0)),
                  pl.BlockSpec((None, TOK_W, d), lambda i: (wl, 0, 0)),
                  pl.BlockSpec((None, MEM_W, d), lambda i: (wl, TOK_W // MEM_W, 0)),
                  pl.BlockSpec((None, 1, d), lambda i: (gl, 0, 0)),
                  pl.BlockSpec((tm, d), lambda i: (i, 0))],
        out_specs=pl.BlockSpec((tm, d), lambda i: (i, 0)),
        out_shape=jax.ShapeDtypeStruct((m, d), F32),
        compiler_params=_cparams("parallel"),
        name="mix_out",
    )(tok, mem, w_out, w_out, g, x)


def _mlp_body(x_ref, g1_ref, wu_ref, wd_ref, g2_ref, o_ref, *rest):
    xn_ref = rest[-1]
    f = pl.program_id(1)

    @pl.when(f == 0)
    def _():
        xn_ref[...] = _rms(x_ref[...], g1_ref[...]).astype(BF16)
        o_ref[...] = jnp.zeros_like(o_ref)

    wu = wu_ref[...].astype(BF16)
    wd = wd_ref[...].astype(BF16)
    if len(rest) == 3:
        rest[0][...] = wu
        rest[1][...] = wd
    h = jnp.maximum(jnp.dot(xn_ref[...], wu, preferred_element_type=F32), 0.0)
    o_ref[...] += jnp.dot((h * h).astype(BF16), wd, preferred_element_type=F32)

    @pl.when(f == pl.num_programs(1) - 1)
    def _():
        o_ref[...] = x_ref[...] + _rms(o_ref[...], g2_ref[...])


def mlp(x, g1, w_up, w_down, g2, l, wl):
    m, d = x.shape
    ff = w_up.shape[2]
    emit = w_up.dtype != BF16
    tm, tf = (min(m, 1024), 512) if emit else (min(m, 512), 1024)
    rows_mode = pl.Buffered(1) if emit else None
    out_specs = [pl.BlockSpec((tm, d), lambda i, f: (i, 0), pipeline_mode=rows_mode)]
    out_shape = [jax.ShapeDtypeStruct((m, d), F32)]
    if emit:
        out_specs += [pl.BlockSpec((d, tf), lambda i, f: (0, f)), pl.BlockSpec((tf, d), lambda i, f: (f, 0))]
        out_shape += [jax.ShapeDtypeStruct((d, ff), BF16), jax.ShapeDtypeStruct((ff, d), BF16)]
    outs = pl.pallas_call(
        _mlp_body,
        grid=(m // tm, ff // tf),
        in_specs=[pl.BlockSpec((tm, d), lambda i, f: (i, 0), pipeline_mode=rows_mode),
                  pl.BlockSpec((None, 1, d), lambda i, f: (l, 0, 0)),
                  pl.BlockSpec((None, d, tf), lambda i, f: (wl, 0, f)),
                  pl.BlockSpec((None, tf, d), lambda i, f: (wl, f, 0)),
                  pl.BlockSpec((None, 1, d), lambda i, f: (l, 0, 0))],
        out_specs=out_specs,
        out_shape=out_shape,
        scratch_shapes=[pltpu.VMEM((tm, d), BF16)],
        compiler_params=_cparams("parallel", "arbitrary"),
        name="mlp",
    )(x, g1, w_up, w_down, g2)
    return outs if emit else outs[0]


def _mem_attn_body(q_ref, k_ref, v_ref, o_ref):
    scale = MEM_HD ** -0.5
    nb = k_ref.shape[0]
    rq = q_ref.shape[0] // nb
    sls = [slice(h * MEM_HD, (h + 1) * MEM_HD) for h in range(MEM_HEADS)]
    rows = [slice(b * rq, (b + 1) * rq) for b in range(nb)]
    head = ((lambda ref, b, h: ref[b, :, h, :]) if len(k_ref.shape) == 4
            else (lambda ref, b, h: ref[b, :, sls[h]]))
    probs = [(b, h) for b in range(nb) for h in range(MEM_HEADS)]
    s = [_bdot_nt(q_ref[rows[b], sls[h]], head(k_ref, b, h)) * scale for b, h in probs]
    e = [jnp.exp(t - jnp.max(t, axis=-1, keepdims=True)) for t in s]
    o = [_bdot(e[n], head(v_ref, b, h)) * (1.0 / jnp.sum(e[n], axis=-1, keepdims=True))
         for n, (b, h) in enumerate(probs)]
    for n, (b, h) in enumerate(probs):
        o_ref[rows[b], sls[h]] = o[n].astype(o_ref.dtype)


def mem_attn(proj, q_col_block, mem_k, mem_v, l, batch, rows_per_seq):
    m = proj.shape[0]
    if mem_k.ndim == 5:
        nb = _batch_block(batch, 4)
        tr = nb * rows_per_seq
        blk = (None, nb, N_MEM, MEM_HEADS, MEM_HD)
        k_spec = pl.BlockSpec(blk, lambda i: (l, i, 0, 0, 0))
        v_spec = k_spec
    else:
        tr = min(rows_per_seq, 512)
        tiles_per_seq = rows_per_seq // tr
        blk = (None, 1, N_MEM, MEM_W)
        k_spec = pl.BlockSpec(blk, lambda i: (l, i // tiles_per_seq, 0, 0))
        v_spec = pl.BlockSpec(blk, lambda i: (l, i // tiles_per_seq, 0, 1))
    return pl.pallas_call(
        _mem_attn_body,
        grid=(m // tr,),
        in_specs=[pl.BlockSpec((tr, MEM_W), lambda i: (i, q_col_block)), k_spec, v_spec],
        out_specs=pl.BlockSpec((tr, MEM_W), lambda i: (i, 0)),
        out_shape=jax.ShapeDtypeStruct((m, MEM_W), BF16),
        compiler_params=_cparams("parallel"),
        name="mem_attn",
    )(proj, mem_k, mem_v)


def _swa_core(units, q_tile, k_tiles, v_tiles, valid_of, sink_ref, o_store, lq):
    lane = lax.broadcasted_iota(jnp.int32, (lq, LANES), 1)
    low = lane < HEAD_D
    probs = [(u, j) for u in units for j in range(SWA_KV)]
    s = []
    for u, j in probs:
        xs = []
        for i in range(SWA_G // 2):
            pair = q_tile(u, (j * SWA_G + 2 * i) * HEAD_D) * (HEAD_D ** -0.5)
            xs.append(jnp.where(low, pair, 0.0).astype(BF16))
            xs.append(jnp.where(low, 0.0, pair).astype(BF16))
        x = jnp.concatenate(xs, axis=0)
        kj = jnp.concatenate([t.astype(BF16) for t in k_tiles(u, j)], axis=0)
        sj = lax.dot_general(x, kj, _NT, preferred_element_type=F32)
        valid = valid_of(u)
        s.append(sj if valid is None else jnp.where(valid, sj, NEG))
    e, inv = [], []
    for n, (u, j) in enumerate(probs):
        sink = sink_ref[j]
        mx = jnp.maximum(jnp.max(s[n], axis=-1, keepdims=True), sink)
        en = jnp.exp(s[n] - mx)
        inv.append(1.0 / (jnp.sum(en, axis=-1, keepdims=True) + jnp.exp(sink - mx)))
        e.append(en.astype(BF16))
    o = [jnp.dot(e[n], jnp.concatenate([t.astype(BF16) for t in v_tiles(u, j)], axis=0),
                 preferred_element_type=F32) * inv[n] for n, (u, j) in enumerate(probs)]
    for n, (u, j) in enumerate(probs):
        for i in range(SWA_G // 2):
            top = o[n][(2 * i) * lq:(2 * i + 1) * lq]
            bot = o[n][(2 * i + 1) * lq:(2 * i + 2) * lq]
            o_store(u, (j * SWA_G + 2 * i) * HEAD_D, jnp.where(low, top, bot))


def _swa_prompt_body(q_ref, ka, kb, va, vb, sink_ref, o_ref):
    c2 = pl.program_id(1)
    col = lax.broadcasted_iota(jnp.int32, (SWA_G * CHUNK, 3 * CHUNK), 1)
    units = [(b, h) for b in range(q_ref.shape[0]) for h in range(2)]
    rows = lambda h: slice(h * CHUNK, (h + 1) * CHUNK)

    def tiles(ra, rb):
        def get(u, j):
            b, h = u
            ls = slice(j * LANES, (j + 1) * LANES)
            blocks = [ra[b, rows(0), ls], ra[b, rows(1), ls], rb[b, rows(0), ls], rb[b, rows(1), ls]]
            return blocks[h:h + 3]
        return get

    def store(u, c0, tile):
        o_ref[u[0], rows(u[1]), c0:c0 + LANES] = tile.astype(o_ref.dtype)

    _swa_core(units, lambda u, c0: q_ref[u[0], rows(u[1]), c0:c0 + LANES], tiles(ka, kb), tiles(va, vb),
              lambda u: col >= jnp.maximum(2 - (2 * c2 + u[1]), 0) * CHUNK,
              sink_ref, store, CHUNK)


def _batch_block(batch, want):
    return want if batch % want == 0 else 1


def swa_prompt(proj, kvd, sink_col, batch, seq):
    rows = 2 * CHUNK
    kd_w = SWA_KV * LANES
    nb = _batch_block(batch, 2)
    kv_prev = lambda col: pl.BlockSpec((nb, rows, kd_w), lambda b, c: (b, jnp.maximum(c - 1, 0), col))
    kv_same = lambda col: pl.BlockSpec((nb, rows, kd_w), lambda b, c: (b, c, col))
    kv3 = kvd.reshape(batch, seq, 2 * kd_w)
    out = pl.pallas_call(
        _swa_prompt_body,
        grid=(batch // nb, seq // rows),
        in_specs=[pl.BlockSpec((nb, rows, TOK_W), lambda b, c: (b, c, 0)),
                  kv_prev(0), kv_same(0), kv_prev(1), kv_same(1),
                  pl.BlockSpec((SWA_KV, SWA_G * CHUNK, 1), lambda b, c: (0, 0, 0))],
        out_specs=pl.BlockSpec((nb, rows, TOK_W), lambda b, c: (b, c, 0)),
        out_shape=jax.ShapeDtypeStruct((batch, seq, TOK_W), BF16),
        compiler_params=_cparams("parallel", "arbitrary"),
        name="swa_prompt",
    )(proj.reshape(batch, seq, -1), kv3, kv3, kv3, kv3, sink_col)
    return out.reshape(batch * seq, TOK_W)


def _swa_sample_body(q_ref, k_ref, v_ref, sink_ref, o_ref):
    nb, lq, _ = q_ref.shape

    def store(b, c0, tile):
        o_ref[b, :, c0:c0 + LANES] = tile.astype(o_ref.dtype)

    _swa_core(range(nb), lambda b, c0: q_ref[b, :, c0:c0 + LANES],
              lambda b, j: [k_ref[b, :, j * LANES:(j + 1) * LANES]],
              lambda b, j: [v_ref[b, :, j * LANES:(j + 1) * LANES]],
              lambda b: None, sink_ref, store, lq)


def swa_sample(proj, k_all, v_all, sink_col, batch, lq):
    keys = k_all.shape[1]
    kd_w = SWA_KV * LANES
    nb = _batch_block(batch, 4)
    out = pl.pallas_call(
        _swa_sample_body,
        grid=(batch // nb,),
        in_specs=[pl.BlockSpec((nb, lq, TOK_W), lambda b: (b, 0, 0)),
                  pl.BlockSpec((nb, keys, kd_w), lambda b: (b, 0, 0)),
                  pl.BlockSpec((nb, keys, kd_w), lambda b: (b, 0, 0)),
                  pl.BlockSpec((SWA_KV, SWA_G * lq, 1), lambda b: (0, 0, 0))],
        out_specs=pl.BlockSpec((nb, lq, TOK_W), lambda b: (b, 0, 0)),
        out_shape=jax.ShapeDtypeStruct((batch, lq, TOK_W), BF16),
        compiler_params=_cparams("parallel"),
        name="swa_sample",
    )(proj.reshape(batch, lq, -1), k_all, v_all, sink_col)
    return out.reshape(batch * lq, TOK_W)


def _pair_ones():
    r = lax.broadcasted_iota(jnp.int32, (LANES, LANES), 0)
    c = lax.broadcasted_iota(jnp.int32, (LANES, LANES), 1)
    return jnp.where((r < HEAD_D) == (c < HEAD_D), 1.0, 0.0).astype(BF16)


def _head_sum(x, ones_bd):
    return jnp.dot(x.astype(BF16), ones_bd, preferred_element_type=F32)


def _pre_stage(p_ref, prev_ref, wr, vf_ref, outs):
    has_vres = vf_ref is not None
    mu_ref, w0_ref, w2_ref, a0_ref, a2_ref, g2_ref = (wr[n] for n in ("mu", "w0", "w2", "a0", "a2", "g2"))
    kk_ref, ka_ref, rk_ref = wr["k_k"], wr["k_a"], wr["r_k"]
    if has_vres:
        v0_ref, v1_ref, v2_ref = wr["v0"], wr["v1"], wr["v2"]
    r_o, lw_o, k_o, v_o, kn_o, b_o, g_o, bonus_o = outs
    tr = p_ref.shape[0]
    row0 = lax.broadcasted_iota(jnp.int32, (tr, LANES), 0) == 0
    ones_bd = _pair_ones()

    def mixed(c0, width):
        outs_ = []
        for t in range(width // LANES):
            sl = slice(c0 + t * LANES, c0 + (t + 1) * LANES)
            p = p_ref[:, sl]
            pp = jnp.where(row0, prev_ref[:, sl], pltpu.roll(p, 1, 0))
            outs_.append(p + (pp - p) * mu_ref[:, sl])
        return outs_

    lora = mixed(LORA_OFF, LANES)[0]
    gate_in = jnp.concatenate(mixed(GATE_OFF, SHIFT_PAD - GATE_OFF), axis=1)
    w_pre = w0_ref[...] + _bdot(jnp.tanh(lora), w2_ref[...])
    lw_o[...] = -jnp.exp(-jax.nn.softplus(-w_pre) - 0.5)
    a = jax.nn.sigmoid(a0_ref[...] + _bdot(lora, a2_ref[...]))
    g_o[...] = _bdot(jax.nn.sigmoid(gate_in), g2_ref[...])

    r_t = mixed(0, TOK_W)
    k_t = mixed(TOK_W, TOK_W)
    v_t = mixed(2 * TOK_W, TOK_W)
    if has_vres:
        v_all = jnp.concatenate(v_t, axis=1)
        gate = jax.nn.sigmoid(v0_ref[...] + _bdot(_bdot(v_all, v1_ref[...]), v2_ref[...]))
        v_all = v_all + (vf_ref[...] - v_all) * gate
        v_t = [v_all[:, t * LANES:(t + 1) * LANES] for t in range(N_PAIRS)]
    for t in range(N_PAIRS):
        sl = slice(t * LANES, (t + 1) * LANES)
        kraw = k_t[t]
        kk = kraw * kk_ref[:, sl]
        kk = kk / jnp.maximum(jnp.sqrt(_head_sum(kk * kk, ones_bd)), 1e-12)
        a_t = a[:, sl]
        k = kraw * (1.0 + (a_t - 1.0) * ka_ref[:, sl])
        v = v_t[t]
        r_o[:, sl] = r_t[t]
        k_o[:, sl] = k
        v_o[:, sl] = v
        kn_o[:, sl] = kk
        b_o[:, sl] = kk * a_t
        bonus_o[:, sl] = _head_sum(r_t[t] * k * rk_ref[:, sl], ones_bd) * v


def _wkv_stage(mid, lg_ref, lb_ref, o_ref, s_ref, L):
    r_ref, lw_ref, k_ref, v_ref, kn_ref, b_ref, g_ref, bonus_ref = mid
    nch = r_ref.shape[0] // L
    W2 = 2 * L

    t_i = lax.broadcasted_iota(jnp.int32, (L, W2), 0)
    l_i = lax.broadcasted_iota(jnp.int32, (L, W2), 1)
    j_i = jnp.where(l_i >= L, l_i - L, l_i)
    strict = t_i > j_i
    incl = t_i >= j_i
    eye2 = jnp.where(t_i == j_i, 1.0, 0.0)
    levels = []
    s = 1
    while s < L:
        sh = s.bit_length() - 1
        levels.append(((t_i >> (sh + 1)) == (j_i >> (sh + 1)))
                      & (((t_i >> sh) & 1) == 1) & (((j_i >> sh) & 1) == 0))
        s *= 2
    first_t = l_i < L
    lane = lax.broadcasted_iota(jnp.int32, (L, LANES), 1)
    low = lane < HEAD_D
    rr = lax.broadcasted_iota(jnp.int32, (LANES, LANES), 0)
    cc = lax.broadcasted_iota(jnp.int32, (LANES, LANES), 1)
    bd = (rr < HEAD_D) == (cc < HEAD_D)
    ones_bd = jnp.where(bd, 1.0, 0.0).astype(BF16)
    tri = jnp.where(lax.broadcasted_iota(jnp.int32, (L, L), 0)
                    >= lax.broadcasted_iota(jnp.int32, (L, L), 1), 1.0, 0.0).astype(BF16)

    def cumsum_rows(x):
        hi = x.astype(BF16)
        r1 = x - hi.astype(F32)
        mid = r1.astype(BF16)
        lo = (r1 - mid.astype(F32)).astype(BF16)
        return (jnp.dot(tri, hi, preferred_element_type=F32) + jnp.dot(tri, mid, preferred_element_type=F32)
                + jnp.dot(tri, lo, preferred_element_type=F32))

    def stack_t(x):
        return jnp.concatenate([jnp.where(first_t, x, 0.0), jnp.where(first_t, 0.0, x)],
                               axis=0).astype(BF16)

    def stack_d(x):
        return jnp.concatenate([jnp.where(low, x, 0.0), jnp.where(low, 0.0, x)], axis=0).astype(BF16)

    def tdot(m, stacked):
        return jnp.dot(m.astype(BF16), stacked, preferred_element_type=F32)

    cum_all = [cumsum_rows(lw_ref[ch * L:(ch + 1) * L, :]) for ch in range(nch)]

    probs = [(ch, hp) for ch in range(nch) for hp in range(N_PAIRS)]
    pairs = range(len(probs))
    rows = [slice(ch * L, (ch + 1) * L) for ch, _ in probs]
    sls = [slice(hp * LANES, (hp + 1) * LANES) for _, hp in probs]
    a_t, r_t, sv, bh, kh, dec, cb, ck = [], [], [], [], [], [], [], []
    for p in pairs:
        rw, sl = rows[p], sls[p]
        lw = lw_ref[rw, sl]
        cum = cum_all[probs[p][0]][:, sl]
        cum_l = cum[L - 1:L, :]
        p_inv = jnp.exp(-cum)
        p_end = jnp.exp(cum_l - cum)
        b = b_ref[rw, sl]
        k = k_ref[rw, sl]
        a_t.append(-kn_ref[rw, sl] * jnp.exp(cum - lw))
        r_t.append(r_ref[rw, sl] * jnp.exp(cum))
        sv.append(stack_d(v_ref[rw, sl]))
        bh.append((b * p_end).astype(BF16))
        kh.append((k * p_end).astype(BF16))
        dec.append(jnp.exp(cum_l))
        x = jnp.concatenate([a_t[p], r_t[p]], axis=0).astype(BF16)
        cb.append(lax.dot_general(x, stack_d(b * p_inv), _NT, preferred_element_type=F32))
        ck.append(lax.dot_general(x, stack_d(k * p_inv), _NT, preferred_element_type=F32))
    a_ab = [jnp.where(strict, cb[p][:L], 0.0) for p in pairs]
    a_rb = [jnp.where(incl, cb[p][L:], 0.0).astype(BF16) for p in pairs]
    a_k = [jnp.concatenate([jnp.where(strict, ck[p][:L], 0.0), jnp.where(incl, ck[p][L:], 0.0)],
                           axis=0).astype(BF16) for p in pairs]
    kv = [jnp.dot(a_k[p], sv[p], preferred_element_type=F32) for p in pairs]
    t_m = [eye2 + jnp.where(levels[0], a_ab[p], 0.0) for p in pairs]
    for msk in levels[1:]:
        ta = [tdot(t_m[p], stack_t(jnp.where(msk, a_ab[p], 0.0))) for p in pairs]
        t_m = [t_m[p] + tdot(ta[p], stack_t(t_m[p])) for p in pairs]
    t_b = [t_m[p].astype(BF16) for p in pairs]
    w12 = [jnp.dot(t_b[p], jnp.concatenate([stack_d(a_t[p]), stack_d(kv[p][:L])], axis=1),
                   preferred_element_type=F32) for p in pairs]
    qq = [jnp.dot(a_rb[p], jnp.concatenate([stack_d(w12[p][:, :LANES]), stack_d(w12[p][:, LANES:])],
                                           axis=1), preferred_element_type=F32) for p in pairs]
    mm = [_bdot_tn(w12[p], bh[p]) for p in pairs]
    mk = [_bdot_tn(v_ref[rows[p], sls[p]], kh[p]) for p in pairs]
    y = []
    for p in pairs:
        hp = probs[p][1]
        s_p = s_ref[hp]
        q1 = r_t[p] + qq[p][:, :LANES]
        q2 = qq[p][:, LANES:] + kv[p][L:]
        m1 = jnp.where(bd, mm[p][:LANES], 0.0)
        m2 = jnp.where(bd, mm[p][LANES:] + mk[p], 0.0)
        y.append(_bdot_nt(q1, s_p) + q2)
        s_ref[hp] = s_p * dec[p] + _bdot(s_p, m1) + m2
    mean = [_head_sum(y[p], ones_bd) * (1.0 / HEAD_D) for p in pairs]
    dv = [y[p] - mean[p] for p in pairs]
    var = [_head_sum(dv[p] * dv[p], ones_bd) * (1.0 / HEAD_D) for p in pairs]
    for p in pairs:
        rw, sl = rows[p], sls[p]
        yn = dv[p] * lax.rsqrt(var[p] + GN_EPS) * lg_ref[:, sl] + lb_ref[:, sl]
        o_ref[rw, sl] = ((yn + bonus_ref[rw, sl]) * g_ref[rw, sl]).astype(o_ref.dtype)


_PRE_W = ("mu", "w0", "w2", "a0", "a2", "g2", "k_k", "k_a", "r_k")
_VRES_W = ("v0", "v1", "v2")
_MID = 8


def _rwkv_body(*refs, chunk, has_vres):
    names = _PRE_W + (_VRES_W if has_vres else ())
    it = iter(refs)
    p_ref, shift_ref = next(it), next(it)
    vf_ref = next(it) if has_vres else None
    wr = {n: next(it) for n in names}
    s0_ref, lg_ref, lb_ref = next(it), next(it), next(it)
    o_ref = next(it)
    v_out_ref = None if has_vres else next(it)
    s_out_ref, s_ref, prev_ref = next(it), next(it), next(it)
    mid = [next(it) for _ in range(_MID)]
    c = pl.program_id(1)

    @pl.when(c == 0)
    def _():
        zero = jnp.zeros((HEAD_D, HEAD_D), F32)
        for hp in range(N_PAIRS):
            top = jnp.concatenate([s0_ref[2 * hp], zero], axis=1)
            bot = jnp.concatenate([zero, s0_ref[2 * hp + 1]], axis=1)
            s_ref[hp] = jnp.concatenate([top, bot], axis=0)
        prev_ref[...] = shift_ref[...]

    _pre_stage(p_ref, prev_ref, wr, vf_ref, mid)
    prev_ref[...] = p_ref[p_ref.shape[0] - 1:, :]
    if v_out_ref is not None:
        v_out_ref[...] = mid[3][...]
    _wkv_stage(mid, lg_ref, lb_ref, o_ref, s_ref, chunk)

    @pl.when(c == pl.num_programs(1) - 1)
    def _():
        for hp in range(N_PAIRS):
            s_p = s_ref[hp]
            s_out_ref[2 * hp] = s_p[:HEAD_D, :HEAD_D]
            s_out_ref[2 * hp + 1] = s_p[HEAD_D:, HEAD_D:]


def rwkv_mix(proj, shift0, pw, v_first, s0, sl, batch, seq, chunk, rows):
    steps = seq // rows
    has_vres = v_first is not None
    names = _PRE_W + (_VRES_W if has_vres else ())
    full = lambda a: pl.BlockSpec(a.shape, lambda bi, c: (0,) * a.ndim)
    tile = pl.BlockSpec((rows, TOK_W), lambda bi, c: (bi * steps + c, 0))
    state = pl.BlockSpec((None, N_HEADS, HEAD_D, HEAD_D), lambda bi, c: (bi, 0, 0, 0))
    in_specs = [pl.BlockSpec((rows, SHIFT_PAD), lambda bi, c: (bi * steps + c, 0)),
                pl.BlockSpec((None, 1, SHIFT_PAD), lambda bi, c: (bi, 0, 0))]
    args = [proj, shift0]
    if has_vres:
        in_specs.append(tile)
        args.append(v_first)
    state_in = pl.BlockSpec((None, None, N_HEADS, HEAD_D, HEAD_D), lambda bi, c: (sl, bi, 0, 0, 0))
    in_specs += [full(pw[n]) for n in names] + [state_in, full(pw["lnx_g"]), full(pw["lnx_b"])]
    args += [pw[n] for n in names] + [s0, pw["lnx_g"], pw["lnx_b"]]
    out_specs = [tile] + ([] if has_vres else [tile]) + [state]
    out_shape = ([jax.ShapeDtypeStruct((batch * seq, TOK_W), BF16)]
                 + ([] if has_vres else [jax.ShapeDtypeStruct((batch * seq, TOK_W), F32)])
                 + [jax.ShapeDtypeStruct((batch, N_HEADS, HEAD_D, HEAD_D), F32)])
    outs = pl.pallas_call(
        functools.partial(_rwkv_body, chunk=chunk, has_vres=has_vres),
        grid=(batch, steps),
        in_specs=in_specs,
        out_specs=out_specs,
        out_shape=out_shape,
        scratch_shapes=[pltpu.VMEM((N_PAIRS, LANES, LANES), F32), pltpu.VMEM((1, SHIFT_PAD), F32)]
                       + [pltpu.VMEM((rows, TOK_W), F32)] * _MID,
        compiler_params=_cparams("parallel", "arbitrary"),
        name="rwkv_mix",
    )(*args)
    if has_vres:
        return outs[0], None, outs[1]
    return outs


def _dup_heads(w):
    lead = w.shape[:-1]
    w = w.reshape(lead + (SWA_KV, 1, HEAD_D))
    return jnp.broadcast_to(w, lead + (SWA_KV, 2, HEAD_D)).reshape(lead + (SWA_KV * LANES,))


def _undup_heads(x):
    lead = x.shape[:-1]
    return x.reshape(lead + (SWA_KV, 2, HEAD_D))[..., 0, :]


def _pad_cols(a, width):
    return jnp.pad(a, [(0, 0)] * (a.ndim - 1) + [(0, width - a.shape[-1])])


def _spread_cols(w):
    out = jnp.zeros(w.shape[:-1] + (A_IN_PAD,), w.dtype)
    out = lax.dynamic_update_slice_in_dim(out, w[..., :A_SHIFT], 0, axis=-1)
    return lax.dynamic_update_slice_in_dim(out, w[..., A_SHIFT:], A_QMEM_OFF, axis=-1)


def _sink_col(sinks, lq):
    return jnp.repeat(sinks.astype(F32).reshape(SWA_KV, SWA_G), lq, axis=1)[..., None]


def _trunk(x, mem_k, mem_v, wkv0, shift0, past_k, past_v, P, batch, seq, mlp_bf16):
    chunk = min(CHUNK, seq)
    rows_mix = min(2 * chunk, seq)
    new_wkv, new_shift = [], []
    v_first = None
    kvd = None
    for l in range(DEPTH):
        if l < N_A:
            pw = P["a"][l]
            proj = norm_matmul(x, P["ln_mix_pre"], l, P["a_w_in"], l)
            new_shift.append(proj.reshape(batch, seq, A_IN_PAD)[:, -1:, :A_SHIFT])
            tok, v, s_new = rwkv_mix(proj, _pad_cols(shift0[l].astype(F32), SHIFT_PAD), pw, v_first,
                                     wkv0.astype(F32), l, batch, seq, chunk, rows_mix)
            if l == 0:
                v_first = v
            new_wkv.append(s_new)
            q_col = A_QMEM_OFF // MEM_W
            w_out, wi = P["a_w_out"], l
        else:
            i = l - N_A
            if kvd is None:
                kvd = norm_matmul(x, P["kv_norm_g"], 0, P["w_kvd"], 0)
            proj = norm_matmul(x, P["ln_mix_pre"], l, P["b_w_in"], i)
            if past_k is None:
                tok = swa_prompt(proj, kvd, _sink_col(P["b_sinks"][i], CHUNK), batch, seq)
            else:
                kd_w = SWA_KV * LANES
                k_all = jnp.concatenate([_dup_heads(past_k.reshape(batch, -1, SWA_KV * HEAD_D)),
                                         kvd[:, :kd_w].reshape(batch, seq, kd_w)], axis=1)
                v_all = jnp.concatenate([_dup_heads(past_v.reshape(batch, -1, SWA_KV * HEAD_D)),
                                         kvd[:, kd_w:].reshape(batch, seq, kd_w)], axis=1)
                tok = swa_sample(proj, k_all, v_all, _sink_col(P["b_sinks"][i], seq), batch, seq)
            q_col = TOK_W // MEM_W
            w_out, wi = P["b_w_out"], i
        mem_o = mem_attn(proj, q_col, mem_k, mem_v, l, batch, seq)
        x = mix_out(tok, mem_o, w_out, wi, P["ln_mix_post"], l, x)
        if l in mlp_bf16:
            wu, wd = mlp_bf16[l]
            x = mlp(x, P["ln_mlp_pre"], wu, wd, P["ln_mlp_post"], l, 0)
        else:
            x, wu, wd = mlp(x, P["ln_mlp_pre"], P["w_up"], P["w_down"], P["ln_mlp_post"], l, l)
            mlp_bf16[l] = (wu[None], wd[None])
    kd_w = SWA_KV * LANES
    keep = min(WINDOW, seq)
    tail = kvd.reshape(batch, seq, 2 * kd_w)[:, seq - keep:]
    k_new = _undup_heads(tail[..., :kd_w]).reshape(batch, keep, SWA_KV, HEAD_D)
    v_new = _undup_heads(tail[..., kd_w:]).reshape(batch, keep, SWA_KV, HEAD_D)
    return x, jnp.stack(new_wkv), jnp.stack(new_shift), k_new, v_new


def _prep_params(ln_mix_pre, ln_mix_post, ln_mlp_pre, ln_mlp_post, mem_norm_g, w_mem_kv, w_up, w_down,
                 a_w_in, a_mu, a_w0, a_w2, a_a0, a_a2, a_g2, a_k_k, a_k_a, a_r_k, a_lnx_g, a_lnx_b,
                 a_v0, a_v1, a_v2, a_w_out, kv_norm_g, w_kv, b_w_in, b_sinks, b_w_out):
    d = D_MODEL
    zrow = lambda n: jnp.zeros((n, TOK_W), F32)
    a_params = []
    for i in range(N_A):
        pw = {
            "mu": _pad_cols(a_mu[i].reshape(1, A_SHIFT), SHIFT_PAD),
            "w0": a_w0[i].reshape(1, TOK_W), "a0": a_a0[i].reshape(1, TOK_W),
            "w2": jnp.concatenate([a_w2[i], zrow(A_LORA)], axis=0).astype(BF16),
            "a2": jnp.concatenate([zrow(W_LORA), a_a2[i]], axis=0).astype(BF16),
            "g2": jnp.concatenate([a_g2[i], zrow(SHIFT_PAD - A_SHIFT)], axis=0).astype(BF16),
            "k_k": a_k_k[i].reshape(1, TOK_W), "k_a": a_k_a[i].reshape(1, TOK_W),
            "r_k": a_r_k[i].reshape(1, TOK_W),
            "lnx_g": a_lnx_g[i].reshape(1, TOK_W), "lnx_b": a_lnx_b[i].reshape(1, TOK_W),
        }
        if i > 0:
            pw["v0"] = a_v0[i - 1].reshape(1, TOK_W)
            pw["v1"] = a_v1[i - 1].astype(BF16)
            pw["v2"] = a_v2[i - 1].astype(BF16)
        a_params.append(pw)
    kw = SWA_KV * HEAD_D
    gain = lambda g: g.reshape(-1, 1, d)
    P = {
        "a": a_params,
        "ln_mix_pre": gain(ln_mix_pre), "ln_mix_post": gain(ln_mix_post), "ln_mlp_pre": gain(ln_mlp_pre),
        "ln_mlp_post": gain(ln_mlp_post), "mem_norm_g": gain(mem_norm_g), "kv_norm_g": gain(kv_norm_g),
        "w_up": w_up, "w_down": w_down, "w_mem_kv": w_mem_kv.astype(BF16),
        "w_kvd": jnp.concatenate([_dup_heads(w_kv[:, :kw]), _dup_heads(w_kv[:, kw:])],
                                 axis=1).astype(BF16)[None],
        "a_w_in": _spread_cols(a_w_in.astype(BF16)),
        "a_w_out": a_w_out.astype(BF16),
        "b_w_in": b_w_in.astype(BF16), "b_sinks": b_sinks, "b_w_out": b_w_out.astype(BF16),
    }
    return P


def kernel(x_prompt, x_sample, mem_prompt, state_wkv, state_shift, cache_win_k, cache_win_v, cache_mem_k, cache_mem_v, ln_mix_pre, ln_mix_post, ln_mlp_pre, ln_mlp_post, mem_norm_g, w_mem_kv, w_up, w_down, a_w_in, a_mu, a_w0, a_w2, a_a0, a_a2, a_g2, a_k_k, a_k_a, a_r_k, a_lnx_g, a_lnx_b, a_v0, a_v1, a_v2, a_w_out, kv_norm_g, w_kv, b_w_in, b_sinks, b_w_out):
    bp, sp, d = x_prompt.shape
    bs, ss, _ = x_sample.shape
    P = _prep_params(ln_mix_pre, ln_mix_post, ln_mlp_pre, ln_mlp_post, mem_norm_g, w_mem_kv, w_up, w_down,
                     a_w_in, a_mu, a_w0, a_w2, a_a0, a_a2, a_g2, a_k_k, a_k_a, a_r_k, a_lnx_g, a_lnx_b,
                     a_v0, a_v1, a_v2, a_w_out, kv_norm_g, w_kv, b_w_in, b_sinks, b_w_out)

    mlp_bf16 = {}
    y_s, wkv_s, shift_s, k_s, v_s = _trunk(
        x_sample.reshape(bs * ss, d), cache_mem_k, cache_mem_v, state_wkv, state_shift, cache_win_k,
        cache_win_v, P, bs, ss, mlp_bf16)
    past = cache_win_k.shape[1]
    win_k_s = jnp.concatenate([cache_win_k, k_s], axis=1)[:, -past:]
    win_v_s = jnp.concatenate([cache_win_v, v_s], axis=1)[:, -past:]

    mkv = norm_matmul_layers(mem_prompt.reshape(bp * N_MEM, d), P["mem_norm_g"], P["w_mem_kv"])
    mkv4 = mkv.reshape(DEPTH, bp, N_MEM, 2 * MEM_W)
    wkv0 = jnp.zeros((N_A, bp, N_HEADS, HEAD_D, HEAD_D), F32)
    shift0 = jnp.zeros((N_A, bp, 1, A_SHIFT), F32)
    y_p, wkv_p, shift_p, k_p, v_p = _trunk(
        x_prompt.reshape(bp * sp, d), mkv4, mkv4, wkv0, shift0, None, None, P, bp, sp, mlp_bf16)

    return (y_p.reshape(bp, sp, d), y_s.reshape(bs, ss, d), wkv_p, shift_p,
            k_p[:, -WINDOW:], v_p[:, -WINDOW:],
            mkv[:, :, :MEM_W].reshape(DEPTH, bp, N_MEM, MEM_HEADS, MEM_HD),
            mkv[:, :, MEM_W:].reshape(DEPTH, bp, N_MEM, MEM_HEADS, MEM_HD),
            wkv_s, shift_s, win_k_s, win_v_s)
```

```python
import functools

import jax
import jax.numpy as jnp
from jax import lax
from jax.experimental import pallas as pl
from jax.experimental.pallas import tpu as pltpu

F32 = jnp.float32
BF16 = jnp.bfloat16

D_MODEL = 2048
DEPTH = 4
N_A = 2
CHUNK = 64
N_MEM = 256
MEM_HEADS = 4
MEM_HD = 128
MEM_W = MEM_HEADS * MEM_HD
TOK_W = D_MODEL - MEM_W
HEAD_D = 64
N_HEADS = TOK_W // HEAD_D
N_PAIRS = N_HEADS // 2
W_LORA = 64
A_LORA = 64
G_LORA = 224
A_SHIFT = 3 * TOK_W + W_LORA + A_LORA + G_LORA
LORA_OFF = 3 * TOK_W
GATE_OFF = LORA_OFF + 128
SHIFT_PAD = 4992
A_QMEM_OFF = 5120
A_IN_PAD = A_QMEM_OFF + MEM_W
SWA_KV = 4
SWA_G = N_HEADS // SWA_KV
WINDOW = 128
D_FF = 4 * D_MODEL
RMS_EPS = 1e-6
GN_EPS = 64e-5
NEG = -1e30
LANES = 128
VMEM_LIMIT = 56 * 1024 * 1024

_NT = (((1,), (1,)), ((), ()))
_TN = (((0,), (0,)), ((), ()))


def _cparams(*sem):
    return pltpu.CompilerParams(dimension_semantics=sem, vmem_limit_bytes=VMEM_LIMIT)


def _bdot(a, b):
    return jnp.dot(a.astype(BF16), b.astype(BF16), preferred_element_type=F32)


def _bdot_nt(a, b):
    return lax.dot_general(a.astype(BF16), b.astype(BF16), _NT, preferred_element_type=F32)


def _bdot_tn(a, b):
    return lax.dot_general(a.astype(BF16), b.astype(BF16), _TN, preferred_element_type=F32)


def _rms(x, g):
    return x * lax.rsqrt(jnp.mean(x * x, axis=-1, keepdims=True) + RMS_EPS) * g


def _norm_matmul_body(x_ref, g_ref, w_ref, o_ref, xn_ref):
    @pl.when(pl.program_id(1) == 0)
    def _():
        xn_ref[...] = _rms(x_ref[...], g_ref[...]).astype(BF16)

    o_ref[...] = jnp.dot(xn_ref[...], w_ref[...], preferred_element_type=F32).astype(o_ref.dtype)


def _proj_tiles(m, n):
    tm = min(m, 1024)
    for tn in (1408, 1024, 512):
        if n % tn == 0:
            return tm, tn
    raise ValueError(f"unsupported projection width {n}")


def norm_matmul(x, g, gl, w, wl):
    m, d = x.shape
    n = w.shape[2]
    tm, tn = _proj_tiles(m, n)
    return pl.pallas_call(
        _norm_matmul_body,
        grid=(m // tm, n // tn),
        in_specs=[pl.BlockSpec((tm, d), lambda i, j: (i, 0)),
                  pl.BlockSpec((None, 1, d), lambda i, j: (gl, 0, 0)),
                  pl.BlockSpec((None, d, tn), lambda i, j: (wl, 0, j))],
        out_specs=pl.BlockSpec((tm, tn), lambda i, j: (i, j)),
        out_shape=jax.ShapeDtypeStruct((m, n), F32),
        scratch_shapes=[pltpu.VMEM((tm, d), BF16)],
        compiler_params=_cparams("parallel", "arbitrary"),
        name="norm_matmul",
    )(x, g, w)


def norm_matmul_layers(x, g, w):
    m, d = x.shape
    nl, _, n = w.shape
    _, tn = _proj_tiles(m, n)
    return pl.pallas_call(
        _norm_matmul_body,
        grid=(nl, n // tn),
        in_specs=[pl.BlockSpec((m, d), lambda l, j: (0, 0)),
                  pl.BlockSpec((None, 1, d), lambda l, j: (l, 0, 0)),
                  pl.BlockSpec((None, d, tn), lambda l, j: (l, 0, j))],
        out_specs=pl.BlockSpec((None, m, tn), lambda l, j: (l, 0, j)),
        out_shape=jax.ShapeDtypeStruct((nl, m, n), F32),
        scratch_shapes=[pltpu.VMEM((m, d), BF16)],
        compiler_params=_cparams("parallel", "arbitrary"),
        name="norm_matmul_layers",
    )(x, g, w)


def _mix_out_body(tok_ref, mem_ref, wt_ref, wm_ref, g_ref, x_ref, o_ref):
    y = jnp.dot(tok_ref[...], wt_ref[...], preferred_element_type=F32)
    y = y + jnp.dot(mem_ref[...], wm_ref[...], preferred_element_type=F32)
    o_ref[...] = x_ref[...] + _rms(y, g_ref[...])


def mix_out(tok, mem, w_out, wl, g, gl, x):
    m, d = x.shape
    tm = min(m, 512)
    return pl.pallas_call(
        _mix_out_body,
        grid=(m // tm,),
        in_specs=[pl.BlockSpec((tm, TOK_W), lambda i: (i, 0)),
                  pl.BlockSpec((tm, MEM_W), lambda i: (i, 0)),
                  pl.BlockSpec((None, TOK_W, d), lambda i: (wl, 0, 0)),
                  pl.BlockSpec((None, MEM_W, d), lambda i: (wl, TOK_W // MEM_W, 0)),
                  pl.BlockSpec((None, 1, d), lambda i: (gl, 0, 0)),
                  pl.BlockSpec((tm, d), lambda i: (i, 0))],
        out_specs=pl.BlockSpec((tm, d), lambda i: (i, 0)),
        out_shape=jax.ShapeDtypeStruct((m, d), F32),
        compiler_params=_cparams("parallel"),
        name="mix_out",
    )(tok, mem, w_out, w_out, g, x)


def _mlp_body(x_ref, g1_ref, wu_ref, wd_ref, g2_ref, o_ref, *rest):
    xn_ref = rest[-1]
    f = pl.program_id(1)

    @pl.when(f == 0)
    def _():
        xn_ref[...] = _rms(x_ref[...], g1_ref[...]).astype(BF16)
        o_ref[...] = jnp.zeros_like(o_ref)

    wu = wu_ref[...].astype(BF16)
    wd = wd_ref[...].astype(BF16)
    if len(rest) == 3:
        rest[0][...] = wu
        rest[1][...] = wd
    h = jnp.maximum(jnp.dot(xn_ref[...], wu, preferred_element_type=F32), 0.0)
    o_ref[...] += jnp.dot((h * h).astype(BF16), wd, preferred_element_type=F32)

    @pl.when(f == pl.num_programs(1) - 1)
    def _():
        o_ref[...] = x_ref[...] + _rms(o_ref[...], g2_ref[...])


def mlp(x, g1, w_up, w_down, g2, l, wl):
    m, d = x.shape
    ff = w_up.shape[2]
    emit = w_up.dtype != BF16
    tm, tf = (min(m, 1024), 512) if emit else (min(m, 512), 1024)
    rows_mode = pl.Buffered(1) if emit else None
    out_specs = [pl.BlockSpec((tm, d), lambda i, f: (i, 0), pipeline_mode=rows_mode)]
    out_shape = [jax.ShapeDtypeStruct((m, d), F32)]
    if emit:
        out_specs += [pl.BlockSpec((d, tf), lambda i, f: (0, f)), pl.BlockSpec((tf, d), lambda i, f: (f, 0))]
        out_shape += [jax.ShapeDtypeStruct((d, ff), BF16), jax.ShapeDtypeStruct((ff, d), BF16)]
    outs = pl.pallas_call(
        _mlp_body,
        grid=(m // tm, ff // tf),
        in_specs=[pl.BlockSpec((tm, d), lambda i, f: (i, 0), pipeline_mode=rows_mode),
                  pl.BlockSpec((None, 1, d), lambda i, f: (l, 0, 0)),
                  pl.BlockSpec((None, d, tf), lambda i, f: (wl, 0, f)),
                  pl.BlockSpec((None, tf, d), lambda i, f: (wl, f, 0)),
                  pl.BlockSpec((None, 1, d), lambda i, f: (l, 0, 0))],
        out_specs=out_specs,
        out_shape=out_shape,
        scratch_shapes=[pltpu.VMEM((tm, d), BF16)],
        compiler_params=_cparams("parallel", "arbitrary"),
        name="mlp",
    )(x, g1, w_up, w_down, g2)
    return outs if emit else outs[0]


def _mem_attn_body(q_ref, k_ref, v_ref, o_ref):
    scale = MEM_HD ** -0.5
    nb = k_ref.shape[0]
    rq = q_ref.shape[0] // nb
    sls = [slice(h * MEM_HD, (h + 1) * MEM_HD) for h in range(MEM_HEADS)]
    rows = [slice(b * rq, (b + 1) * rq) for b in range(nb)]
    head = ((lambda ref, b, h: ref[b, :, h, :]) if len(k_ref.shape) == 4
            else (lambda ref, b, h: ref[b, :, sls[h]]))
    probs = [(b, h) for b in range(nb) for h in range(MEM_HEADS)]
    s = [_bdot_nt(q_ref[rows[b], sls[h]], head(k_ref, b, h)) * scale for b, h in probs]
    e = [jnp.exp(t - jnp.max(t, axis=-1, keepdims=True)) for t in s]
    o = [_bdot(e[n], head(v_ref, b, h)) * (1.0 / jnp.sum(e[n], axis=-1, keepdims=True))
         for n, (b, h) in enumerate(probs)]
    for n, (b, h) in enumerate(probs):
        o_ref[rows[b], sls[h]] = o[n].astype(o_ref.dtype)


def mem_attn(proj, q_col_block, mem_k, mem_v, l, batch, rows_per_seq):
    m = proj.shape[0]
    if mem_k.ndim == 5:
        nb = _batch_block(batch, 4)
        tr = nb * rows_per_seq
        blk = (None, nb, N_MEM, MEM_HEADS, MEM_HD)
        k_spec = pl.BlockSpec(blk, lambda i: (l, i, 0, 0, 0))
        v_spec = k_spec
    else:
        tr = min(rows_per_seq, 512)
        tiles_per_seq = rows_per_seq // tr
        blk = (None, 1, N_MEM, MEM_W)
        k_spec = pl.BlockSpec(blk, lambda i: (l, i // tiles_per_seq, 0, 0))
        v_spec = pl.BlockSpec(blk, lambda i: (l, i // tiles_per_seq, 0, 1))
    return pl.pallas_call(
        _mem_attn_body,
        grid=(m // tr,),
        in_specs=[pl.BlockSpec((tr, MEM_W), lambda i: (i, q_col_block)), k_spec, v_spec],
        out_specs=pl.BlockSpec((tr, MEM_W), lambda i: (i, 0)),
        out_shape=jax.ShapeDtypeStruct((m, MEM_W), BF16),
        compiler_params=_cparams("parallel"),
        name="mem_attn",
    )(proj, mem_k, mem_v)


def _swa_core(units, q_tile, k_tiles, v_tiles, valid_of, sink_ref, o_store, lq):
    lane = lax.broadcasted_iota(jnp.int32, (lq, LANES), 1)
    low = lane < HEAD_D
    probs = [(u, j) for u in units for j in range(SWA_KV)]
    s = []
    for u, j in probs:
        xs = []
        for i in range(SWA_G // 2):
            pair = q_tile(u, (j * SWA_G + 2 * i) * HEAD_D) * (HEAD_D ** -0.5)
            xs.append(jnp.where(low, pair, 0.0).astype(BF16))
            xs.append(jnp.where(low, 0.0, pair).astype(BF16))
        x = jnp.concatenate(xs, axis=0)
        kj = jnp.concatenate([t.astype(BF16) for t in k_tiles(u, j)], axis=0)
        sj = lax.dot_general(x, kj, _NT, preferred_element_type=F32)
        valid = valid_of(u)
        s.append(sj if valid is None else jnp.where(valid, sj, NEG))
    e, inv = [], []
    for n, (u, j) in enumerate(probs):
        sink = sink_ref[j]
        mx = jnp.maximum(jnp.max(s[n], axis=-1, keepdims=True), sink)
        en = jnp.exp(s[n] - mx)
        inv.append(1.0 / (jnp.sum(en, axis=-1, keepdims=True) + jnp.exp(sink - mx)))
        e.append(en.astype(BF16))
    o = [jnp.dot(e[n], jnp.concatenate([t.astype(BF16) for t in v_tiles(u, j)], axis=0),
                 preferred_element_type=F32) * inv[n] for n, (u, j) in enumerate(probs)]
    for n, (u, j) in enumerate(probs):
        for i in range(SWA_G // 2):
            top = o[n][(2 * i) * lq:(2 * i + 1) * lq]
            bot = o[n][(2 * i + 1) * lq:(2 * i + 2) * lq]
            o_store(u, (j * SWA_G + 2 * i) * HEAD_D, jnp.where(low, top, bot))


def _swa_prompt_body(q_ref, ka, kb, va, vb, sink_ref, o_ref):
    c2 = pl.program_id(1)
    col = lax.broadcasted_iota(jnp.int32, (SWA_G * CHUNK, 3 * CHUNK), 1)
    units = [(b, h) for b in range(q_ref.shape[0]) for h in range(2)]
    rows = lambda h: slice(h * CHUNK, (h + 1) * CHUNK)

    def tiles(ra, rb):
        def get(u, j):
            b, h = u
            ls = slice(j * LANES, (j + 1) * LANES)
            blocks = [ra[b, rows(0), ls], ra[b, rows(1), ls], rb[b, rows(0), ls], rb[b, rows(1), ls]]
            return blocks[h:h + 3]
        return get

    def store(u, c0, tile):
        o_ref[u[0], rows(u[1]), c0:c0 + LANES] = tile.astype(o_ref.dtype)

    _swa_core(units, lambda u, c0: q_ref[u[0], rows(u[1]), c0:c0 + LANES], tiles(ka, kb), tiles(va, vb),
              lambda u: col >= jnp.maximum(2 - (2 * c2 + u[1]), 0) * CHUNK,
              sink_ref, store, CHUNK)


def _batch_block(batch, want):
    return want if batch % want == 0 else 1


def swa_prompt(proj, kvd, sink_col, batch, seq):
    rows = 2 * CHUNK
    kd_w = SWA_KV * LANES
    nb = _batch_block(batch, 2)
    kv_prev = lambda col: pl.BlockSpec((nb, rows, kd_w), lambda b, c: (b, jnp.maximum(c - 1, 0), col))
    kv_same = lambda col: pl.BlockSpec((nb, rows, kd_w), lambda b, c: (b, c, col))
    kv3 = kvd.reshape(batch, seq, 2 * kd_w)
    out = pl.pallas_call(
        _swa_prompt_body,
        grid=(batch // nb, seq // rows),
        in_specs=[pl.BlockSpec((nb, rows, TOK_W), lambda b, c: (b, c, 0)),
                  kv_prev(0), kv_same(0), kv_prev(1), kv_same(1),
                  pl.BlockSpec((SWA_KV, SWA_G * CHUNK, 1), lambda b, c: (0, 0, 0))],
        out_specs=pl.BlockSpec((nb, rows, TOK_W), lambda b, c: (b, c, 0)),
        out_shape=jax.ShapeDtypeStruct((batch, seq, TOK_W), BF16),
        compiler_params=_cparams("parallel", "arbitrary"),
        name="swa_prompt",
    )(proj.reshape(batch, seq, -1), kv3, kv3, kv3, kv3, sink_col)
    return out.reshape(batch * seq, TOK_W)


def _swa_sample_body(q_ref, k_ref, v_ref, sink_ref, o_ref):
    nb, lq, _ = q_ref.shape

    def store(b, c0, tile):
        o_ref[b, :, c0:c0 + LANES] = tile.astype(o_ref.dtype)

    _swa_core(range(nb), lambda b, c0: q_ref[b, :, c0:c0 + LANES],
              lambda b, j: [k_ref[b, :, j * LANES:(j + 1) * LANES]],
              lambda b, j: [v_ref[b, :, j * LANES:(j + 1) * LANES]],
              lambda b: None, sink_ref, store, lq)


def swa_sample(proj, k_all, v_all, sink_col, batch, lq):
    keys = k_all.shape[1]
    kd_w = SWA_KV * LANES
    nb = _batch_block(batch, 4)
    out = pl.pallas_call(
        _swa_sample_body,
        grid=(batch // nb,),
        in_specs=[pl.BlockSpec((nb, lq, TOK_W), lambda b: (b, 0, 0)),
                  pl.BlockSpec((nb, keys, kd_w), lambda b: (b, 0, 0)),
                  pl.BlockSpec((nb, keys, kd_w), lambda b: (b, 0, 0)),
                  pl.BlockSpec((SWA_KV, SWA_G * lq, 1), lambda b: (0, 0, 0))],
        out_specs=pl.BlockSpec((nb, lq, TOK_W), lambda b: (b, 0, 0)),
        out_shape=jax.ShapeDtypeStruct((batch, lq, TOK_W), BF16),
        compiler_params=_cparams("parallel"),
        name="swa_sample",
    )(proj.reshape(batch, lq, -1), k_all, v_all, sink_col)
    return out.reshape(batch * lq, TOK_W)


def _pair_ones():
    r = lax.broadcasted_iota(jnp.int32, (LANES, LANES), 0)
    c = lax.broadcasted_iota(jnp.int32, (LANES, LANES), 1)
    return jnp.where((r < HEAD_D) == (c < HEAD_D), 1.0, 0.0).astype(BF16)


def _head_sum(x, ones_bd):
    return jnp.dot(x.astype(BF16), ones_bd, preferred_element_type=F32)


def _pre_stage(p_ref, prev_ref, wr, vf_ref, outs):
    has_vres = vf_ref is not None
    mu_ref, w0_ref, w2_ref, a0_ref, a2_ref, g2_ref = (wr[n] for n in ("mu", "w0", "w2", "a0", "a2", "g2"))
    kk_ref, ka_ref, rk_ref = wr["k_k"], wr["k_a"], wr["r_k"]
    if has_vres:
        v0_ref, v1_ref, v2_ref = wr["v0"], wr["v1"], wr["v2"]
    r_o, lw_o, k_o, v_o, kn_o, b_o, g_o, bonus_o = outs
    tr = p_ref.shape[0]
    row0 = lax.broadcasted_iota(jnp.int32, (tr, LANES), 0) == 0
    ones_bd = _pair_ones()

    def mixed(c0, width):
        outs_ = []
        for t in range(width // LANES):
            sl = slice(c0 + t * LANES, c0 + (t + 1) * LANES)
            p = p_ref[:, sl]
            pp = jnp.where(row0, prev_ref[:, sl], pltpu.roll(p, 1, 0))
            outs_.append(p + (pp - p) * mu_ref[:, sl])
        return outs_

    lora = mixed(LORA_OFF, LANES)[0]
    gate_in = jnp.concatenate(mixed(GATE_OFF, SHIFT_PAD - GATE_OFF), axis=1)
    w_pre = w0_ref[...] + _bdot(jnp.tanh(lora), w2_ref[...])
    lw_o[...] = -jnp.exp(-jax.nn.softplus(-w_pre) - 0.5)
    a = jax.nn.sigmoid(a0_ref[...] + _bdot(lora, a2_ref[...]))
    g_o[...] = _bdot(jax.nn.sigmoid(gate_in), g2_ref[...])

    r_t = mixed(0, TOK_W)
    k_t = mixed(TOK_W, TOK_W)
    v_t = mixed(2 * TOK_W, TOK_W)
    if has_vres:
        v_all = jnp.concatenate(v_t, axis=1)
        gate = jax.nn.sigmoid(v0_ref[...] + _bdot(_bdot(v_all, v1_ref[...]), v2_ref[...]))
        v_all = v_all + (vf_ref[...] - v_all) * gate
        v_t = [v_all[:, t * LANES:(t + 1) * LANES] for t in range(N_PAIRS)]
    for t in range(N_PAIRS):
        sl = slice(t * LANES, (t + 1) * LANES)
        kraw = k_t[t]
        kk = kraw * kk_ref[:, sl]
        kk = kk / jnp.maximum(jnp.sqrt(_head_sum(kk * kk, ones_bd)), 1e-12)
        a_t = a[:, sl]
        k = kraw * (1.0 + (a_t - 1.0) * ka_ref[:, sl])
        v = v_t[t]
        r_o[:, sl] = r_t[t]
        k_o[:, sl] = k
        v_o[:, sl] = v
        kn_o[:, sl] = kk
        b_o[:, sl] = kk * a_t
        bonus_o[:, sl] = _head_sum(r_t[t] * k * rk_ref[:, sl], ones_bd) * v


def _wkv_stage(mid, lg_ref, lb_ref, o_ref, s_ref, L):
    r_ref, lw_ref, k_ref, v_ref, kn_ref, b_ref, g_ref, bonus_ref = mid
    nch = r_ref.shape[0] // L
    W2 = 2 * L

    t_i = lax.broadcasted_iota(jnp.int32, (L, W2), 0)
    l_i = lax.broadcasted_iota(jnp.int32, (L, W2), 1)
    j_i = jnp.where(l_i >= L, l_i - L, l_i)
    strict = t_i > j_i
    incl = t_i >= j_i
    eye2 = jnp.where(t_i == j_i, 1.0, 0.0)
    levels = []
    s = 1
    while s < L:
        sh = s.bit_length() - 1
        levels.append(((t_i >> (sh + 1)) == (j_i >> (sh + 1)))
                      & (((t_i >> sh) & 1) == 1) & (((j_i >> sh) & 1) == 0))
        s *= 2
    first_t = l_i < L
    lane = lax.broadcasted_iota(jnp.int32, (L, LANES), 1)
    low = lane < HEAD_D
    rr = lax.broadcasted_iota(jnp.int32, (LANES, LANES), 0)
    cc = lax.broadcasted_iota(jnp.int32, (LANES, LANES), 1)
    bd = (rr < HEAD_D) == (cc < HEAD_D)
    ones_bd = jnp.where(bd, 1.0, 0.0).astype(BF16)
    tri = jnp.where(lax.broadcasted_iota(jnp.int32, (L, L), 0)
                    >= lax.broadcasted_iota(jnp.int32, (L, L), 1), 1.0, 0.0).astype(BF16)

    def cumsum_rows(x):
        hi = x.astype(BF16)
        r1 = x - hi.astype(F32)
        mid = r1.astype(BF16)
        lo = (r1 - mid.astype(F32)).astype(BF16)
        return (jnp.dot(tri, hi, preferred_element_type=F32) + jnp.dot(tri, mid, preferred_element_type=F32)
                + jnp.dot(tri, lo, preferred_element_type=F32))

    def stack_t(x):
        return jnp.concatenate([jnp.where(first_t, x, 0.0), jnp.where(first_t, 0.0, x)],
                               axis=0).astype(BF16)

    def stack_d(x):
        return jnp.concatenate([jnp.where(low, x, 0.0), jnp.where(low, 0.0, x)], axis=0).astype(BF16)

    def tdot(m, stacked):
        return jnp.dot(m.astype(BF16), stacked, preferred_element_type=F32)

    cum_all = [cumsum_rows(lw_ref[ch * L:(ch + 1) * L, :]) for ch in range(nch)]

    probs = [(ch, hp) for ch in range(nch) for hp in range(N_PAIRS)]
    pairs = range(len(probs))
    rows = [slice(ch * L, (ch + 1) * L) for ch, _ in probs]
    sls = [slice(hp * LANES, (hp + 1) * LANES) for _, hp in probs]
    a_t, r_t, sv, bh, kh, dec, cb, ck = [], [], [], [], [], [], [], []
    for p in pairs:
        rw, sl = rows[p], sls[p]
        lw = lw_ref[rw, sl]
        cum = cum_all[probs[p][0]][:, sl]
        cum_l = cum[L - 1:L, :]
        p_inv = jnp.exp(-cum)
        p_end = jnp.exp(cum_l - cum)
        b = b_ref[rw, sl]
        k = k_ref[rw, sl]
        a_t.append(-kn_ref[rw, sl] * jnp.exp(cum - lw))
        r_t.append(r_ref[rw, sl] * jnp.exp(cum))
        sv.append(stack_d(v_ref[rw, sl]))
        bh.append((b * p_end).astype(BF16))
        kh.append((k * p_end).astype(BF16))
        dec.append(jnp.exp(cum_l))
        x = jnp.concatenate([a_t[p], r_t[p]], axis=0).astype(BF16)
        cb.append(lax.dot_general(x, stack_d(b * p_inv), _NT, preferred_element_type=F32))
        ck.append(lax.dot_general(x, stack_d(k * p_inv), _NT, preferred_element_type=F32))
    a_ab = [jnp.where(strict, cb[p][:L], 0.0) for p in pairs]
    a_rb = [jnp.where(incl, cb[p][L:], 0.0).astype(BF16) for p in pairs]
    a_k = [jnp.concatenate([jnp.where(strict, ck[p][:L], 0.0), jnp.where(incl, ck[p][L:], 0.0)],
                           axis=0).astype(BF16) for p in pairs]
    kv = [jnp.dot(a_k[p], sv[p], preferred_element_type=F32) for p in pairs]
    t_m = [eye2 + jnp.where(levels[0], a_ab[p], 0.0) for p in pairs]
    for msk in levels[1:]:
        ta = [tdot(t_m[p], stack_t(jnp.where(msk, a_ab[p], 0.0))) for p in pairs]
        t_m = [t_m[p] + tdot(ta[p], stack_t(t_m[p])) for p in pairs]
    t_b = [t_m[p].astype(BF16) for p in pairs]
    w12 = [jnp.dot(t_b[p], jnp.concatenate([stack_d(a_t[p]), stack_d(kv[p][:L])], axis=1),
                   preferred_element_type=F32) for p in pairs]
    qq = [jnp.dot(a_rb[p], jnp.concatenate([stack_d(w12[p][:, :LANES]), stack_d(w12[p][:, LANES:])],
                                           axis=1), preferred_element_type=F32) for p in pairs]
    mm = [_bdot_tn(w12[p], bh[p]) for p in pairs]
    mk = [_bdot_tn(v_ref[rows[p], sls[p]], kh[p]) for p in pairs]
    y = []
    for p in pairs:
        hp = probs[p][1]
        s_p = s_ref[hp]
        q1 = r_t[p] + qq[p][:, :LANES]
        q2 = qq[p][:, LANES:] + kv[p][L:]
        m1 = jnp.where(bd, mm[p][:LANES], 0.0)
        m2 = jnp.where(bd, mm[p][LANES:] + mk[p], 0.0)
        y.append(_bdot_nt(q1, s_p) + q2)
        s_ref[hp] = s_p * dec[p] + _bdot(s_p, m1) + m2
    mean = [_head_sum(y[p], ones_bd) * (1.0 / HEAD_D) for p in pairs]
    dv = [y[p] - mean[p] for p in pairs]
    var = [_head_sum(dv[p] * dv[p], ones_bd) * (1.0 / HEAD_D) for p in pairs]
    for p in pairs:
        rw, sl = rows[p], sls[p]
        yn = dv[p] * lax.rsqrt(var[p] + GN_EPS) * lg_ref[:, sl] + lb_ref[:, sl]
        o_ref[rw, sl] = ((yn + bonus_ref[rw, sl]) * g_ref[rw, sl]).astype(o_ref.dtype)


_PRE_W = ("mu", "w0", "w2", "a0", "a2", "g2", "k_k", "k_a", "r_k")
_VRES_W = ("v0", "v1", "v2")
_MID = 8


def _rwkv_body(*refs, chunk, has_vres):
    names = _PRE_W + (_VRES_W if has_vres else ())
    it = iter(refs)
    p_ref, shift_ref = next(it), next(it)
    vf_ref = next(it) if has_vres else None
    wr = {n: next(it) for n in names}
    s0_ref, lg_ref, lb_ref = next(it), next(it), next(it)
    o_ref = next(it)
    v_out_ref = None if has_vres else next(it)
    s_out_ref, s_ref, prev_ref = next(it), next(it), next(it)
    mid = [next(it) for _ in range(_MID)]
    c = pl.program_id(1)

    @pl.when(c == 0)
    def _():
        zero = jnp.zeros((HEAD_D, HEAD_D), F32)
        for hp in range(N_PAIRS):
            top = jnp.concatenate([s0_ref[2 * hp], zero], axis=1)
            bot = jnp.concatenate([zero, s0_ref[2 * hp + 1]], axis=1)
            s_ref[hp] = jnp.concatenate([top, bot], axis=0)
        prev_ref[...] = shift_ref[...]

    _pre_stage(p_ref, prev_ref, wr, vf_ref, mid)
    prev_ref[...] = p_ref[p_ref.shape[0] - 1:, :]
    if v_out_ref is not None:
        v_out_ref[...] = mid[3][...]
    _wkv_stage(mid, lg_ref, lb_ref, o_ref, s_ref, chunk)

    @pl.when(c == pl.num_programs(1) - 1)
    def _():
        for hp in range(N_PAIRS):
            s_p = s_ref[hp]
            s_out_ref[2 * hp] = s_p[:HEAD_D, :HEAD_D]
            s_out_ref[2 * hp + 1] = s_p[HEAD_D:, HEAD_D:]


def rwkv_mix(proj, shift0, pw, v_first, s0, sl, batch, seq, chunk, rows):
    steps = seq // rows
    has_vres = v_first is not None
    names = _PRE_W + (_VRES_W if has_vres else ())
    full = lambda a: pl.BlockSpec(a.shape, lambda bi, c: (0,) * a.ndim)
    tile = pl.BlockSpec((rows, TOK_W), lambda bi, c: (bi * steps + c, 0))
    state = pl.BlockSpec((None, N_HEADS, HEAD_D, HEAD_D), lambda bi, c: (bi, 0, 0, 0))
    in_specs = [pl.BlockSpec((rows, SHIFT_PAD), lambda bi, c: (bi * steps + c, 0)),
                pl.BlockSpec((None, 1, SHIFT_PAD), lambda bi, c: (bi, 0, 0))]
    args = [proj, shift0]
    if has_vres:
        in_specs.append(tile)
        args.append(v_first)
    state_in = pl.BlockSpec((None, None, N_HEADS, HEAD_D, HEAD_D), lambda bi, c: (sl, bi, 0, 0, 0))
    in_specs += [full(pw[n]) for n in names] + [state_in, full(pw["lnx_g"]), full(pw["lnx_b"])]
    args += [pw[n] for n in names] + [s0, pw["lnx_g"], pw["lnx_b"]]
    out_specs = [tile] + ([] if has_vres else [tile]) + [state]
    out_shape = ([jax.ShapeDtypeStruct((batch * seq, TOK_W), BF16)]
                 + ([] if has_vres else [jax.ShapeDtypeStruct((batch * seq, TOK_W), F32)])
                 + [jax.ShapeDtypeStruct((batch, N_HEADS, HEAD_D, HEAD_D), F32)])
    outs = pl.pallas_call(
        functools.partial(_rwkv_body, chunk=chunk, has_vres=has_vres),
        grid=(batch, steps),
        in_specs=in_specs,
        out_specs=out_specs,
        out_shape=out_shape,
        scratch_shapes=[pltpu.VMEM((N_PAIRS, LANES, LANES), F32), pltpu.VMEM((1, SHIFT_PAD), F32)]
                       + [pltpu.VMEM((rows, TOK_W), F32)] * _MID,
        compiler_params=_cparams("parallel", "arbitrary"),
        name="rwkv_mix",
    )(*args)
    if has_vres:
        return outs[0], None, outs[1]
    return outs


def _dup_heads(w):
    lead = w.shape[:-1]
    w = w.reshape(lead + (SWA_KV, 1, HEAD_D))
    return jnp.broadcast_to(w, lead + (SWA_KV, 2, HEAD_D)).reshape(lead + (SWA_KV * LANES,))


def _undup_heads(x):
    lead = x.shape[:-1]
    return x.reshape(lead + (SWA_KV, 2, HEAD_D))[..., 0, :]


def _pad_cols(a, width):
    return jnp.pad(a, [(0, 0)] * (a.ndim - 1) + [(0, width - a.shape[-1])])


def _spread_cols(w):
    out = jnp.zeros(w.shape[:-1] + (A_IN_PAD,), w.dtype)
    out = lax.dynamic_update_slice_in_dim(out, w[..., :A_SHIFT], 0, axis=-1)
    return lax.dynamic_update_slice_in_dim(out, w[..., A_SHIFT:], A_QMEM_OFF, axis=-1)


def _sink_col(sinks, lq):
    return jnp.repeat(sinks.astype(F32).reshape(SWA_KV, SWA_G), lq, axis=1)[..., None]


def _trunk(x, mem_k, mem_v, wkv0, shift0, past_k, past_v, P, batch, seq, mlp_bf16):
    chunk = min(CHUNK, seq)
    rows_mix = min(4 * chunk, seq)
    new_wkv, new_shift = [], []
    v_first = None
    kvd = None
    for l in range(DEPTH):
        if l < N_A:
            pw = P["a"][l]
            proj = norm_matmul(x, P["ln_mix_pre"], l, P["a_w_in"], l)
            new_shift.append(proj.reshape(batch, seq, A_IN_PAD)[:, -1:, :A_SHIFT])
            tok, v, s_new = rwkv_mix(proj, _pad_cols(shift0[l].astype(F32), SHIFT_PAD), pw, v_first,
                                     wkv0.astype(F32), l, batch, seq, chunk, rows_mix)
            if l == 0:
                v_first = v
            new_wkv.append(s_new)
            q_col = A_QMEM_OFF // MEM_W
            w_out, wi = P["a_w_out"], l
        else:
            i = l - N_A
            if kvd is None:
                kvd = norm_matmul(x, P["kv_norm_g"], 0, P["w_kvd"], 0)
            proj = norm_matmul(x, P["ln_mix_pre"], l, P["b_w_in"], i)
            if past_k is None:
                tok = swa_prompt(proj, kvd, _sink_col(P["b_sinks"][i], CHUNK), batch, seq)
            else:
                kd_w = SWA_KV * LANES
                k_all = jnp.concatenate([_dup_heads(past_k.reshape(batch, -1, SWA_KV * HEAD_D)),
                                         kvd[:, :kd_w].reshape(batch, seq, kd_w)], axis=1)
                v_all = jnp.concatenate([_dup_heads(past_v.reshape(batch, -1, SWA_KV * HEAD_D)),
                                         kvd[:, kd_w:].reshape(batch, seq, kd_w)], axis=1)
                tok = swa_sample(proj, k_all, v_all, _sink_col(P["b_sinks"][i], seq), batch, seq)
            q_col = TOK_W // MEM_W
            w_out, wi = P["b_w_out"], i
        mem_o = mem_attn(proj, q_col, mem_k, mem_v, l, batch, seq)
        x = mix_out(tok, mem_o, w_out, wi, P["ln_mix_post"], l, x)
        if l in mlp_bf16:
            wu, wd = mlp_bf16[l]
            x = mlp(x, P["ln_mlp_pre"], wu, wd, P["ln_mlp_post"], l, 0)
        else:
            x, wu, wd = mlp(x, P["ln_mlp_pre"], P["w_up"], P["w_down"], P["ln_mlp_post"], l, l)
            mlp_bf16[l] = (wu[None], wd[None])
    kd_w = SWA_KV * LANES
    keep = min(WINDOW, seq)
    tail = kvd.reshape(batch, seq, 2 * kd_w)[:, seq - keep:]
    k_new = _undup_heads(tail[..., :kd_w]).reshape(batch, keep, SWA_KV, HEAD_D)
    v_new = _undup_heads(tail[..., kd_w:]).reshape(batch, keep, SWA_KV, HEAD_D)
    return x, jnp.stack(new_wkv), jnp.stack(new_shift), k_new, v_new


def _prep_params(ln_mix_pre, ln_mix_post, ln_mlp_pre, ln_mlp_post, mem_norm_g, w_mem_kv, w_up, w_down,
                 a_w_in, a_mu, a_w0, a_w2, a_a0, a_a2, a_g2, a_k_k, a_k_a, a_r_k, a_lnx_g, a_lnx_b,
                 a_v0, a_v1, a_v2, a_w_out, kv_norm_g, w_kv, b_w_in, b_sinks, b_w_out):
    d = D_MODEL
    zrow = lambda n: jnp.zeros((n, TOK_W), F32)
    a_params = []
    for i in range(N_A):
        pw = {
            "mu": _pad_cols(a_mu[i].reshape(1, A_SHIFT), SHIFT_PAD),
            "w0": a_w0[i].reshape(1, TOK_W), "a0": a_a0[i].reshape(1, TOK_W),
            "w2": jnp.concatenate([a_w2[i], zrow(A_LORA)], axis=0).astype(BF16),
            "a2": jnp.concatenate([zrow(W_LORA), a_a2[i]], axis=0).astype(BF16),
            "g2": jnp.concatenate([a_g2[i], zrow(SHIFT_PAD - A_SHIFT)], axis=0).astype(BF16),
            "k_k": a_k_k[i].reshape(1, TOK_W), "k_a": a_k_a[i].reshape(1, TOK_W),
            "r_k": a_r_k[i].reshape(1, TOK_W),
            "lnx_g": a_lnx_g[i].reshape(1, TOK_W), "lnx_b": a_lnx_b[i].reshape(1, TOK_W),
        }
        if i > 0:
            pw["v0"] = a_v0[i - 1].reshape(1, TOK_W)
            pw["v1"] = a_v1[i - 1].astype(BF16)
            pw["v2"] = a_v2[i - 1].astype(BF16)
        a_params.append(pw)
    kw = SWA_KV * HEAD_D
    gain = lambda g: g.reshape(-1, 1, d)
    P = {
        "a": a_params,
        "ln_mix_pre": gain(ln_mix_pre), "ln_mix_post": gain(ln_mix_post), "ln_mlp_pre": gain(ln_mlp_pre),
        "ln_mlp_post": gain(ln_mlp_post), "mem_norm_g": gain(mem_norm_g), "kv_norm_g": gain(kv_norm_g),
        "w_up": w_up, "w_down": w_down, "w_mem_kv": w_mem_kv.astype(BF16),
        "w_kvd": jnp.concatenate([_dup_heads(w_kv[:, :kw]), _dup_heads(w_kv[:, kw:])],
                                 axis=1).astype(BF16)[None],
        "a_w_in": _spread_cols(a_w_in.astype(BF16)),
        "a_w_out": a_w_out.astype(BF16),
        "b_w_in": b_w_in.astype(BF16), "b_sinks": b_sinks, "b_w_out": b_w_out.astype(BF16),
    }
    return P


def kernel(x_prompt, x_sample, mem_prompt, state_wkv, state_shift, cache_win_k, cache_win_v, cache_mem_k, cache_mem_v, ln_mix_pre, ln_mix_post, ln_mlp_pre, ln_mlp_post, mem_norm_g, w_mem_kv, w_up, w_down, a_w_in, a_mu, a_w0, a_w2, a_a0, a_a2, a_g2, a_k_k, a_k_a, a_r_k, a_lnx_g, a_lnx_b, a_v0, a_v1, a_v2, a_w_out, kv_norm_g, w_kv, b_w_in, b_sinks, b_w_out):
    bp, sp, d = x_prompt.shape
    bs, ss, _ = x_sample.shape
    P = _prep_params(ln_mix_pre, ln_mix_post, ln_mlp_pre, ln_mlp_post, mem_norm_g, w_mem_kv, w_up, w_down,
                     a_w_in, a_mu, a_w0, a_w2, a_a0, a_a2, a_g2, a_k_k, a_k_a, a_r_k, a_lnx_g, a_lnx_b,
                     a_v0, a_v1, a_v2, a_w_out, kv_norm_g, w_kv, b_w_in, b_sinks, b_w_out)

    mlp_bf16 = {}
    y_s, wkv_s, shift_s, k_s, v_s = _trunk(
        x_sample.reshape(bs * ss, d), cache_mem_k, cache_mem_v, state_wkv, state_shift, cache_win_k,
        cache_win_v, P, bs, ss, mlp_bf16)
    past = cache_win_k.shape[1]
    win_k_s = jnp.concatenate([cache_win_k, k_s], axis=1)[:, -past:]
    win_v_s = jnp.concatenate([cache_win_v, v_s], axis=1)[:, -past:]

    mkv = norm_matmul_layers(mem_prompt.reshape(bp * N_MEM, d), P["mem_norm_g"], P["w_mem_kv"])
    mkv4 = mkv.reshape(DEPTH, bp, N_MEM, 2 * MEM_W)
    wkv0 = jnp.zeros((N_A, bp, N_HEADS, HEAD_D, HEAD_D), F32)
    shift0 = jnp.zeros((N_A, bp, 1, A_SHIFT), F32)
    y_p, wkv_p, shift_p, k_p, v_p = _trunk(
        x_prompt.reshape(bp * sp, d), mkv4, mkv4, wkv0, shift0, None, None, P, bp, sp, mlp_bf16)

    return (y_p.reshape(bp, sp, d), y_s.reshape(bs, ss, d), wkv_p, shift_p,
            k_p[:, -WINDOW:], v_p[:, -WINDOW:],
            mkv[:, :, :MEM_W].reshape(DEPTH, bp, N_MEM, MEM_HEADS, MEM_HD),
            mkv[:, :, MEM_W:].reshape(DEPTH, bp, N_MEM, MEM_HEADS, MEM_HD),
            wkv_s, shift_s, win_k_s, win_v_s)
```
